```python
import math
import jax, jax.numpy as jnp
from jax import lax
import numpy as np

D_MODEL = 1024
BATCH = 8
SEQ = 2048
DEPTH = 4
DEC_BATCH = 128
DEC_SEQ = 8
PAST_LEN = 16384
PAGE_SIZE = 128

PLE_DIM = 256
BRANCH_W = 256
N_BRANCH = 4
GLA_HEADS = 4
GLA_DK = 32
GLA_DV = 64
GLA_RANK = 16
GLA_TAU = 16.0
GDN_HEADS = 4
GDN_DK = 64
GDN_DV = 64
GDN_CONV = 4
CM_GROUPS = 4
CM_CHUNK = 128
SC_WIDTH = 3
LINEAR_CHUNK = 64
D_FF = ((8 * D_MODEL // 3 + 255) // 256) * 256
EPS = 1e-6
IN_SIZES = (GLA_HEADS * GLA_DK, GLA_HEADS * GLA_DK, GLA_HEADS * GLA_DV, GLA_HEADS * GLA_DV, GLA_RANK,
            GDN_HEADS * GDN_DK, GDN_HEADS * GDN_DK, GDN_HEADS * GDN_DV, GDN_HEADS * GDN_DV, GDN_HEADS, GDN_HEADS,
            BRANCH_W, BRANCH_W,
            BRANCH_W, BRANCH_W, BRANCH_W)
IN_WIDTH = sum(IN_SIZES)

kernel_name = 'hybrid_gla_gdn_chunkmlp_shortconv_step'


def _rmsnorm(x, g):
    xf = x.astype(jnp.float32)
    y = xf * lax.rsqrt(jnp.mean(xf * xf, axis=-1, keepdims=True) + EPS)
    return (y * g.astype(jnp.float32)).astype(x.dtype)


def _layernorm(x, g, b):
    xf = x.astype(jnp.float32)
    mu = jnp.mean(xf, axis=-1, keepdims=True)
    var = jnp.mean(jnp.square(xf - mu), axis=-1, keepdims=True)
    y = (xf - mu) * lax.rsqrt(var + EPS)
    return (y * g.astype(jnp.float32) + b.astype(jnp.float32)).astype(x.dtype)


def _l2norm(x):
    return x * lax.rsqrt(jnp.sum(x * x, axis=-1, keepdims=True) + EPS)


def _causal_dwconv(x, buf, w):
    width = w.shape[0]
    T = x.shape[1]
    xp = jnp.concatenate([buf.astype(x.dtype), x], axis=1)
    y = w[0] * xp[:, 0:T]
    for j in range(1, width):
        y = y + w[j] * xp[:, j:j + T]
    return y, xp[:, xp.shape[1] - (width - 1):]


def _split_cols(proj):
    idx = []
    acc = 0
    for s in IN_SIZES[:-1]:
        acc += s
        idx.append(acc)
    return jnp.split(proj, idx, axis=-1)


def _chunk_len(T):
    return math.gcd(T, LINEAR_CHUNK)


def _gla_chunked(q, k, v, log_a, S0):
    B, T, H, K = q.shape
    L = _chunk_len(T)
    N = T // L
    q, k, v, log_a = [a.reshape(B, N, L, H, a.shape[-1]) for a in (q, k, v, log_a)]
    b = jnp.cumsum(log_a, axis=2)
    b_last = b[:, :, -1:]
    q_d = q * jnp.exp(b)
    k_d = k * jnp.exp(-b)
    k_e = k * jnp.exp(b_last - b)
    mask = jnp.tril(jnp.ones((L, L), dtype=bool))
    A = jnp.where(mask, jnp.einsum('bnlhk,bnmhk->bnhlm', q_d, k_d), 0.0)
    o_intra = jnp.einsum('bnhlm,bnmhv->bnlhv', A, v)
    dec = jnp.exp(b[:, :, -1])
    dS = jnp.einsum('bnlhk,bnlhv->bnhkv', k_e, v)

    def step(S, inp):
        qd, dc, ds = inp
        o = jnp.einsum('blhk,bhkv->blhv', qd, S)
        return dc[..., None] * S + ds, o

    S_fin, o_inter = lax.scan(step, S0, (jnp.moveaxis(q_d, 1, 0), jnp.moveaxis(dec, 1, 0), jnp.moveaxis(dS, 1, 0)))
    o = o_intra + jnp.moveaxis(o_inter, 0, 1)
    return o.reshape(B, T, H, v.shape[-1]), S_fin


def _gdn_chunked(q, k, v, g, beta, S0):
    B, T, H, K = q.shape
    V = v.shape[-1]
    L = _chunk_len(T)
    N = T // L
    qh = jnp.transpose(q.reshape(B, N, L, H, K), (0, 1, 3, 2, 4))
    kh = jnp.transpose(k.reshape(B, N, L, H, K), (0, 1, 3, 2, 4))
    vh = jnp.transpose(v.reshape(B, N, L, H, V), (0, 1, 3, 2, 4))
    Gh = jnp.cumsum(jnp.moveaxis(g.reshape(B, N, L, H), -1, 2), axis=-1)
    bh = jnp.moveaxis(beta.reshape(B, N, L, H), -1, 2)
    incl = jnp.tril(jnp.ones((L, L), dtype=bool))
    strict = jnp.tril(jnp.ones((L, L), dtype=bool), -1)
    diff = Gh[..., :, None] - Gh[..., None, :]
    decay = jnp.where(incl, jnp.exp(jnp.where(incl, diff, 0.0)), 0.0)
    kk = jnp.einsum('bnhlk,bnhmk->bnhlm', kh, kh)
    M = jnp.eye(L, dtype=q.dtype) + jnp.where(strict, bh[..., :, None] * decay * kk, 0.0)
    U = lax.linalg.triangular_solve(M, bh[..., None] * vh, left_side=True, lower=True, unit_diagonal=True)
    W = lax.linalg.triangular_solve(M, (bh * jnp.exp(Gh))[..., None] * kh, left_side=True, lower=True, unit_diagonal=True)
    qk = jnp.einsum('bnhlk,bnhmk->bnhlm', qh, kh) * decay
    q_g = qh * jnp.exp(Gh)[..., None]
    k_end = kh * jnp.exp(Gh[..., -1:] - Gh)[..., None]
    dec_end = jnp.exp(Gh[..., -1])

    def step(S, inp):
        Wn, Un, qkn, qgn, ken, dn = inp
        u = Un - jnp.einsum('bhlk,bhkv->bhlv', Wn, S)
        o = jnp.einsum('bhlk,bhkv->bhlv', qgn, S) + jnp.einsum('bhlm,bhmv->bhlv', qkn, u)
        S = dn[..., None, None] * S + jnp.einsum('bhlk,bhlv->bhkv', ken, u)
        return S, o

    mv = lambda a: jnp.moveaxis(a, 1, 0)
    S_fin, o = lax.scan(step, S0, (mv(W), mv(U), mv(qk), mv(q_g), mv(k_end), mv(dec_end)))
    o = jnp.transpose(o, (1, 0, 3, 2, 4)).reshape(B, T, H, V)
    return o, S_fin


def _chunk_mlp(u, v, ln_g, ln_b, ws, bs):
    B, T, C = v.shape
    vn = _layernorm(v, ln_g, ln_b)
    pad = (-T) % CM_CHUNK
    N = (T + pad) // CM_CHUNK
    vp = jnp.pad(vn, ((0, 0), (0, pad), (0, 0))).reshape(B, N, CM_CHUNK, CM_GROUPS, C // CM_GROUPS)
    wm = jnp.where(jnp.tril(jnp.ones((CM_CHUNK, CM_CHUNK), dtype=bool)), ws, 0.0)
    s = jnp.einsum('gts,bnsgc->bntgc', wm, vp) + jnp.transpose(bs)[:, :, None]
    s = s.reshape(B, N * CM_CHUNK, C)[:, :T]
    return u * s, vn


def _layer(h, pe, s_gla, s_gdn, b_gdn, b_sc, lp):
    f32 = jnp.float32
    B, T, _ = h.shape
    xn = _rmsnorm(h, lp['norm_mix'])
    (g_q, g_k, g_v, g_r, g_a, d_q, d_k, d_v, d_z, d_a, d_b,
     c_u, c_v, s_h, s_b, s_c) = _split_cols(xn @ lp['w_in'])
    q = g_q.reshape(B, T, GLA_HEADS, GLA_DK).astype(f32) * (GLA_DK ** -0.5)
    k = g_k.reshape(B, T, GLA_HEADS, GLA_DK).astype(f32)
    v = g_v.reshape(B, T, GLA_HEADS, GLA_DV).astype(f32)
    log_a = jax.nn.log_sigmoid((g_a @ lp['gla_wa2'] + lp['gla_ba']).astype(f32)).reshape(B, T, GLA_HEADS, GLA_DK) / GLA_TAU
    o, s_gla_new = _gla_chunked(q, k, v, log_a, s_gla.astype(f32))
    o_gla = (_rmsnorm(o, lp['gla_norm']).reshape(B, T, BRANCH_W) * jax.nn.silu(g_r.astype(f32))).astype(h.dtype)
    qkv, b_gdn_new = _causal_dwconv(jnp.concatenate([d_q, d_k, d_v], axis=-1), b_gdn, lp['gdn_conv_w'])
    qkv = jax.nn.silu(qkv.astype(f32))
    cq, ck, cv = jnp.split(qkv, [GDN_HEADS * GDN_DK, 2 * GDN_HEADS * GDN_DK], axis=-1)
    q = _l2norm(cq.reshape(B, T, GDN_HEADS, GDN_DK)) * (GDN_DK ** -0.5)
    k = _l2norm(ck.reshape(B, T, GDN_HEADS, GDN_DK))
    v = cv.reshape(B, T, GDN_HEADS, GDN_DV)
    g = -jnp.exp(lp['gdn_a_log'].astype(f32)) * jax.nn.softplus(d_a.astype(f32) + lp['gdn_dt_bias'].astype(f32))
    beta = jax.nn.sigmoid(d_b.astype(f32))
    o, s_gdn_new = _gdn_chunked(q, k, v, g, beta, s_gdn.astype(f32))
    o_gdn = (_rmsnorm(o, lp['gdn_norm']).reshape(B, T, BRANCH_W) * jax.nn.silu(d_z.astype(f32))).astype(h.dtype)
    o_cm, vn = _chunk_mlp(jax.nn.gelu(c_u), jax.nn.gelu(c_v), lp['cm_ln_g'], lp['cm_ln_b'], lp['cm_ws'], lp['cm_bs'])
    y_sc, b_sc_new = _causal_dwconv(s_c * s_h, b_sc, lp['sc_conv_w'])
    o_sc = s_b * y_sc
    branches = jnp.stack([o_gla, o_gdn, o_cm.astype(h.dtype), o_sc], axis=2)
    gates = jax.nn.sigmoid(xn @ lp['w_gate']).reshape(B, T, N_BRANCH, D_MODEL)
    merged = jnp.sum(jnp.einsum('btgc,gcd->btgd', branches, lp['w_branch']) * gates, axis=2)
    h = h + merged @ lp['w_o']
    xf = _rmsnorm(h, lp['norm_ffn'])
    h = h + (jax.nn.silu(xf @ lp['w_ffn_gate']) * (xf @ lp['w_ffn_up'])) @ lp['w_ffn_down']
    pg = jax.nn.sigmoid(_rmsnorm(h, lp['norm_ple']) @ lp['w_ple_gate'])
    h = h + pg * (pe @ lp['w_ple'])
    return h, s_gla_new.astype(h.dtype), s_gdn_new.astype(h.dtype), b_gdn_new, b_sc_new, vn


def setup_inputs(seed: int = 0) -> dict:
    key = jax.random.key(seed)
    ks = jax.random.split(key, 40)
    f32 = jnp.float32

    def nrm(k, shape, scale):
        return jax.random.normal(k, shape, f32) * scale

    def gain(k, shape):
        return 1.0 + 0.02 * jax.random.normal(k, shape, f32)

    dt = jnp.exp(jax.random.uniform(ks[20], (DEPTH, GDN_HEADS), f32, math.log(0.001), math.log(0.1)))
    return {
        'x_prompt': nrm(ks[0], (BATCH, SEQ, D_MODEL), 1.0),
        'x_sample': nrm(ks[1], (DEC_BATCH, DEC_SEQ, D_MODEL), 1.0),
        'state_gla': nrm(ks[2], (DEPTH, DEC_BATCH, GLA_HEADS, GLA_DK, GLA_DV), 0.5),
        'state_gdn': nrm(ks[3], (DEPTH, DEC_BATCH, GDN_HEADS, GDN_DK, GDN_DV), 0.1),
        'state_gdn_conv': nrm(ks[4], (DEPTH, DEC_BATCH, GDN_CONV - 1, 2 * GDN_HEADS * GDN_DK + GDN_HEADS * GDN_DV), 1.0),
        'state_sconv': nrm(ks[5], (DEPTH, DEC_BATCH, SC_WIDTH - 1, BRANCH_W), 1.0),
        'p_prompt': nrm(ks[6], (DEPTH, BATCH, SEQ, PLE_DIM), 1.0),
        'p_sample': nrm(ks[7], (DEPTH, DEC_BATCH, DEC_SEQ, PLE_DIM), 1.0),
        'norm_mix': gain(ks[8], (DEPTH, D_MODEL)),
        'w_in': nrm(ks[9], (DEPTH, D_MODEL, IN_WIDTH), D_MODEL ** -0.5),
        'gla_wa2': nrm(ks[10], (DEPTH, GLA_RANK, GLA_HEADS * GLA_DK), GLA_RANK ** -0.5),
        'gla_ba': nrm(ks[11], (DEPTH, GLA_HEADS * GLA_DK), 0.1),
        'gla_norm': gain(ks[12], (DEPTH, GLA_DV)),
        'gdn_conv_w': nrm(ks[13], (DEPTH, GDN_CONV, 2 * GDN_HEADS * GDN_DK + GDN_HEADS * GDN_DV), GDN_CONV ** -0.5),
        'gdn_a_log': jnp.log(jax.random.uniform(ks[14], (DEPTH, GDN_HEADS), f32, 1.0, 16.0)),
        'gdn_dt_bias': dt + jnp.log(-jnp.expm1(-dt)),
        'gdn_norm': gain(ks[15], (DEPTH, GDN_DV)),
        'cm_ln_g': gain(ks[16], (DEPTH, BRANCH_W)),
        'cm_ln_b': nrm(ks[17], (DEPTH, BRANCH_W), 0.02),
        'cm_ws': nrm(ks[18], (DEPTH, CM_GROUPS, CM_CHUNK, CM_CHUNK), CM_CHUNK ** -0.5),
        'cm_bs': 1.0 + nrm(ks[19], (DEPTH, CM_GROUPS, CM_CHUNK), 0.1),
        'sc_conv_w': nrm(ks[21], (DEPTH, SC_WIDTH, BRANCH_W), SC_WIDTH ** -0.5),
        'w_gate': nrm(ks[22], (DEPTH, D_MODEL, N_BRANCH * D_MODEL), D_MODEL ** -0.5),
        'w_branch': nrm(ks[23], (DEPTH, N_BRANCH, BRANCH_W, D_MODEL), BRANCH_W ** -0.5),
        'w_o': nrm(ks[24], (DEPTH, D_MODEL, D_MODEL), D_MODEL ** -0.5),
        'norm_ffn': gain(ks[25], (DEPTH, D_MODEL)),
        'w_ffn_gate': nrm(ks[26], (DEPTH, D_MODEL, D_FF), D_MODEL ** -0.5),
        'w_ffn_up': nrm(ks[27], (DEPTH, D_MODEL, D_FF), D_MODEL ** -0.5),
        'w_ffn_down': nrm(ks[28], (DEPTH, D_FF, D_MODEL), D_FF ** -0.5),
        'norm_ple': gain(ks[29], (DEPTH, D_MODEL)),
        'w_ple_gate': nrm(ks[30], (DEPTH, D_MODEL, D_MODEL), D_MODEL ** -0.5),
        'w_ple': nrm(ks[31], (DEPTH, PLE_DIM, D_MODEL), PLE_DIM ** -0.5),
        'norm_final': gain(ks[32], (D_MODEL,)),
    }


def reference(x_prompt, x_sample, state_gla, state_gdn, state_gdn_conv, state_sconv, p_prompt, p_sample,
              norm_mix, w_in, gla_wa2, gla_ba, gla_norm, gdn_conv_w, gdn_a_log, gdn_dt_bias, gdn_norm,
              cm_ln_g, cm_ln_b, cm_ws, cm_bs, sc_conv_w, w_gate, w_branch, w_o,
              norm_ffn, w_ffn_gate, w_ffn_up, w_ffn_down, norm_ple, w_ple_gate, w_ple, norm_final):
    dt = x_prompt.dtype
    bp = x_prompt.shape[0]
    z_gla = jnp.zeros((bp, GLA_HEADS, GLA_DK, GLA_DV), dt)
    z_gdn = jnp.zeros((bp, GDN_HEADS, GDN_DK, GDN_DV), dt)
    z_gdn_buf = jnp.zeros((bp, GDN_CONV - 1, state_gdn_conv.shape[-1]), dt)
    z_sc_buf = jnp.zeros((bp, SC_WIDTH - 1, BRANCH_W), dt)
    hp, hs = x_prompt, x_sample
    gla_p, gla_s, gdn_p, gdn_s, gc_p, gc_s, sc_p, sc_s, cv_s = [], [], [], [], [], [], [], [], []
    for i in range(DEPTH):
        lp = {'norm_mix': norm_mix[i], 'w_in': w_in[i], 'gla_wa2': gla_wa2[i], 'gla_ba': gla_ba[i],
              'gla_norm': gla_norm[i], 'gdn_conv_w': gdn_conv_w[i], 'gdn_a_log': gdn_a_log[i],
              'gdn_dt_bias': gdn_dt_bias[i], 'gdn_norm': gdn_norm[i], 'cm_ln_g': cm_ln_g[i], 'cm_ln_b': cm_ln_b[i],
              'cm_ws': cm_ws[i], 'cm_bs': cm_bs[i], 'sc_conv_w': sc_conv_w[i], 'w_gate': w_gate[i],
              'w_branch': w_branch[i], 'w_o': w_o[i], 'norm_ffn': norm_ffn[i], 'w_ffn_gate': w_ffn_gate[i],
              'w_ffn_up': w_ffn_up[i], 'w_ffn_down': w_ffn_down[i], 'norm_ple': norm_ple[i],
              'w_ple_gate': w_ple_gate[i], 'w_ple': w_ple[i]}
        hp, a, b, c, d, _ = _layer(hp, p_prompt[i], z_gla, z_gdn, z_gdn_buf, z_sc_buf, lp)
        gla_p.append(a); gdn_p.append(b); gc_p.append(c); sc_p.append(d)
        hs, a, b, c, d, e = _layer(hs, p_sample[i], state_gla[i], state_gdn[i], state_gdn_conv[i], state_sconv[i], lp)
        gla_s.append(a); gdn_s.append(b); gc_s.append(c); sc_s.append(d); cv_s.append(e)
    y_prompt = _rmsnorm(hp, norm_final)
    y_sample = _rmsnorm(hs, norm_final)
    return (y_prompt, y_sample, jnp.stack(gla_p), jnp.stack(gla_s), jnp.stack(gdn_p), jnp.stack(gdn_s),
            jnp.stack(gc_p), jnp.stack(gc_s), jnp.stack(sc_p), jnp.stack(sc_s), jnp.stack(cv_s))
```

```python
import functools

import jax
import jax.numpy as jnp
from jax import lax
from jax.experimental import pallas as pl
from jax.experimental.pallas import tpu as pltpu

F32 = jnp.float32
BF16 = jnp.bfloat16
HI = lax.Precision.HIGHEST

D_MODEL = 1024
PLE_DIM = 256
BRANCH_W = 256
N_BRANCH = 4
GLA_HEADS = 4
GLA_DK = 32
GLA_DV = 64
GLA_RANK = 16
GLA_TAU = 16.0
GDN_HEADS = 4
GDN_DK = 64
GDN_DV = 64
GDN_CONV = 4
CM_GROUPS = 4
CM_CHUNK = 128
SC_WIDTH = 3
LINEAR_CHUNK = 64
D_FF = 2816
EPS = 1e-6

ROWS = 64
GLA_QK = GLA_HEADS * GLA_DK
LANE = 128
CONV_W = 3 * BRANCH_W + BRANCH_W
HIST = 8

C_GQ, C_GK, C_GV, C_GR, C_GA = 0, 128, 256, 512, 768
C_DQKV, C_DZ, C_DA, C_DB = 896, 1664, 1920, 2176
C_CU, C_CV = 2432, 2688
C_SH, C_SB, C_SC = 2944, 3200, 3456
W1_COLS = 3712
GLA_OUT = 896
GDN_OUT = 2048
CM_OUT = 512

VMEM_LIMIT = 56 * 1024 * 1024
TOKEN_TILE = 256


def _dot(a, b, precision=None):
    return jnp.dot(a, b, preferred_element_type=F32, precision=precision)


def _dot_nt(a, b, precision=None):
    return lax.dot_general(a, b, (((1,), (1,)), ((), ())), preferred_element_type=F32, precision=precision)


def _dot_tn(a, b, precision=None):
    return lax.dot_general(a, b, (((0,), (0,)), ((), ())), preferred_element_type=F32, precision=precision)


def _sigmoid(x):
    return 1.0 / (1.0 + jnp.exp(-x))


def _silu(x):
    return x * _sigmoid(x)


def _softplus(x):
    return jnp.maximum(x, 0.0) + jnp.log1p(jnp.exp(-jnp.abs(x)))


def _gelu_tanh(x):
    return 0.5 * x * (1.0 + jnp.tanh(0.7978845608028654 * (x + 0.044715 * (x * x * x))))


def _rms(x, w):
    return x * lax.rsqrt(jnp.mean(x * x, axis=-1, keepdims=True) + EPS) * w


def _idiv(x, n):
    assert n & (n - 1) == 0
    return lax.shift_right_logical(x, n.bit_length() - 1)


def _imod(x, n):
    assert n & (n - 1) == 0
    return lax.bitwise_and(x, n - 1)


def _const_spec(shape):
    return pl.BlockSpec(shape, lambda *_: (0,) * len(shape))


def _inproj_kernel(h_ref, nw_ref, w_ref, wa2_ref, ba_ref, alog_ref, dtb_ref, lng_ref, lnb_ref,
                   gla_ref, gdn_ref, cm_ref):
    xn = _rms(h_ref[...], nw_ref[...]).astype(BF16)
    proj = _dot(xn, w_ref[...])
    gla_ref[:, 0:128] = proj[:, C_GQ:C_GQ + 128] * (GLA_DK ** -0.5)
    gla_ref[:, 128:512] = proj[:, C_GK:C_GR]
    gla_ref[:, 512:768] = _silu(proj[:, C_GR:C_GR + 256])
    za = _dot(proj[:, C_GA:C_GA + LANE].astype(BF16), wa2_ref[...]) + ba_ref[...]
    gla_ref[:, 768:896] = -_softplus(-za) * (1.0 / GLA_TAU)
    gdn_ref[:, 0:768] = proj[:, C_DQKV:C_DQKV + 768]
    gdn_ref[:, 768:1024] = proj[:, C_SC:C_SC + 256] * proj[:, C_SH:C_SH + 256]
    gdn_ref[:, 1024:1280] = _silu(proj[:, C_DZ:C_DZ + 256])
    gdn_ref[:, 1280:1536] = proj[:, C_SB:C_SB + 256]
    gdn_ref[:, 1536:1792] = -jnp.exp(alog_ref[...]) * _softplus(proj[:, C_DA:C_DA + 256] + dtb_ref[...])
    gdn_ref[:, 1792:2048] = _sigmoid(proj[:, C_DB:C_DB + 256])
    cm_ref[:, 0:256] = _gelu_tanh(proj[:, C_CU:C_CU + 256])
    gv = _gelu_tanh(proj[:, C_CV:C_CV + 256])
    mu = jnp.mean(gv, axis=-1, keepdims=True)
    d = gv - mu
    var = jnp.mean(d * d, axis=-1, keepdims=True)
    cm_ref[:, 256:512] = d * lax.rsqrt(var + EPS) * lng_ref[...] + lnb_ref[...]


def _inproj(h, nw, w1, wa2, ba, alog, dtb, lng, lnb):
    ntok = h.shape[0]
    tm = TOKEN_TILE
    row = lambda n: pl.BlockSpec((tm, n), lambda i: (i, 0))
    return pl.pallas_call(
        _inproj_kernel,
        grid=(ntok // tm,),
        in_specs=[row(D_MODEL), _const_spec((1, D_MODEL)), _const_spec((D_MODEL, W1_COLS)),
                  _const_spec((LANE, GLA_QK)), _const_spec((1, GLA_QK)), _const_spec((1, 256)),
                  _const_spec((1, 256)), _const_spec((1, 256)), _const_spec((1, 256))],
        out_specs=[row(GLA_OUT), row(GDN_OUT), row(CM_OUT)],
        out_shape=[jax.ShapeDtypeStruct((ntok, GLA_OUT), F32), jax.ShapeDtypeStruct((ntok, GDN_OUT), F32),
                   jax.ShapeDtypeStruct((ntok, CM_OUT), F32)],
        compiler_params=pltpu.CompilerParams(dimension_semantics=("parallel",), vmem_limit_bytes=VMEM_LIMIT),
        name="inproj",
    )(h, nw, w1, wa2, ba, alog, dtb, lng, lnb)


def _seq_masks(nseq):
    seg = ROWS // nseq
    ri = lax.broadcasted_iota(jnp.int32, (ROWS, ROWS), 0)
    ci = lax.broadcasted_iota(jnp.int32, (ROWS, ROWS), 1)
    same = _idiv(ri, seg) == _idiv(ci, seg)
    return same, same & (ci <= ri), same & (ci < ri)


def _lane_group_mask(width, group, h):
    li = lax.broadcasted_iota(jnp.int32, (1, width), 1)
    return (_idiv(li, group) == h).astype(F32)


def _stack_heads(x, group, nheads):
    w = x.shape[1]
    return jnp.concatenate([x * _lane_group_mask(w, group, h) for h in range(nheads)], axis=0)


def _diag_blocks(y, group, nheads):
    r = y.shape[0] // nheads
    w = y.shape[1]
    out = y[0:r] * _lane_group_mask(w, group, 0)
    for h in range(1, nheads):
        out = out + y[h * r:(h + 1) * r] * _lane_group_mask(w, group, h)
    return out


def _widen(x, nseq):
    if nseq == 1:
        return x
    seg = ROWS // nseq
    ri = lax.broadcasted_iota(jnp.int32, (ROWS, 1), 0)
    return jnp.concatenate([x * (_idiv(ri, seg) == j).astype(F32) for j in range(nseq)], axis=1)


def _block_diag_mask(rows, cols, rgroup, cgroup):
    ri = lax.broadcasted_iota(jnp.int32, (rows, cols), 0)
    ci = lax.broadcasted_iota(jnp.int32, (rows, cols), 1)
    return (_idiv(ri, rgroup) == _idiv(ci, cgroup)).astype(F32)


def _group_mean(x, group):
    w = x.shape[1]
    avg = _block_diag_mask(w, w, group, group) * (1.0 / group)
    return _dot(x, avg, HI)


def _gla_kernel(p_ref, s0_ref, gn_ref, o_ref, st_ref, *, nseq):
    @pl.when(pl.program_id(1) == 0)
    def _():
        st_ref[...] = s0_ref[...]

    seg = ROWS // nseq
    p = p_ref[...]
    q, k, v, rs, la = p[:, 0:128], p[:, 128:256], p[:, 256:512], p[:, 512:768], p[:, 768:896]
    same, tri, _ = _seq_masks(nseq)
    b = _dot(tri.astype(F32), la, HI)
    btot = _dot(same.astype(F32), la, HI)
    qd = q * jnp.exp(b)
    kd = k * jnp.exp(-b)
    ke = k * jnp.exp(btot - b)
    a = _dot_nt(_stack_heads(qd, GLA_DK, GLA_HEADS), kd, HI)
    a = jnp.where(jnp.concatenate([tri] * GLA_HEADS, axis=0), a, 0.0)
    o = _diag_blocks(_dot(a, v, HI), GLA_DV, GLA_HEADS)
    st = st_ref[...]
    o = o + _dot_nt(_widen(qd, nseq), st, HI)
    dec = jnp.exp(jnp.concatenate([btot[j * seg:j * seg + 1] for j in range(nseq)], axis=1))
    ri = lax.broadcasted_iota(jnp.int32, st.shape, 0)
    ci = lax.broadcasted_iota(jnp.int32, st.shape, 1)
    bd = (_idiv(_imod(ci, GLA_QK), GLA_DK) == _idiv(ri, GLA_DV)).astype(F32)
    st_ref[...] = st * dec + _dot_tn(v, _widen(ke, nseq), HI) * bd
    ms = _group_mean(o * o, GLA_DV)
    o_ref[...] = o * lax.rsqrt(ms + EPS) * gn_ref[...] * rs


def _gla(p_gla, s0w, gn, *, base_block, nouter, nchunk, nseq):
    sw = nseq * GLA_QK
    return pl.pallas_call(
        functools.partial(_gla_kernel, nseq=nseq),
        grid=(nouter, nchunk),
        in_specs=[pl.BlockSpec((ROWS, GLA_OUT), lambda o, c: (base_block + o * nchunk + c, 0)),
                  pl.BlockSpec((None, 256, sw), lambda o, c: (o, 0, 0)),
                  _const_spec((1, 256))],
        out_specs=[pl.BlockSpec((ROWS, 256), lambda o, c: (o * nchunk + c, 0)),
                   pl.BlockSpec((None, 256, sw), lambda o, c: (o, 0, 0))],
        out_shape=[jax.ShapeDtypeStruct((nouter * nchunk * ROWS, 256), F32),
                   jax.ShapeDtypeStruct((nouter, 256, sw), F32)],
        compiler_params=pltpu.CompilerParams(dimension_semantics=("arbitrary", "arbitrary"),
                                             vmem_limit_bytes=VMEM_LIMIT),
        name="gla_prompt" if nseq == 1 else "gla_sample",
    )(p_gla, s0w, gn)


def _unit_lower_inverse(a, nfactors):
    n = a.shape[0]
    eye = (lax.broadcasted_iota(jnp.int32, (n, n), 0) == lax.broadcasted_iota(jnp.int32, (n, n), 1)).astype(F32)
    inv = eye - a
    pw = a
    for _ in range(nfactors - 1):
        pw = _dot(pw, pw, HI)
        inv = inv + _dot(inv, pw, HI)
    return inv


def _gdn_kernel(p_ref, h0_ref, s0_ref, cw_ref, gn_ref, o_ref, osc_ref, st_ref, hist_ref, *, nseq):
    @pl.when(pl.program_id(1) == 0)
    def _():
        st_ref[...] = s0_ref[...]
        hist_ref[...] = h0_ref[...]

    seg = ROWS // nseq
    nh, hd = GDN_HEADS, GDN_DK
    p = p_ref[...]
    x = p[:, 0:CONV_W]
    zs, sb, g, beta = p[:, 1024:1280], p[:, 1280:1536], p[:, 1536:1792], p[:, 1792:2048]
    hist = hist_ref[...]
    tloc = _imod(lax.broadcasted_iota(jnp.int32, (ROWS, 1), 0), seg)
    cw = cw_ref[...]
    acc = cw[3:4] * x
    for d in range(1, GDN_CONV):
        prev = jnp.where(tloc < d, pltpu.roll(hist, (d - HIST) % ROWS, 0), pltpu.roll(x, d, 0))
        acc = acc + cw[3 - d:4 - d] * prev
    hist_ref[0:HIST] = x[ROWS - HIST:ROWS]
    osc_ref[...] = sb * acc[:, 768:1024]
    qkv = _silu(acc[:, 0:768])
    cq, ck, cv = qkv[:, 0:256], qkv[:, 256:512], qkv[:, 512:768]
    q = cq * lax.rsqrt(_group_mean(cq * cq, hd) * hd + EPS) * (hd ** -0.5)
    k = ck * lax.rsqrt(_group_mean(ck * ck, hd) * hd + EPS)

    same, tri, strict = _seq_masks(nseq)
    gc = _dot(tri.astype(F32), g, HI)
    gtot = _dot(same.astype(F32), g, HI)
    eg = jnp.exp(gc)
    tri4 = jnp.concatenate([tri] * nh, axis=0)
    strict4 = jnp.concatenate([strict] * nh, axis=0)
    kk = _dot_nt(_stack_heads(k, hd, nh), k, HI)
    qk = _dot_nt(_stack_heads(q, hd, nh), k, HI)
    gcol = jnp.concatenate([gc[:, h * hd:(h + 1) * hd] for h in range(nh)], axis=0)
    bcol = jnp.concatenate([beta[:, h * hd:(h + 1) * hd] for h in range(nh)], axis=0)
    sel_r = lax.broadcasted_iota(jnp.int32, (nh * ROWS, nh * hd), 0)
    sel_c = lax.broadcasted_iota(jnp.int32, (nh * ROWS, nh * hd), 1)
    grow = _dot_nt((sel_c == _idiv(sel_r, ROWS) * hd).astype(F32), gc, HI)
    decay = jnp.where(tri4, jnp.exp(jnp.where(tri4, gcol - grow, 0.0)), 0.0)
    amat = jnp.where(strict4, bcol * decay * kk, 0.0)
    nfac = max(1, (seg - 1).bit_length())
    tinv = jnp.concatenate([_unit_lower_inverse(amat[h * ROWS:(h + 1) * ROWS], nfac) for h in range(nh)], axis=0)
    uw = _dot(tinv, jnp.concatenate([beta * cv, beta * eg * k], axis=1), HI)
    u0 = _diag_blocks(uw[:, 0:256], hd, nh)
    w = _diag_blocks(uw[:, 256:512], hd, nh)
    st = st_ref[...]
    u = u0 - _dot(_widen(w, nseq), st, HI)
    o = _dot(_widen(q * eg, nseq), st, HI) + _diag_blocks(_dot(qk * decay, u, HI), hd, nh)
    kend = k * jnp.exp(gtot - gc)
    dn = jnp.exp(gtot)
    dn_tall = jnp.concatenate([jnp.broadcast_to(dn[j * seg:j * seg + 1], (nh * hd, nh * hd)) for j in range(nseq)],
                              axis=0)
    ri = lax.broadcasted_iota(jnp.int32, st.shape, 0)
    ci = lax.broadcasted_iota(jnp.int32, st.shape, 1)
    bd = (_idiv(_imod(ri, nh * hd), hd) == _idiv(ci, hd)).astype(F32)
    st_ref[...] = st * dn_tall + _dot_tn(_widen(kend, nseq), u, HI) * bd
    ms = _group_mean(o * o, GDN_DV)
    o_ref[...] = o * lax.rsqrt(ms + EPS) * gn_ref[...] * zs


def _gdn(p_gdn, hist0, s0t, cw, gn, *, base_block, nouter, nchunk, nseq):
    sr = nseq * 256
    return pl.pallas_call(
        functools.partial(_gdn_kernel, nseq=nseq),
        grid=(nouter, nchunk),
        in_specs=[pl.BlockSpec((ROWS, GDN_OUT), lambda o, c: (base_block + o * nchunk + c, 0)),
                  pl.BlockSpec((None, ROWS, CONV_W), lambda o, c: (o, 0, 0)),
                  pl.BlockSpec((None, sr, 256), lambda o, c: (o, 0, 0)),
                  _const_spec((GDN_CONV, CONV_W)), _const_spec((1, 256))],
        out_specs=[pl.BlockSpec((ROWS, 256), lambda o, c: (o * nchunk + c, 0)),
                   pl.BlockSpec((ROWS, 256), lambda o, c: (o * nchunk + c, 0)),
                   pl.BlockSpec((None, sr, 256), lambda o, c: (o, 0, 0))],
        out_shape=[jax.ShapeDtypeStruct((nouter * nchunk * ROWS, 256), F32),
                   jax.ShapeDtypeStruct((nouter * nchunk * ROWS, 256), F32),
                   jax.ShapeDtypeStruct((nouter, sr, 256), F32)],
        scratch_shapes=[pltpu.VMEM((ROWS, CONV_W), F32)],
        compiler_params=pltpu.CompilerParams(dimension_semantics=("arbitrary", "arbitrary"),
                                             vmem_limit_bytes=VMEM_LIMIT),
        name="gdn_prompt" if nseq == 1 else "gdn_sample",
    )(p_gdn, hist0, s0t, cw, gn)


def _cm_kernel(p_ref, ws_ref, bias_ref, o_ref, *, seg):
    r = p_ref.shape[0]
    p = p_ref[...]
    gu, vn = p[:, 0:256], p[:, 256:512]
    ri = _imod(lax.broadcasted_iota(jnp.int32, (CM_GROUPS * r, r), 0), r)
    ci = lax.broadcasted_iota(jnp.int32, (CM_GROUPS * r, r), 1)
    wm = jnp.where((_idiv(ri, seg) == _idiv(ci, seg)) & (ci <= ri), ws_ref[...], 0.0)
    s = _diag_blocks(_dot(wm, vn, HI), BRANCH_W // CM_GROUPS, CM_GROUPS) + bias_ref[...]
    o_ref[...] = gu * s


def _cm(p_cm, ws_stack, bias, *, base_block, nblocks, rows, seg):
    return pl.pallas_call(
        functools.partial(_cm_kernel, seg=seg),
        grid=(nblocks,),
        in_specs=[pl.BlockSpec((rows, CM_OUT), lambda i: (base_block + i, 0)),
                  _const_spec((CM_GROUPS * rows, rows)), _const_spec((rows, 256))],
        out_specs=pl.BlockSpec((rows, 256), lambda i: (i, 0)),
        out_shape=jax.ShapeDtypeStruct((nblocks * rows, 256), F32),
        compiler_params=pltpu.CompilerParams(dimension_semantics=("parallel",), vmem_limit_bytes=VMEM_LIMIT),
        name="cm_prompt" if seg == rows else "cm_sample",
    )(p_cm, ws_stack, bias)


def _merge_kernel(h_ref, b0_ref, b1_ref, b2_ref, b3_ref, nw_ref, wg_ref, wb_ref, wo_ref, o_ref):
    h = h_ref[...]
    xn = _rms(h, nw_ref[...]).astype(BF16)
    merged = None
    for gi, b_ref in enumerate((b0_ref, b1_ref, b2_ref, b3_ref)):
        gate = _sigmoid(_dot(xn, wg_ref[:, gi * D_MODEL:(gi + 1) * D_MODEL]))
        term = _dot(b_ref[...].astype(BF16), wb_ref[gi]) * gate
        merged = term if merged is None else merged + term
    o_ref[...] = h + _dot(merged.astype(BF16), wo_ref[...])


def _merge(h, branches, nw, wg, wb, wo):
    ntok = h.shape[0]
    tm = TOKEN_TILE
    row = lambda n: pl.BlockSpec((tm, n), lambda i: (i, 0))
    return pl.pallas_call(
        _merge_kernel,
        grid=(ntok // tm,),
        in_specs=[row(D_MODEL), row(256), row(256), row(256), row(256), _const_spec((1, D_MODEL)),
                  _const_spec((D_MODEL, N_BRANCH * D_MODEL)), _const_spec((N_BRANCH, BRANCH_W, D_MODEL)),
                  _const_spec((D_MODEL, D_MODEL))],
        out_specs=row(D_MODEL),
        out_shape=jax.ShapeDtypeStruct((ntok, D_MODEL), F32),
        compiler_params=pltpu.CompilerParams(dimension_semantics=("parallel",), vmem_limit_bytes=VMEM_LIMIT),
        name="merge",
    )(h, *branches, nw, wg, wb, wo)


def _ffn_kernel(h_ref, pe_ref, nf_ref, wfg_ref, wfu_ref, wfd_ref, np_ref, wpg_ref, wp_ref, nfin_ref, o_ref, *, final):
    h = h_ref[...]
    xf = _rms(h, nf_ref[...]).astype(BF16)
    act = _silu(_dot(xf, wfg_ref[...])) * _dot(xf, wfu_ref[...])
    h = h + _dot(act.astype(BF16), wfd_ref[...])
    pg = _sigmoid(_dot(_rms(h, np_ref[...]).astype(BF16), wpg_ref[...]))
    h = h + pg * _dot(pe_ref[...].astype(BF16), wp_ref[...])
    o_ref[...] = _rms(h, nfin_ref[...]) if final else h


def _ffn(h, pe, nf, wfg, wfu, wfd, npl, wpg, wp, nfin, *, final):
    ntok = h.shape[0]
    tm = TOKEN_TILE
    row = lambda n: pl.BlockSpec((tm, n), lambda i: (i, 0))
    once = lambda shape: pl.BlockSpec(shape, lambda *_: (0,) * len(shape), pipeline_mode=pl.Buffered(1))
    return pl.pallas_call(
        functools.partial(_ffn_kernel, final=final),
        grid=(ntok // tm,),
        in_specs=[row(D_MODEL), row(PLE_DIM), _const_spec((1, D_MODEL)),
                  once((D_MODEL, D_FF)), once((D_MODEL, D_FF)), once((D_FF, D_MODEL)),
                  _const_spec((1, D_MODEL)), once((D_MODEL, D_MODEL)), once((PLE_DIM, D_MODEL)),
                  _const_spec((1, D_MODEL))],
        out_specs=row(D_MODEL),
        out_shape=jax.ShapeDtypeStruct((ntok, D_MODEL), F32),
        compiler_params=pltpu.CompilerParams(dimension_semantics=("parallel",), vmem_limit_bytes=VMEM_LIMIT),
        name="ffn_final" if final else "ffn",
    )(h, pe, nf, wfg, wfu, wfd, npl, wpg, wp, nfin)


def _perm_w_in(w_in):
    offs, acc = [], 0
    for s in (128, 128, 256, 256, 16, 256, 256, 256, 256, 4, 4, 256, 256, 256, 256, 256):
        offs.append(acc)
        acc += s
    (o_gq, o_gk, o_gv, o_gr, o_ga, o_dq, o_dk, o_dv, o_dz, o_da, o_db, o_cu, o_cv, o_sh, o_sb, o_sc) = offs
    col = lambda o, n: w_in[:, o:o + n]
    zeros = jnp.zeros((D_MODEL, LANE - GLA_RANK), w_in.dtype)
    return jnp.concatenate([
        col(o_gq, 128), col(o_gk, 128), col(o_gv, 256), col(o_gr, 256), col(o_ga, GLA_RANK), zeros,
        col(o_dq, 768), col(o_dz, 256),
        jnp.repeat(col(o_da, GDN_HEADS), GDN_DK, axis=1), jnp.repeat(col(o_db, GDN_HEADS), GDN_DK, axis=1),
        col(o_cu, 256), col(o_cv, 256), col(o_sh, 256), col(o_sb, 256), col(o_sc, 256)], axis=1)


def _gla_state_in(s, nseq):
    b = s.shape[0]
    eye = jnp.eye(GLA_HEADS, dtype=s.dtype)
    t = jnp.einsum('bhkv,hg->bhvgk', s, eye).reshape(b // nseq, nseq, GLA_HEADS * GLA_DV, GLA_QK)
    return jnp.transpose(t, (0, 2, 1, 3)).reshape(b // nseq, GLA_HEADS * GLA_DV, nseq * GLA_QK)


def _gla_state_out(sw, nseq):
    g = sw.shape[0]
    t = sw.reshape(g, GLA_HEADS, GLA_DV, nseq, GLA_HEADS, GLA_DK)
    d = jnp.stack([t[:, h, :, :, h, :] for h in range(GLA_HEADS)], axis=1)
    return jnp.transpose(d, (0, 3, 1, 4, 2)).reshape(g * nseq, GLA_HEADS, GLA_DK, GLA_DV)


def _gdn_state_in(s, nseq):
    b = s.shape[0]
    eye = jnp.eye(GDN_HEADS, dtype=s.dtype)
    t = jnp.einsum('bhkv,hg->bhkgv', s, eye)
    return t.reshape(b // nseq, nseq * GDN_HEADS * GDN_DK, GDN_HEADS * GDN_DV)


def _gdn_state_out(st, nseq):
    g = st.shape[0]
    t = st.reshape(g, nseq, GDN_HEADS, GDN_DK, GDN_HEADS, GDN_DV)
    d = jnp.stack([t[:, :, h, :, h, :] for h in range(GDN_HEADS)], axis=2)
    return d.reshape(g * nseq, GDN_HEADS, GDN_DK, GDN_DV)


def kernel(x_prompt, x_sample, state_gla, state_gdn, state_gdn_conv, state_sconv, p_prompt, p_sample, norm_mix, w_in, gla_wa2, gla_ba, gla_norm, gdn_conv_w, gdn_a_log, gdn_dt_bias, gdn_norm, cm_ln_g, cm_ln_b, cm_ws, cm_bs, sc_conv_w, w_gate, w_branch, w_o, norm_ffn, w_ffn_gate, w_ffn_up, w_ffn_down, norm_ple, w_ple_gate, w_ple, norm_final):
    depth = w_in.shape[0]
    bp, tp, _ = x_prompt.shape
    bs, ts, _ = x_sample.shape
    npt, nst = bp * tp, bs * ts
    sseq = ROWS // ts
    assert ts == 8 and tp % CM_CHUNK == 0 and bs % sseq == 0 and npt % TOKEN_TILE == 0 and nst % TOKEN_TILE == 0
    nchunk = tp // ROWS
    sgroups = bs // sseq

    h = jnp.concatenate([x_prompt.reshape(npt, D_MODEL), x_sample.reshape(nst, D_MODEL)], axis=0)
    row = lambda a: a.reshape(1, -1)
    outs = {k: [] for k in ("gla_p", "gla_s", "gdn_p", "gdn_s", "gc_p", "gc_s", "sc_p", "sc_s", "cv_s")}
    for i in range(depth):
        w1 = _perm_w_in(w_in[i]).astype(BF16)
        wa2 = jnp.concatenate([gla_wa2[i], jnp.zeros((LANE - GLA_RANK, GLA_QK), F32)], axis=0).astype(BF16)
        p_gla, p_gdn, p_cm = _inproj(
            h, row(norm_mix[i]), w1, wa2, row(gla_ba[i]), row(jnp.repeat(gdn_a_log[i], GDN_DK)),
            row(jnp.repeat(gdn_dt_bias[i], GDN_DK)), row(cm_ln_g[i]), row(cm_ln_b[i]))

        gn = row(jnp.tile(gla_norm[i], GLA_HEADS))
        o_gla_p, st_p = _gla(p_gla, jnp.zeros((bp, 256, GLA_QK), F32), gn,
                             base_block=0, nouter=bp, nchunk=nchunk, nseq=1)
        o_gla_s, st_s = _gla(p_gla, _gla_state_in(state_gla[i], sseq), gn,
                             base_block=npt // ROWS, nouter=sgroups, nchunk=1, nseq=sseq)
        outs["gla_p"].append(_gla_state_out(st_p, 1))
        outs["gla_s"].append(_gla_state_out(st_s, sseq))

        cw = jnp.concatenate([gdn_conv_w[i], jnp.concatenate([jnp.zeros((1, BRANCH_W), F32), sc_conv_w[i]], axis=0)],
                             axis=1)
        gdn_gn = row(jnp.tile(gdn_norm[i], GDN_HEADS))
        hist_s = jnp.concatenate([
            jnp.pad(state_gdn_conv[i], ((0, 0), (HIST - (GDN_CONV - 1), 0), (0, 0))),
            jnp.pad(state_sconv[i], ((0, 0), (HIST - (SC_WIDTH - 1), 0), (0, 0)))], axis=2).reshape(sgroups, ROWS, CONV_W)
        o_gdn_p, o_sc_p, sd_p = _gdn(p_gdn, jnp.zeros((bp, ROWS, CONV_W), F32), jnp.zeros((bp, 256, 256), F32), cw,
                                     gdn_gn, base_block=0, nouter=bp, nchunk=nchunk, nseq=1)
        o_gdn_s, o_sc_s, sd_s = _gdn(p_gdn, hist_s, _gdn_state_in(state_gdn[i], sseq), cw, gdn_gn,
                                     base_block=npt // ROWS, nouter=sgroups, nchunk=1, nseq=sseq)
        outs["gdn_p"].append(_gdn_state_out(sd_p, 1))
        outs["gdn_s"].append(_gdn_state_out(sd_s, sseq))
        xp3 = p_gdn[:npt, 0:CONV_W].reshape(bp, tp, CONV_W)
        xs3 = p_gdn[npt:, 0:CONV_W].reshape(bs, ts, CONV_W)
        outs["gc_p"].append(xp3[:, tp - (GDN_CONV - 1):, 0:768])
        outs["gc_s"].append(xs3[:, ts - (GDN_CONV - 1):, 0:768])
        outs["sc_p"].append(xp3[:, tp - (SC_WIDTH - 1):, 768:])
        outs["sc_s"].append(xs3[:, ts - (SC_WIDTH - 1):, 768:])

        ws_p = cm_ws[i].reshape(CM_GROUPS * CM_CHUNK, CM_CHUNK)
        bias_p = jnp.repeat(cm_bs[i].T, BRANCH_W // CM_GROUPS, axis=1)
        ws_s = jnp.tile(cm_ws[i][:, :ts, :ts], (1, sseq, sseq)).reshape(CM_GROUPS * ROWS, ROWS)
        bias_s = jnp.tile(bias_p[:ts], (sseq, 1))
        o_cm_p = _cm(p_cm, ws_p, bias_p, base_block=0, nblocks=npt // CM_CHUNK, rows=CM_CHUNK, seg=CM_CHUNK)
        o_cm_s = _cm(p_cm, ws_s, bias_s, base_block=npt // ROWS, nblocks=nst // ROWS, rows=ROWS, seg=ts)
        outs["cv_s"].append(p_cm[npt:, 256:512].reshape(bs, ts, BRANCH_W))

        cat = lambda a, b: jnp.concatenate([a, b], axis=0)
        branches = (cat(o_gla_p, o_gla_s), cat(o_gdn_p, o_gdn_s), cat(o_cm_p, o_cm_s), cat(o_sc_p, o_sc_s))
        h = _merge(h, branches, row(norm_mix[i]), w_gate[i].astype(BF16), w_branch[i].astype(BF16),
                   w_o[i].astype(BF16))
        pe = cat(p_prompt[i].reshape(npt, PLE_DIM), p_sample[i].reshape(nst, PLE_DIM))
        h = _ffn(h, pe, row(norm_ffn[i]), w_ffn_gate[i].astype(BF16), w_ffn_up[i].astype(BF16),
                 w_ffn_down[i].astype(BF16), row(norm_ple[i]), w_ple_gate[i].astype(BF16), w_ple[i].astype(BF16),
                 row(norm_final), final=(i == depth - 1))

    y_prompt = h[:npt].reshape(bp, tp, D_MODEL)
    y_sample = h[npt:].reshape(bs, ts, D_MODEL)
    st = lambda k: jnp.stack(outs[k])
    return (y_prompt, y_sample, st("gla_p"), st("gla_s"), st("gdn_p"), st("gdn_s"),
            st("gc_p"), st("gc_s"), st("sc_p"), st("sc_s"), st("cv_s"))
```

```python
import functools

import jax
import jax.numpy as jnp
from jax import lax
from jax.experimental import pallas as pl
from jax.experimental.pallas import tpu as pltpu

F32 = jnp.float32
BF16 = jnp.bfloat16

D_MODEL = 1024
PLE_DIM = 256
BRANCH_W = 256
N_BRANCH = 4
GLA_HEADS = 4
GLA_DK = 32
GLA_DV = 64
GLA_RANK = 16
GLA_TAU = 16.0
GDN_HEADS = 4
GDN_DK = 64
GDN_DV = 64
GDN_CONV = 4
CM_GROUPS = 4
CM_CHUNK = 128
SC_WIDTH = 3
D_FF = 2816
EPS = 1e-6

ROWS = 64
BLOCKS_PER_STEP = 4
STEP_ROWS = ROWS * BLOCKS_PER_STEP
SAMPLE_BLOCKS_PER_STEP = 2
SAMPLE_STEP_ROWS = ROWS * SAMPLE_BLOCKS_PER_STEP
GLA_QK = GLA_HEADS * GLA_DK
GDN_QK = GDN_HEADS * GDN_DK
LANE = 128
CONV_W = 3 * BRANCH_W + BRANCH_W
HIST = 8

C_GQ, C_GK, C_GV, C_GR, C_GA = 0, 128, 256, 512, 768
C_DQKV, C_DZ, C_DA, C_DB = 896, 1664, 1920, 2176
C_CU, C_CV = 2432, 2688
C_SH, C_SB, C_SC = 2944, 3200, 3456
W1_COLS = 3712
P_GLA = 0
P_GDN = 896
P_CM = 2944
P_COLS = 3456

VMEM_LIMIT = 56 * 1024 * 1024
TOKEN_TILE = 256


def _dot(a, b):
    return jnp.dot(a, b, preferred_element_type=F32)


def _dot_nt(a, b):
    return lax.dot_general(a, b, (((1,), (1,)), ((), ())), preferred_element_type=F32)


def _dot_tn(a, b):
    return lax.dot_general(a, b, (((0,), (0,)), ((), ())), preferred_element_type=F32)


def _split(x, n):
    parts, r = [], x
    for i in range(n):
        p = r.astype(BF16)
        parts.append(p)
        if i + 1 < n:
            r = r - p.astype(F32)
    return parts


def _dot1(a, b, dot=_dot):
    return dot(a.astype(BF16), b.astype(BF16))


def _dot3(a, b_pieces):
    ah, al = _split(a, 2)
    bh, bl = b_pieces
    return _dot(jnp.concatenate([ah, ah, al], axis=1), jnp.concatenate([bh, bl, bh], axis=0))


def _mask_dot(mask, x, n):
    return _dot(jnp.concatenate([mask.astype(BF16)] * n, axis=1), jnp.concatenate(_split(x, n), axis=0))


def _mask_dot_nt(mask, x, n):
    return _dot_nt(jnp.concatenate([mask.astype(BF16)] * n, axis=1), jnp.concatenate(_split(x, n), axis=1))


def _mask_dot_rhs(x, mask, n):
    return _dot(jnp.concatenate(_split(x, n), axis=1), jnp.concatenate([mask.astype(BF16)] * n, axis=0))


def _sigmoid(x):
    return 1.0 / (1.0 + jnp.exp(-x))


def _silu(x):
    return x * _sigmoid(x)


def _softplus(x):
    return jnp.maximum(x, 0.0) + jnp.log1p(jnp.exp(-jnp.abs(x)))


def _gelu_tanh(x):
    return 0.5 * x * (1.0 + jnp.tanh(0.7978845608028654 * (x + 0.044715 * (x * x * x))))


def _rms(x, w):
    return x * lax.rsqrt(jnp.mean(x * x, axis=-1, keepdims=True) + EPS) * w


def _idiv(x, n):
    assert n & (n - 1) == 0
    return lax.shift_right_logical(x, n.bit_length() - 1)


def _imod(x, n):
    assert n & (n - 1) == 0
    return lax.bitwise_and(x, n - 1)


def _const_spec(shape):
    return pl.BlockSpec(shape, lambda *_: (0,) * len(shape))


def _iota2(shape):
    return lax.broadcasted_iota(jnp.int32, shape, 0), lax.broadcasted_iota(jnp.int32, shape, 1)


def _inproj_kernel(h_ref, nw_ref, w_ref, wa2_ref, ba_ref, alog_ref, dtb_ref, lng_ref, lnb_ref, p_ref):
    xn = _rms(h_ref[...], nw_ref[...]).astype(BF16)
    proj = _dot(xn, w_ref[...])
    p_ref[:, 0:128] = proj[:, C_GQ:C_GQ + 128] * (GLA_DK ** -0.5)
    p_ref[:, 128:512] = proj[:, C_GK:C_GR]
    p_ref[:, 512:768] = _silu(proj[:, C_GR:C_GR + 256])
    za = _dot(proj[:, C_GA:C_GA + LANE].astype(BF16), wa2_ref[...]) + ba_ref[...]
    p_ref[:, 768:896] = -_softplus(-za) * (1.0 / GLA_TAU)
    g0 = P_GDN
    p_ref[:, g0:g0 + 768] = proj[:, C_DQKV:C_DQKV + 768]
    p_ref[:, g0 + 768:g0 + 1024] = proj[:, C_SC:C_SC + 256] * proj[:, C_SH:C_SH + 256]
    p_ref[:, g0 + 1024:g0 + 1280] = _silu(proj[:, C_DZ:C_DZ + 256])
    p_ref[:, g0 + 1280:g0 + 1536] = proj[:, C_SB:C_SB + 256]
    p_ref[:, g0 + 1536:g0 + 1792] = -jnp.exp(alog_ref[...]) * _softplus(proj[:, C_DA:C_DA + 256] + dtb_ref[...])
    p_ref[:, g0 + 1792:g0 + 2048] = _sigmoid(proj[:, C_DB:C_DB + 256])
    p_ref[:, P_CM:P_CM + 256] = _gelu_tanh(proj[:, C_CU:C_CU + 256])
    gv = _gelu_tanh(proj[:, C_CV:C_CV + 256])
    mu = jnp.mean(gv, axis=-1, keepdims=True)
    d = gv - mu
    var = jnp.mean(d * d, axis=-1, keepdims=True)
    p_ref[:, P_CM + 256:P_CM + 512] = d * lax.rsqrt(var + EPS) * lng_ref[...] + lnb_ref[...]


def _inproj(h, nw, w1, wa2, ba, alog, dtb, lng, lnb):
    ntok = h.shape[0]
    tm = TOKEN_TILE
    row = lambda n: pl.BlockSpec((tm, n), lambda i: (i, 0))
    return pl.pallas_call(
        _inproj_kernel,
        grid=(ntok // tm,),
        in_specs=[row(D_MODEL), _const_spec((1, D_MODEL)), _const_spec((D_MODEL, W1_COLS)),
                  _const_spec((LANE, GLA_QK)), _const_spec((1, GLA_QK)), _const_spec((1, 256)),
                  _const_spec((1, 256)), _const_spec((1, 256)), _const_spec((1, 256))],
        out_specs=row(P_COLS),
        out_shape=jax.ShapeDtypeStruct((ntok, P_COLS), F32),
        compiler_params=pltpu.CompilerParams(dimension_semantics=("parallel",), vmem_limit_bytes=VMEM_LIMIT),
        name="inproj",
    )(h, nw, w1, wa2, ba, alog, dtb, lng, lnb)


def _stack_heads(x, group, nheads, period=None):
    w = x.shape[1]
    li = lax.broadcasted_iota(jnp.int32, (1, w), 1)
    if period is not None:
        li = _imod(li, period)
    hid = _idiv(li, group)
    zero = jnp.zeros_like(x)
    return jnp.concatenate([jnp.where(hid == h, x, zero) for h in range(nheads)], axis=0)


def _block_diag(x, nblocks):
    r = x.shape[0]
    ri, ci = _iota2((nblocks * r, nblocks * r))
    return jnp.where(_idiv(ri, r) == _idiv(ci, r), jnp.concatenate([x] * nblocks, axis=0), jnp.zeros((), x.dtype))


def _widen(x, nseq):
    if nseq == 1:
        return x
    seg = ROWS // nseq
    sid = _idiv(lax.broadcasted_iota(jnp.int32, (ROWS, 1), 0), seg)
    zero = jnp.zeros_like(x)
    return jnp.concatenate([jnp.where(sid == j, x, zero) for j in range(nseq)], axis=1)


def _group_mean(x, group):
    w = x.shape[1]
    ri, ci = _iota2((w, w))
    avg = jnp.where(_idiv(ri, group) == _idiv(ci, group), 1.0 / group, 0.0).astype(BF16)
    return _mask_dot_rhs(x, avg, 2)


def _head_diag_mask(rows, cols, rhead, chead, rper):
    ri, ci = _iota2((rows, cols))
    return (_idiv(_imod(ri, rper), rhead) == _idiv(ci, chead)).astype(F32)


def _expand_state(s, width, reps):
    ri, ci = _iota2((width, reps * width))
    return _mask_dot_rhs(s, ri == _imod(ci, width), 3)


def _compact_state(st, width, reps):
    out = st[:, 0:width]
    for h in range(1, reps):
        out = out + st[:, h * width:(h + 1) * width]
    return out


class _Masks:
    def __init__(self, seg):
        ri, ci = _iota2((ROWS, ROWS))
        self.same = _idiv(ri, seg) == _idiv(ci, seg)
        self.tri = self.same & (ci <= ri)
        rl, cl = _iota2((ROWS, GDN_HEADS * ROWS))
        cl = _imod(cl, ROWS)
        same_l = _idiv(rl, seg) == _idiv(cl, seg)
        self.tri_l = same_l & (cl <= rl)
        self.strict_l = same_l & (cl < rl)
        self.eye_l = (rl == cl).astype(F32)
        self.ones = jnp.ones((ROWS, ROWS), BF16)
        self.seg = seg
        nseq = ROWS // seg
        self.bd_gla = _head_diag_mask(nseq * GLA_QK, GLA_HEADS * GLA_DV, GLA_DK, GLA_DV, GLA_QK)
        self.bd_gdn = _head_diag_mask(nseq * GDN_QK, GDN_QK, GDN_DK, GDN_DV, GDN_QK)


def _gla_prep(p, m, nseq):
    q, k, v, la = p[:, 0:128], p[:, 128:256], p[:, 256:512], p[:, 768:896]
    b = _mask_dot(m.tri, la, 3)
    if m.seg == ROWS:
        btot = jnp.broadcast_to(b[ROWS - 1:ROWS], b.shape)
    else:
        btot = _mask_dot(m.same, la, 3)
    qd = q * jnp.exp(b)
    kd = k * jnp.exp(-b)
    ke = k * jnp.exp(btot - b)
    a = jnp.where(m.tri_l, _dot1(qd, _stack_heads(kd, GLA_DK, GLA_HEADS), _dot_nt), 0.0)
    o_intra = _dot1(a, _stack_heads(v, GLA_DV, GLA_HEADS))
    dec = jnp.exp(_dot_tn(jnp.concatenate(_split(_widen(la, nseq), 3), axis=0),
                          jnp.ones((3 * ROWS, GLA_HEADS * GLA_DV), BF16)))
    ds = _dot1(_widen(ke, nseq), v, _dot_tn) * m.bd_gla
    return o_intra, _widen(qd, nseq), dec, ds


def _gla_scan(prep, st):
    o_intra, qd, dec, ds = prep
    return o_intra + _dot1(qd, st), st * dec + ds


def _gla_out(o, gn, rs):
    return o * lax.rsqrt(_group_mean(o * o, GLA_DV) + EPS) * gn * rs


def _gdn_qk_norm(qkv):
    hd = GDN_DK
    r = qkv.shape[0]
    cqk = jnp.concatenate([qkv[:, 0:256], qkv[:, 256:512]], axis=0)
    nrm = lax.rsqrt(_group_mean(cqk * cqk, hd) * hd + EPS)
    return qkv[:, 0:256] * nrm[0:r] * (hd ** -0.5), qkv[:, 256:512] * nrm[r:2 * r]


def _gdn_prep(q, k, cv, gs, betas, m, seg):
    nh, hd = GDN_HEADS, GDN_DK
    n = range(len(q))
    gc = [_mask_dot(m.tri, gs[i], 3) for i in n]
    if seg == ROWS:
        gtot = [jnp.broadcast_to(gc[i][ROWS - 1:ROWS], gc[i].shape) for i in n]
    else:
        gtot = [_mask_dot(m.same, gs[i], 3) for i in n]
    qkk = [_dot1(jnp.concatenate([q[i], k[i]], axis=0), _stack_heads(k[i], hd, nh), _dot_nt) for i in n]
    grow = [_mask_dot(m.ones, gc[i] * m.eye_l, 3) for i in n]
    decay = [jnp.where(m.tri_l, jnp.exp(jnp.where(m.tri_l, gc[i] - grow[i], 0.0)), 0.0) for i in n]
    amat = [jnp.where(m.strict_l, betas[i] * decay[i] * qkk[i][ROWS:2 * ROWS], 0.0) for i in n]
    inv = [m.eye_l - amat[i] for i in n]
    pw = amat
    pw_bd = [_block_diag(pw[i].astype(BF16), nh) for i in n]
    for _ in range(max(1, (seg - 1).bit_length()) - 1):
        pw = [_dot(pw[i].astype(BF16), pw_bd[i]) for i in n]
        pw_bd = [_block_diag(pw[i].astype(BF16), nh) for i in n]
        inv = [inv[i] + _dot(inv[i].astype(BF16), pw_bd[i]) for i in n]
    eg = [jnp.exp(gc[i]) for i in n]
    rhs = [jnp.concatenate([betas[i] * cv[i], betas[i] * eg[i] * k[i]], axis=1) for i in n]
    uw = [_dot(inv[i].astype(BF16), _stack_heads(rhs[i].astype(BF16), hd, nh, period=GDN_QK)) for i in n]
    return [(uw[i][:, 0:256], uw[i][:, 256:512], q[i] * eg[i], qkk[i][0:ROWS] * decay[i],
             k[i] * jnp.exp(gtot[i] - gc[i]), jnp.exp(gtot[i])) for i in n]


def _gdn_scan(prep, st, m, nseq):
    u0, w, qg, qkd, kend, dn = prep
    seg = ROWS // nseq
    ws = _dot1(jnp.concatenate([_widen(w, nseq), _widen(qg, nseq)], axis=0), st)
    u = u0 - ws[0:ROWS]
    o = ws[ROWS:2 * ROWS] + _dot1(qkd, _stack_heads(u, GDN_DV, GDN_HEADS))
    dn_tall = jnp.concatenate([jnp.broadcast_to(dn[j * seg:j * seg + 1], (GDN_QK, GDN_QK)) for j in range(nseq)],
                              axis=0)
    st = st * dn_tall + _dot1(_widen(kend, nseq), u, _dot_tn) * m.bd_gdn
    return o, st


def _gdn_out(o, gn, zs):
    return o * lax.rsqrt(_group_mean(o * o, GDN_DV) + EPS) * gn * zs


def _cm_block(p, ws, bias, seg):
    r = p.shape[0]
    gu, vn = p[:, 0:256], p[:, 256:512]
    ri, ci = _iota2((r, CM_GROUPS * r))
    ci = _imod(ci, r)
    wm = jnp.where((_idiv(ri, seg) == _idiv(ci, seg)) & (ci <= ri), ws, 0.0)
    return gu * (_dot1(wm, _stack_heads(vn, BRANCH_W // CM_GROUPS, CM_GROUPS)) + bias)


def _conv_prompt(x, hist, cw):
    n = x.shape[0]
    t8 = lax.broadcasted_iota(jnp.int32, (HIST, 1), 0)
    acc = cw[3:4] * x
    for d in range(1, GDN_CONV):
        xr = pltpu.roll(x, d, 0)
        head = jnp.where(t8 < d, pltpu.roll(hist, d, 0), xr[0:HIST])
        acc = acc + cw[3 - d:4 - d] * jnp.concatenate([head, xr[HIST:n]], axis=0)
    return acc


def _conv_sample(x, hist, cw, seg):
    n = x.shape[0]
    tloc = _imod(lax.broadcasted_iota(jnp.int32, (n, 1), 0), seg)
    acc = cw[3:4] * x
    for d in range(1, GDN_CONV):
        prev = jnp.where(tloc < d, pltpu.roll(hist, (d - seg) % n, 0), pltpu.roll(x, d, 0))
        acc = acc + cw[3 - d:4 - d] * prev
    return acc


def _block_rows(c):
    return slice(c * ROWS, (c + 1) * ROWS)


def _mixer_prompt_kernel(p_ref, cw_ref, gn_gla_ref, gn_gdn_ref, cmw_ref, cmb_ref,
                         o_ref, sgla_ref, sgdn_ref, st_gla, st_gdn, hist):
    @pl.when(pl.program_id(1) == 0)
    def _():
        st_gla[...] = jnp.zeros_like(st_gla)
        st_gdn[...] = jnp.zeros_like(st_gdn)
        hist[...] = jnp.zeros_like(hist)

    nb = range(BLOCKS_PER_STEP)
    m = _Masks(ROWS)
    g0 = P_GDN
    x = p_ref[:, g0:g0 + CONV_W]
    acc = _conv_prompt(x, hist[...], cw_ref[...])
    hist[...] = x[STEP_ROWS - HIST:STEP_ROWS]
    o_ref[:, 768:1024] = p_ref[:, g0 + 1280:g0 + 1536] * acc[:, 768:1024]
    qkv = _silu(acc[:, 0:768])
    qn, kn = _gdn_qk_norm(qkv)
    gdn = _gdn_prep([qn[_block_rows(c)] for c in nb], [kn[_block_rows(c)] for c in nb],
                    [qkv[_block_rows(c), 512:768] for c in nb],
                    [p_ref[_block_rows(c), g0 + 1536:g0 + 1792] for c in nb],
                    [p_ref[_block_rows(c), g0 + 1792:g0 + 2048] for c in nb], m, ROWS)
    gla = [_gla_prep(p_ref[_block_rows(c), P_GLA:P_GLA + 896], m, 1) for c in nb]
    sg = st_gla[...]
    sd = st_gdn[...]
    og, od = [], []
    for c in nb:
        o, sg = _gla_scan(gla[c], sg)
        og.append(o)
        o, sd = _gdn_scan(gdn[c], sd, m, 1)
        od.append(o)
    st_gla[...] = sg
    st_gdn[...] = sd
    o_ref[:, 0:256] = _gla_out(jnp.concatenate(og, axis=0), gn_gla_ref[...], p_ref[:, 512:768])
    o_ref[:, 256:512] = _gdn_out(jnp.concatenate(od, axis=0), gn_gdn_ref[...], p_ref[:, g0 + 1024:g0 + 1280])
    for c in range(STEP_ROWS // CM_CHUNK):
        rows = slice(c * CM_CHUNK, (c + 1) * CM_CHUNK)
        o_ref[rows, 512:768] = _cm_block(p_ref[rows, P_CM:P_CM + 512], cmw_ref[...], cmb_ref[...], CM_CHUNK)

    @pl.when(pl.program_id(1) == pl.num_programs(1) - 1)
    def _():
        sgla_ref[...] = _compact_state(sg, GLA_DV, GLA_HEADS)
        sgdn_ref[...] = _compact_state(sd, GDN_DV, GDN_HEADS)


def _mixer_prompt(p, cw, gn_gla, gn_gdn, cmw, cmb, *, nseqs, nsteps):
    return pl.pallas_call(
        _mixer_prompt_kernel,
        grid=(nseqs, nsteps),
        in_specs=[pl.BlockSpec((STEP_ROWS, P_COLS), lambda o, c: (o * nsteps + c, 0)),
                  _const_spec((GDN_CONV, CONV_W)), _const_spec((1, 256)), _const_spec((1, 256)),
                  _const_spec((CM_CHUNK, CM_GROUPS * CM_CHUNK)), _const_spec((CM_CHUNK, 256))],
        out_specs=[pl.BlockSpec((STEP_ROWS, 4 * BRANCH_W), lambda o, c: (o * nsteps + c, 0)),
                   pl.BlockSpec((None, GLA_QK, GLA_DV), lambda o, c: (o, 0, 0)),
                   pl.BlockSpec((None, GDN_QK, GDN_DV), lambda o, c: (o, 0, 0))],
        out_shape=[jax.ShapeDtypeStruct((nseqs * nsteps * STEP_ROWS, 4 * BRANCH_W), F32),
                   jax.ShapeDtypeStruct((nseqs, GLA_QK, GLA_DV), F32),
                   jax.ShapeDtypeStruct((nseqs, GDN_QK, GDN_DV), F32)],
        scratch_shapes=[pltpu.VMEM((GLA_QK, GLA_HEADS * GLA_DV), F32), pltpu.VMEM((GDN_QK, GDN_QK), F32),
                        pltpu.VMEM((HIST, CONV_W), F32)],
        compiler_params=pltpu.CompilerParams(dimension_semantics=("arbitrary", "arbitrary"),
                                             vmem_limit_bytes=VMEM_LIMIT),
        name="mixer_prompt",
    )(p, cw, gn_gla, gn_gdn, cmw, cmb)


def _mixer_sample_kernel(p_ref, hist_ref, s0gla_ref, s0gdn_ref, cw_ref, gn_gla_ref, gn_gdn_ref, cmw_ref, cmb_ref,
                         o_ref, sgla_ref, sgdn_ref, *, seg):
    nseq = ROWS // seg
    nb = range(SAMPLE_BLOCKS_PER_STEP)
    m = _Masks(seg)
    g0 = P_GDN
    x = p_ref[:, g0:g0 + CONV_W]
    acc = _conv_sample(x, hist_ref[...], cw_ref[...], seg)
    o_ref[:, 768:1024] = p_ref[:, g0 + 1280:g0 + 1536] * acc[:, 768:1024]
    qkv = _silu(acc[:, 0:768])
    qn, kn = _gdn_qk_norm(qkv)
    gdn = _gdn_prep([qn[_block_rows(c)] for c in nb], [kn[_block_rows(c)] for c in nb],
                    [qkv[_block_rows(c), 512:768] for c in nb],
                    [p_ref[_block_rows(c), g0 + 1536:g0 + 1792] for c in nb],
                    [p_ref[_block_rows(c), g0 + 1792:g0 + 2048] for c in nb], m, seg)
    gla = [_gla_prep(p_ref[_block_rows(c), P_GLA:P_GLA + 896], m, nseq) for c in nb]
    og, od = [], []
    for c in nb:
        rows = _block_rows(c)
        seqs = slice(c * nseq, (c + 1) * nseq)
        sg = _expand_state(s0gla_ref[seqs].reshape(nseq * GLA_QK, GLA_DV), GLA_DV, GLA_HEADS) * m.bd_gla
        o, sg = _gla_scan(gla[c], sg)
        og.append(o)
        sgla_ref[seqs] = _compact_state(sg, GLA_DV, GLA_HEADS).reshape(nseq, GLA_QK, GLA_DV)
        sd = _expand_state(s0gdn_ref[seqs].reshape(nseq * GDN_QK, GDN_DV), GDN_DV, GDN_HEADS) * m.bd_gdn
        o, sd = _gdn_scan(gdn[c], sd, m, nseq)
        od.append(o)
        sgdn_ref[seqs] = _compact_state(sd, GDN_DV, GDN_HEADS).reshape(nseq, GDN_QK, GDN_DV)
        o_ref[rows, 512:768] = _cm_block(p_ref[rows, P_CM:P_CM + 512], cmw_ref[...], cmb_ref[...], seg)
    o_ref[:, 0:256] = _gla_out(jnp.concatenate(og, axis=0), gn_gla_ref[...], p_ref[:, 512:768])
    o_ref[:, 256:512] = _gdn_out(jnp.concatenate(od, axis=0), gn_gdn_ref[...], p_ref[:, g0 + 1024:g0 + 1280])


def _mixer_sample(p, hist, s0gla, s0gdn, cw, gn_gla, gn_gdn, cmw, cmb, *, layer, base_step, nsteps, seg):
    rows = SAMPLE_STEP_ROWS
    nseq_step = rows // seg
    return pl.pallas_call(
        functools.partial(_mixer_sample_kernel, seg=seg),
        grid=(nsteps,),
        in_specs=[pl.BlockSpec((rows, P_COLS), lambda i: (base_step + i, 0)),
                  pl.BlockSpec((rows, CONV_W), lambda i: (i, 0)),
                  pl.BlockSpec((None, nseq_step, GLA_QK, GLA_DV), lambda i: (layer, i, 0, 0)),
                  pl.BlockSpec((None, nseq_step, GDN_QK, GDN_DV), lambda i: (layer, i, 0, 0)),
                  _const_spec((GDN_CONV, CONV_W)), _const_spec((1, 256)), _const_spec((1, 256)),
                  _const_spec((ROWS, CM_GROUPS * ROWS)), _const_spec((ROWS, 256))],
        out_specs=[pl.BlockSpec((rows, 4 * BRANCH_W), lambda i: (i, 0)),
                   pl.BlockSpec((nseq_step, GLA_QK, GLA_DV), lambda i: (i, 0, 0)),
                   pl.BlockSpec((nseq_step, GDN_QK, GDN_DV), lambda i: (i, 0, 0))],
        out_shape=[jax.ShapeDtypeStruct((nsteps * rows, 4 * BRANCH_W), F32),
                   jax.ShapeDtypeStruct((nsteps * nseq_step, GLA_QK, GLA_DV), F32),
                   jax.ShapeDtypeStruct((nsteps * nseq_step, GDN_QK, GDN_DV), F32)],
        compiler_params=pltpu.CompilerParams(dimension_semantics=("parallel",), vmem_limit_bytes=VMEM_LIMIT),
        name="mixer_sample",
    )(p, hist, s0gla, s0gdn, cw, gn_gla, gn_gdn, cmw, cmb)


def _merge_kernel(h_ref, bp_ref, bs_ref, nw_ref, wg_ref, wb_ref, wo_ref, o_ref, *, prompt_tiles):
    h = h_ref[...]
    xn = _rms(h, nw_ref[...]).astype(BF16)
    br = jnp.where(pl.program_id(0) < prompt_tiles, bp_ref[...], bs_ref[...]).astype(BF16)
    merged = None
    for gi in range(N_BRANCH):
        gate = _sigmoid(_dot(xn, wg_ref[:, gi * D_MODEL:(gi + 1) * D_MODEL]))
        term = _dot(br[:, gi * BRANCH_W:(gi + 1) * BRANCH_W], wb_ref[gi]) * gate
        merged = term if merged is None else merged + term
    o_ref[...] = h + _dot(merged.astype(BF16), wo_ref[...])


def _stream_specs(tm, width, prompt_tiles, lead=()):
    nlead = (None,) * len(lead)
    return (pl.BlockSpec(nlead + (tm, width), lambda i: lead + (jnp.minimum(i, prompt_tiles - 1), 0)),
            pl.BlockSpec(nlead + (tm, width), lambda i: lead + (jnp.maximum(i - prompt_tiles, 0), 0)))


def _merge(h, br_p, br_s, nw, wg, wb, wo):
    ntok = h.shape[0]
    tm = TOKEN_TILE
    prompt_tiles = br_p.shape[0] // tm
    row = lambda n: pl.BlockSpec((tm, n), lambda i: (i, 0))
    return pl.pallas_call(
        functools.partial(_merge_kernel, prompt_tiles=prompt_tiles),
        grid=(ntok // tm,),
        in_specs=[row(D_MODEL), *_stream_specs(tm, 4 * BRANCH_W, prompt_tiles), _const_spec((1, D_MODEL)),
                  _const_spec((D_MODEL, N_BRANCH * D_MODEL)), _const_spec((N_BRANCH, BRANCH_W, D_MODEL)),
                  _const_spec((D_MODEL, D_MODEL))],
        out_specs=row(D_MODEL),
        out_shape=jax.ShapeDtypeStruct((ntok, D_MODEL), F32),
        compiler_params=pltpu.CompilerParams(dimension_semantics=("parallel",), vmem_limit_bytes=VMEM_LIMIT),
        name="merge",
    )(h, br_p, br_s, nw, wg, wb, wo)


def _ffn_kernel(h_ref, pp_ref, ps_ref, nf_ref, wfg_ref, wfu_ref, wfd_ref, np_ref, wpg_ref, wp_ref, nfin_ref, o_ref, *,
                final, prompt_tiles):
    h = h_ref[...]
    xf = _rms(h, nf_ref[...]).astype(BF16)
    act = _silu(_dot(xf, wfg_ref[...])) * _dot(xf, wfu_ref[...])
    h = h + _dot(act.astype(BF16), wfd_ref[...])
    pg = _sigmoid(_dot(_rms(h, np_ref[...]).astype(BF16), wpg_ref[...]))
    pe = jnp.where(pl.program_id(0) < prompt_tiles, pp_ref[...], ps_ref[...]).astype(BF16)
    h = h + pg * _dot(pe, wp_ref[...])
    o_ref[...] = _rms(h, nfin_ref[...]) if final else h


def _ffn(h, pe_p, pe_s, nf, wfg, wfu, wfd, npl, wpg, wp, nfin, *, layer, final):
    ntok = h.shape[0]
    tm = TOKEN_TILE
    prompt_tiles = pe_p.shape[1] // tm
    row = lambda n: pl.BlockSpec((tm, n), lambda i: (i, 0))
    once = lambda shape: pl.BlockSpec(shape, lambda *_: (0,) * len(shape), pipeline_mode=pl.Buffered(1))
    return pl.pallas_call(
        functools.partial(_ffn_kernel, final=final, prompt_tiles=prompt_tiles),
        grid=(ntok // tm,),
        in_specs=[row(D_MODEL), *_stream_specs(tm, PLE_DIM, prompt_tiles, lead=(layer,)), _const_spec((1, D_MODEL)),
                  once((D_MODEL, D_FF)), once((D_MODEL, D_FF)), once((D_FF, D_MODEL)),
                  _const_spec((1, D_MODEL)), once((D_MODEL, D_MODEL)), once((PLE_DIM, D_MODEL)),
                  _const_spec((1, D_MODEL))],
        out_specs=row(D_MODEL),
        out_shape=jax.ShapeDtypeStruct((ntok, D_MODEL), F32),
        compiler_params=pltpu.CompilerParams(dimension_semantics=("parallel",), vmem_limit_bytes=VMEM_LIMIT),
        name="ffn_final" if final else "ffn",
    )(h, pe_p, pe_s, nf, wfg, wfu, wfd, npl, wpg, wp, nfin)


def _perm_w_in(w_in):
    offs, acc = [], 0
    for s in (128, 128, 256, 256, 16, 256, 256, 256, 256, 4, 4, 256, 256, 256, 256, 256):
        offs.append(acc)
        acc += s
    (o_gq, o_gk, o_gv, o_gr, o_ga, o_dq, o_dk, o_dv, o_dz, o_da, o_db, o_cu, o_cv, o_sh, o_sb, o_sc) = offs
    col = lambda o, n: w_in[:, o:o + n]
    zeros = jnp.zeros((D_MODEL, LANE - GLA_RANK), w_in.dtype)
    return jnp.concatenate([
        col(o_gq, 128), col(o_gk, 128), col(o_gv, 256), col(o_gr, 256), col(o_ga, GLA_RANK), zeros,
        col(o_dq, 768), col(o_dz, 256),
        jnp.repeat(col(o_da, GDN_HEADS), GDN_DK, axis=1), jnp.repeat(col(o_db, GDN_HEADS), GDN_DK, axis=1),
        col(o_cu, 256), col(o_cv, 256), col(o_sh, 256), col(o_sb, 256), col(o_sc, 256)], axis=1)


def kernel(x_prompt, x_sample, state_gla, state_gdn, state_gdn_conv, state_sconv, p_prompt, p_sample, norm_mix, w_in, gla_wa2, gla_ba, gla_norm, gdn_conv_w, gdn_a_log, gdn_dt_bias, gdn_norm, cm_ln_g, cm_ln_b, cm_ws, cm_bs, sc_conv_w, w_gate, w_branch, w_o, norm_ffn, w_ffn_gate, w_ffn_up, w_ffn_down, norm_ple, w_ple_gate, w_ple, norm_final):
    depth = w_in.shape[0]
    bp, tp, _ = x_prompt.shape
    bs, ts, _ = x_sample.shape
    npt, nst = bp * tp, bs * ts
    sseq = ROWS // ts
    assert ts == HIST and tp % STEP_ROWS == 0 and nst % SAMPLE_STEP_ROWS == 0
    assert npt % TOKEN_TILE == 0 and nst % TOKEN_TILE == 0

    h = jnp.concatenate([x_prompt.reshape(npt, D_MODEL), x_sample.reshape(nst, D_MODEL)], axis=0)
    pe_p = p_prompt.reshape(depth, npt, PLE_DIM)
    pe_s = p_sample.reshape(depth, nst, PLE_DIM)
    s0_gla = state_gla.reshape(depth, bs, GLA_QK, GLA_DV)
    s0_gdn = state_gdn.reshape(depth, bs, GDN_QK, GDN_DV)
    row = lambda a: a.reshape(1, -1)
    outs = {k: [] for k in ("gla_p", "gla_s", "gdn_p", "gdn_s", "gc_p", "gc_s", "sc_p", "sc_s", "cv_s")}
    for i in range(depth):
        w1 = _perm_w_in(w_in[i]).astype(BF16)
        wa2 = jnp.concatenate([gla_wa2[i], jnp.zeros((LANE - GLA_RANK, GLA_QK), F32)], axis=0).astype(BF16)
        p = _inproj(h, row(norm_mix[i]), w1, wa2, row(gla_ba[i]), row(jnp.repeat(gdn_a_log[i], GDN_DK)),
                    row(jnp.repeat(gdn_dt_bias[i], GDN_DK)), row(cm_ln_g[i]), row(cm_ln_b[i]))

        cw = jnp.concatenate([gdn_conv_w[i], jnp.concatenate([jnp.zeros((1, BRANCH_W), F32), sc_conv_w[i]], axis=0)],
                             axis=1)
        hist_s = jnp.concatenate([
            jnp.pad(state_gdn_conv[i], ((0, 0), (HIST - (GDN_CONV - 1), 0), (0, 0))),
            jnp.pad(state_sconv[i], ((0, 0), (HIST - (SC_WIDTH - 1), 0), (0, 0)))], axis=2).reshape(nst, CONV_W)
        gn_gla = row(jnp.tile(gla_norm[i], GLA_HEADS))
        gn_gdn = row(jnp.tile(gdn_norm[i], GDN_HEADS))
        cmw_p = jnp.transpose(cm_ws[i], (1, 0, 2)).reshape(CM_CHUNK, CM_GROUPS * CM_CHUNK)
        cmb_p = jnp.repeat(cm_bs[i].T, BRANCH_W // CM_GROUPS, axis=1)
        cmw_s = jnp.transpose(jnp.tile(cm_ws[i][:, :ts, :ts], (1, sseq, sseq)), (1, 0, 2)).reshape(ROWS, CM_GROUPS * ROWS)
        cmb_s = jnp.tile(cmb_p[:ts], (sseq, 1))

        br_p, gla_p, gdn_p = _mixer_prompt(p, cw, gn_gla, gn_gdn, cmw_p, cmb_p, nseqs=bp, nsteps=tp // STEP_ROWS)
        br_s, gla_s, gdn_s = _mixer_sample(p, hist_s, s0_gla, s0_gdn, cw, gn_gla, gn_gdn, cmw_s, cmb_s, layer=i,
                                           base_step=npt // SAMPLE_STEP_ROWS, nsteps=nst // SAMPLE_STEP_ROWS, seg=ts)
        outs["gla_p"].append(gla_p.reshape(bp, GLA_HEADS, GLA_DK, GLA_DV))
        outs["gla_s"].append(gla_s.reshape(bs, GLA_HEADS, GLA_DK, GLA_DV))
        outs["gdn_p"].append(gdn_p.reshape(bp, GDN_HEADS, GDN_DK, GDN_DV))
        outs["gdn_s"].append(gdn_s.reshape(bs, GDN_HEADS, GDN_DK, GDN_DV))
        xp3 = p[:npt, P_GDN:P_GDN + CONV_W].reshape(bp, tp, CONV_W)
        xs3 = p[npt:, P_GDN:P_GDN + CONV_W].reshape(bs, ts, CONV_W)
        outs["gc_p"].append(xp3[:, tp - (GDN_CONV - 1):, 0:768])
        outs["gc_s"].append(xs3[:, ts - (GDN_CONV - 1):, 0:768])
        outs["sc_p"].append(xp3[:, tp - (SC_WIDTH - 1):, 768:])
        outs["sc_s"].append(xs3[:, ts - (SC_WIDTH - 1):, 768:])
        outs["cv_s"].append(p[npt:, P_CM + 256:P_CM + 512].reshape(bs, ts, BRANCH_W))

        h = _merge(h, br_p, br_s, row(norm_mix[i]), w_gate[i].astype(BF16), w_branch[i].astype(BF16),
                   w_o[i].astype(BF16))
        h = _ffn(h, pe_p, pe_s, row(norm_ffn[i]), w_ffn_gate[i].astype(BF16), w_ffn_up[i].astype(BF16),
                 w_ffn_down[i].astype(BF16), row(norm_ple[i]), w_ple_gate[i].astype(BF16), w_ple[i].astype(BF16),
                 row(norm_final), layer=i, final=(i == depth - 1))

    y_prompt = h[:npt].reshape(bp, tp, D_MODEL)
    y_sample = h[npt:].reshape(bs, ts, D_MODEL)
    st = lambda k: jnp.stack(outs[k])
    return (y_prompt, y_sample, st("gla_p"), st("gla_s"), st("gdn_p"), st("gdn_s"),
            st("gc_p"), st("gc_s"), st("sc_p"), st("sc_s"), st("cv_s"))
```

```python
import functools

import jax
import jax.numpy as jnp
from jax import lax
from jax.experimental import pallas as pl
from jax.experimental.pallas import tpu as pltpu

F32 = jnp.float32
BF16 = jnp.bfloat16

D_MODEL = 1024
PLE_DIM = 256
BRANCH_W = 256
N_BRANCH = 4
GLA_HEADS = 4
GLA_DK = 32
GLA_DV = 64
GLA_RANK = 16
GLA_TAU = 16.0
GDN_HEADS = 4
GDN_DK = 64
GDN_DV = 64
GDN_CONV = 4
CM_GROUPS = 4
CM_CHUNK = 128
SC_WIDTH = 3
D_FF = 2816
EPS = 1e-6

ROWS = 64
PROMPT_SEQS = 4
PROMPT_CHUNKS = 2
PROMPT_SEQ_ROWS = ROWS * PROMPT_CHUNKS
SAMPLE_BLOCKS_PER_STEP = 2
SAMPLE_STEP_ROWS = ROWS * SAMPLE_BLOCKS_PER_STEP
GLA_QK = GLA_HEADS * GLA_DK
GDN_QK = GDN_HEADS * GDN_DK
LANE = 128
CONV_W = 3 * BRANCH_W + BRANCH_W
HIST = 8

C_GQ, C_GK, C_GV, C_GR, C_GA = 0, 128, 256, 512, 768
C_DQKV, C_DZ, C_DA, C_DB = 896, 1664, 1920, 2176
C_CU, C_CV = 2432, 2688
C_SH, C_SB, C_SC = 2944, 3200, 3456
W1_COLS = 3712
P_GLA = 0
P_GDN = 896
P_CM = 2944
P_COLS = 3456

VMEM_LIMIT = 56 * 1024 * 1024
TOKEN_TILE = 512


def _dot(a, b):
    return jnp.dot(a, b, preferred_element_type=F32)


def _dot_nt(a, b):
    return lax.dot_general(a, b, (((1,), (1,)), ((), ())), preferred_element_type=F32)


def _dot_tn(a, b):
    return lax.dot_general(a, b, (((0,), (0,)), ((), ())), preferred_element_type=F32)


def _split(x, n):
    parts, r = [], x
    for i in range(n):
        p = r.astype(BF16)
        parts.append(p)
        if i + 1 < n:
            r = r - p.astype(F32)
    return parts


def _dot1(a, b, dot=_dot):
    return dot(a.astype(BF16), b.astype(BF16))


def _dot3(a, b_pieces):
    ah, al = _split(a, 2)
    bh, bl = b_pieces
    return _dot(jnp.concatenate([ah, ah, al], axis=1), jnp.concatenate([bh, bl, bh], axis=0))


def _mask_dot(mask, x, n):
    return _dot(jnp.concatenate([mask.astype(BF16)] * n, axis=1), jnp.concatenate(_split(x, n), axis=0))


def _mask_dot_nt(mask, x, n):
    return _dot_nt(jnp.concatenate([mask.astype(BF16)] * n, axis=1), jnp.concatenate(_split(x, n), axis=1))


def _mask_dot_rhs(x, mask, n):
    return _dot(jnp.concatenate(_split(x, n), axis=1), jnp.concatenate([mask.astype(BF16)] * n, axis=0))


def _sigmoid(x):
    return 1.0 / (1.0 + jnp.exp(-x))


def _silu(x):
    return x * _sigmoid(x)


def _softplus(x):
    return jnp.maximum(x, 0.0) + jnp.log1p(jnp.exp(-jnp.abs(x)))


def _gelu_tanh(x):
    return 0.5 * x * (1.0 + jnp.tanh(0.7978845608028654 * (x + 0.044715 * (x * x * x))))


def _rms(x, w):
    return x * lax.rsqrt(jnp.mean(x * x, axis=-1, keepdims=True) + EPS) * w


def _idiv(x, n):
    assert n & (n - 1) == 0
    return lax.shift_right_logical(x, n.bit_length() - 1)


def _imod(x, n):
    assert n & (n - 1) == 0
    return lax.bitwise_and(x, n - 1)


def _const_spec(shape):
    return pl.BlockSpec(shape, lambda *_: (0,) * len(shape))


def _layer_spec(shape, layer, buffers=None):
    mode = {} if buffers is None else {"pipeline_mode": pl.Buffered(buffers)}
    return pl.BlockSpec((None,) + shape, lambda *_: (layer,) + (0,) * len(shape), **mode)


def _iota2(shape):
    return lax.broadcasted_iota(jnp.int32, shape, 0), lax.broadcasted_iota(jnp.int32, shape, 1)


def _inproj_kernel(h_ref, nw_ref, w_ref, wa2_ref, ba_ref, alog_ref, dtb_ref, lng_ref, lnb_ref, p_ref):
    xn = _rms(h_ref[...], nw_ref[...]).astype(BF16)
    proj = _dot(xn, w_ref[...])
    p_ref[:, 0:128] = proj[:, C_GQ:C_GQ + 128] * (GLA_DK ** -0.5)
    p_ref[:, 128:512] = proj[:, C_GK:C_GR]
    p_ref[:, 512:768] = _silu(proj[:, C_GR:C_GR + 256])
    za = _dot(proj[:, C_GA:C_GA + LANE].astype(BF16), wa2_ref[...]) + ba_ref[...]
    p_ref[:, 768:896] = -_softplus(-za) * (1.0 / GLA_TAU)
    g0 = P_GDN
    p_ref[:, g0:g0 + 768] = proj[:, C_DQKV:C_DQKV + 768]
    p_ref[:, g0 + 768:g0 + 1024] = proj[:, C_SC:C_SC + 256] * proj[:, C_SH:C_SH + 256]
    p_ref[:, g0 + 1024:g0 + 1280] = _silu(proj[:, C_DZ:C_DZ + 256])
    p_ref[:, g0 + 1280:g0 + 1536] = proj[:, C_SB:C_SB + 256]
    p_ref[:, g0 + 1536:g0 + 1792] = -jnp.exp(alog_ref[...]) * _softplus(proj[:, C_DA:C_DA + 256] + dtb_ref[...])
    p_ref[:, g0 + 1792:g0 + 2048] = _sigmoid(proj[:, C_DB:C_DB + 256])
    p_ref[:, P_CM:P_CM + 256] = _gelu_tanh(proj[:, C_CU:C_CU + 256])
    gv = _gelu_tanh(proj[:, C_CV:C_CV + 256])
    mu = jnp.mean(gv, axis=-1, keepdims=True)
    d = gv - mu
    var = jnp.mean(d * d, axis=-1, keepdims=True)
    p_ref[:, P_CM + 256:P_CM + 512] = d * lax.rsqrt(var + EPS) * lng_ref[...] + lnb_ref[...]


def _inproj(h, nw, w1, wa2, ba, alog, dtb, lng, lnb, *, layer):
    ntok = h.shape[0]
    tm = TOKEN_TILE
    row = lambda n: pl.BlockSpec((tm, n), lambda i: (i, 0))
    return pl.pallas_call(
        _inproj_kernel,
        grid=(ntok // tm,),
        in_specs=[row(D_MODEL), _const_spec((1, D_MODEL)), _layer_spec((D_MODEL, W1_COLS), layer),
                  _const_spec((LANE, GLA_QK)), _const_spec((1, GLA_QK)), _const_spec((1, 256)),
                  _const_spec((1, 256)), _const_spec((1, 256)), _const_spec((1, 256))],
        out_specs=row(P_COLS),
        out_shape=jax.ShapeDtypeStruct((ntok, P_COLS), F32),
        compiler_params=pltpu.CompilerParams(dimension_semantics=("parallel",), vmem_limit_bytes=VMEM_LIMIT),
        name="inproj",
    )(h, nw, w1, wa2, ba, alog, dtb, lng, lnb)


def _stack_heads(x, group, nheads, period=None):
    w = x.shape[1]
    li = lax.broadcasted_iota(jnp.int32, (1, w), 1)
    if period is not None:
        li = _imod(li, period)
    hid = _idiv(li, group)
    zero = jnp.zeros_like(x)
    return jnp.concatenate([jnp.where(hid == h, x, zero) for h in range(nheads)], axis=0)


def _block_diag(x, nblocks):
    r = x.shape[0]
    ri, ci = _iota2((nblocks * r, nblocks * r))
    return jnp.where(_idiv(ri, r) == _idiv(ci, r), jnp.concatenate([x] * nblocks, axis=0), jnp.zeros((), x.dtype))


def _widen(x, nseq):
    if nseq == 1:
        return x
    seg = ROWS // nseq
    sid = _idiv(lax.broadcasted_iota(jnp.int32, (ROWS, 1), 0), seg)
    zero = jnp.zeros_like(x)
    return jnp.concatenate([jnp.where(sid == j, x, zero) for j in range(nseq)], axis=1)


def _group_mean(x, group):
    w = x.shape[1]
    ri, ci = _iota2((w, w))
    avg = jnp.where(_idiv(ri, group) == _idiv(ci, group), 1.0 / group, 0.0).astype(BF16)
    return _mask_dot_rhs(x, avg, 2)


def _head_diag_mask(rows, cols, rhead, chead, rper):
    ri, ci = _iota2((rows, cols))
    return (_idiv(_imod(ri, rper), rhead) == _idiv(ci, chead)).astype(F32)


def _expand_state(s, width, reps):
    ri, ci = _iota2((width, reps * width))
    return _mask_dot_rhs(s, ri == _imod(ci, width), 3)


def _compact_state(st, width, reps):
    out = st[:, 0:width]
    for h in range(1, reps):
        out = out + st[:, h * width:(h + 1) * width]
    return out


class _Masks:
    def __init__(self, seg):
        ri, ci = _iota2((ROWS, ROWS))
        self.same = _idiv(ri, seg) == _idiv(ci, seg)
        self.tri = self.same & (ci <= ri)
        rl, cl = _iota2((ROWS, GDN_HEADS * ROWS))
        cl = _imod(cl, ROWS)
        same_l = _idiv(rl, seg) == _idiv(cl, seg)
        self.tri_l = same_l & (cl <= rl)
        self.strict_l = same_l & (cl < rl)
        self.eye_l = (rl == cl).astype(F32)
        self.ones = jnp.ones((ROWS, ROWS), BF16)
        self.seg = seg
        nseq = ROWS // seg
        self.bd_gla = _head_diag_mask(nseq * GLA_QK, GLA_HEADS * GLA_DV, GLA_DK, GLA_DV, GLA_QK)
        self.bd_gdn = _head_diag_mask(nseq * GDN_QK, GDN_QK, GDN_DK, GDN_DV, GDN_QK)


def _gla_prep(p, m, nseq):
    q, k, v, la = p[:, 0:128], p[:, 128:256], p[:, 256:512], p[:, 768:896]
    b = _mask_dot(m.tri, la, 3)
    if m.seg == ROWS:
        btot = jnp.broadcast_to(b[ROWS - 1:ROWS], b.shape)
    else:
        btot = _mask_dot(m.same, la, 3)
    qd = q * jnp.exp(b)
    kd = k * jnp.exp(-b)
    ke = k * jnp.exp(btot - b)
    a = jnp.where(m.tri_l, _dot1(qd, _stack_heads(kd, GLA_DK, GLA_HEADS), _dot_nt), 0.0)
    o_intra = _dot1(a, _stack_heads(v, GLA_DV, GLA_HEADS))
    dec = jnp.exp(_dot_tn(jnp.concatenate(_split(_widen(la, nseq), 3), axis=0),
                          jnp.ones((3 * ROWS, GLA_HEADS * GLA_DV), BF16)))
    ds = _dot1(_widen(ke, nseq), v, _dot_tn) * m.bd_gla
    return o_intra, _widen(qd, nseq), dec, ds


def _gla_scan(preps, sts):
    outs = [prep[0] + _dot1(prep[1], st) for prep, st in zip(preps, sts)]
    return outs, [st * prep[2] + prep[3] for prep, st in zip(preps, sts)]


def _gla_out(o, gn, rs):
    return o * lax.rsqrt(_group_mean(o * o, GLA_DV) + EPS) * gn * rs


def _gdn_qk_norm(qkv):
    hd = GDN_DK
    r = qkv.shape[0]
    cqk = jnp.concatenate([qkv[:, 0:256], qkv[:, 256:512]], axis=0)
    nrm = lax.rsqrt(_group_mean(cqk * cqk, hd) * hd + EPS)
    return qkv[:, 0:256] * nrm[0:r] * (hd ** -0.5), qkv[:, 256:512] * nrm[r:2 * r]


def _gdn_prep(q, k, cv, gs, betas, m, seg):
    nh, hd = GDN_HEADS, GDN_DK
    n = range(len(q))
    gc = [_mask_dot(m.tri, gs[i], 3) for i in n]
    if seg == ROWS:
        gtot = [jnp.broadcast_to(gc[i][ROWS - 1:ROWS], gc[i].shape) for i in n]
    else:
        gtot = [_mask_dot(m.same, gs[i], 3) for i in n]
    qkk = [_dot1(jnp.concatenate([q[i], k[i]], axis=0), _stack_heads(k[i], hd, nh), _dot_nt) for i in n]
    grow = [_mask_dot(m.ones, gc[i] * m.eye_l, 3) for i in n]
    decay = [jnp.where(m.tri_l, jnp.exp(jnp.where(m.tri_l, gc[i] - grow[i], 0.0)), 0.0) for i in n]
    amat = [jnp.where(m.strict_l, betas[i] * decay[i] * qkk[i][ROWS:2 * ROWS], 0.0) for i in n]
    inv = [m.eye_l - amat[i] for i in n]
    pw = amat
    pw_bd = [_block_diag(pw[i].astype(BF16), nh) for i in n]
    for _ in range(max(1, (seg - 1).bit_length()) - 1):
        pw = [_dot(pw[i].astype(BF16), pw_bd[i]) for i in n]
        pw_bd = [_block_diag(pw[i].astype(BF16), nh) for i in n]
        inv = [inv[i] + _dot(inv[i].astype(BF16), pw_bd[i]) for i in n]
    eg = [jnp.exp(gc[i]) for i in n]
    rhs = [jnp.concatenate([betas[i] * cv[i], betas[i] * eg[i] * k[i]], axis=1) for i in n]
    uw = [_dot(inv[i].astype(BF16), _stack_heads(rhs[i].astype(BF16), hd, nh, period=GDN_QK)) for i in n]
    return [(uw[i][:, 0:256], uw[i][:, 256:512], q[i] * eg[i], qkk[i][0:ROWS] * decay[i],
             k[i] * jnp.exp(gtot[i] - gc[i]), jnp.exp(gtot[i])) for i in n]


def _gdn_scan(preps, sts, m, nseq):
    seg = ROWS // nseq
    n = range(len(preps))
    ws = [_dot1(jnp.concatenate([_widen(preps[i][1], nseq), _widen(preps[i][2], nseq)], axis=0), sts[i]) for i in n]
    u = [preps[i][0] - ws[i][0:ROWS] for i in n]
    outs = [ws[i][ROWS:2 * ROWS] + _dot1(preps[i][3], _stack_heads(u[i], GDN_DV, GDN_HEADS)) for i in n]
    new = []
    for i in n:
        dn = preps[i][5]
        dn_tall = jnp.concatenate(
            [jnp.broadcast_to(dn[j * seg:j * seg + 1], (GDN_QK, GDN_QK)) for j in range(nseq)], axis=0)
        new.append(sts[i] * dn_tall + _dot1(_widen(preps[i][4], nseq), u[i], _dot_tn) * m.bd_gdn)
    return outs, new


def _gdn_out(o, gn, zs):
    return o * lax.rsqrt(_group_mean(o * o, GDN_DV) + EPS) * gn * zs


def _cm_block(p, ws, bias, seg):
    r = p.shape[0]
    gu, vn = p[:, 0:256], p[:, 256:512]
    ri, ci = _iota2((r, CM_GROUPS * r))
    ci = _imod(ci, r)
    wm = jnp.where((_idiv(ri, seg) == _idiv(ci, seg)) & (ci <= ri), ws, 0.0)
    return gu * (_dot1(wm, _stack_heads(vn, BRANCH_W // CM_GROUPS, CM_GROUPS)) + bias)


def _conv_prompt(x, hist, cw):
    n = x.shape[0]
    t8 = lax.broadcasted_iota(jnp.int32, (HIST, 1), 0)
    acc = cw[3:4] * x
    for d in range(1, GDN_CONV):
        xr = pltpu.roll(x, d, 0)
        head = jnp.where(t8 < d, pltpu.roll(hist, d, 0), xr[0:HIST])
        acc = acc + cw[3 - d:4 - d] * jnp.concatenate([head, xr[HIST:n]], axis=0)
    return acc


def _conv_sample(x, hist, cw, seg):
    n = x.shape[0]
    tloc = _imod(lax.broadcasted_iota(jnp.int32, (n, 1), 0), seg)
    acc = cw[3:4] * x
    for d in range(1, GDN_CONV):
        prev = jnp.where(tloc < d, pltpu.roll(hist, (d - seg) % n, 0), pltpu.roll(x, d, 0))
        acc = acc + cw[3 - d:4 - d] * prev
    return acc


def _block_rows(c):
    return slice(c * ROWS, (c + 1) * ROWS)


def _mixer_prompt_kernel(*refs):
    ns, nc, sr = PROMPT_SEQS, PROMPT_CHUNKS, PROMPT_SEQ_ROWS
    p_refs = refs[:ns]
    (cw_ref, gn_gla_ref, gn_gdn_ref, cmw_ref, cmb_ref,
     o_ref, sgla_ref, sgdn_ref, tail_ref, st_gla, st_gdn, hist) = refs[ns:]

    @pl.when(pl.program_id(1) == 0)
    def _():
        st_gla[...] = jnp.zeros_like(st_gla)
        st_gdn[...] = jnp.zeros_like(st_gdn)
        hist[...] = jnp.zeros_like(hist)

    m = _Masks(ROWS)
    g0 = P_GDN
    xs, qkvs = [], []
    for s in range(ns):
        x = p_refs[s][:, g0:g0 + CONV_W]
        acc = _conv_prompt(x, hist[s], cw_ref[...])
        hist[s] = x[sr - HIST:sr]
        o_ref[s, :, 768:1024] = p_refs[s][:, g0 + 1280:g0 + 1536] * acc[:, 768:1024]
        xs.append(x)
        qkvs.append(_silu(acc[:, 0:768]))
    qkv = jnp.concatenate(qkvs, axis=0)
    qn, kn = _gdn_qk_norm(qkv)
    blocks = [(s, c) for c in range(nc) for s in range(ns)]
    rows_of = lambda s, c: slice(s * sr + c * ROWS, s * sr + (c + 1) * ROWS)
    gdn = _gdn_prep([qn[rows_of(s, c)] for s, c in blocks], [kn[rows_of(s, c)] for s, c in blocks],
                    [qkv[rows_of(s, c), 512:768] for s, c in blocks],
                    [p_refs[s][_block_rows(c), g0 + 1536:g0 + 1792] for s, c in blocks],
                    [p_refs[s][_block_rows(c), g0 + 1792:g0 + 2048] for s, c in blocks], m, ROWS)
    gla = [_gla_prep(p_refs[s][_block_rows(c), P_GLA:P_GLA + 896], m, 1) for s, c in blocks]
    sg = [st_gla[s] for s in range(ns)]
    sd = [st_gdn[s] for s in range(ns)]
    og, od = {}, {}
    for c in range(nc):
        o, sg = _gla_scan(gla[c * ns:(c + 1) * ns], sg)
        og.update({(s, c): o[s] for s in range(ns)})
        o, sd = _gdn_scan(gdn[c * ns:(c + 1) * ns], sd, m, 1)
        od.update({(s, c): o[s] for s in range(ns)})
    by_rows = lambda d: jnp.concatenate([d[(s, c)] for s in range(ns) for c in range(nc)], axis=0)
    rs = jnp.concatenate([p_refs[s][:, 512:768] for s in range(ns)], axis=0)
    zs = jnp.concatenate([p_refs[s][:, g0 + 1024:g0 + 1280] for s in range(ns)], axis=0)
    o_gla = _gla_out(by_rows(og), gn_gla_ref[...], rs)
    o_gdn = _gdn_out(by_rows(od), gn_gdn_ref[...], zs)
    for s in range(ns):
        st_gla[s] = sg[s]
        st_gdn[s] = sd[s]
        o_ref[s, :, 0:256] = o_gla[s * sr:(s + 1) * sr]
        o_ref[s, :, 256:512] = o_gdn[s * sr:(s + 1) * sr]
        for c in range(sr // CM_CHUNK):
            rows = slice(c * CM_CHUNK, (c + 1) * CM_CHUNK)
            o_ref[s, rows, 512:768] = _cm_block(p_refs[s][rows, P_CM:P_CM + 512], cmw_ref[...], cmb_ref[...], CM_CHUNK)

    @pl.when(pl.program_id(1) == pl.num_programs(1) - 1)
    def _():
        for s in range(ns):
            sgla_ref[s] = _compact_state(sg[s], GLA_DV, GLA_HEADS)
            sgdn_ref[s] = _compact_state(sd[s], GDN_DV, GDN_HEADS)
            tail_ref[s] = xs[s][sr - HIST:sr]


def _mixer_prompt(p, cw, gn_gla, gn_gdn, cmw, cmb, *, nseqs, seq_len):
    ns, sr = PROMPT_SEQS, PROMPT_SEQ_ROWS
    nsteps = seq_len // sr
    seq_spec = lambda s: pl.BlockSpec((sr, P_COLS), lambda o, c: ((o * ns + s) * nsteps + c, 0))
    per_group = lambda *shape: pl.BlockSpec((ns,) + shape, lambda o, c: (o,) + (0,) * len(shape))
    return pl.pallas_call(
        _mixer_prompt_kernel,
        grid=(nseqs // ns, nsteps),
        in_specs=[seq_spec(s) for s in range(ns)] + [
            _const_spec((GDN_CONV, CONV_W)), _const_spec((1, 256)), _const_spec((1, 256)),
            _const_spec((CM_CHUNK, CM_GROUPS * CM_CHUNK)), _const_spec((CM_CHUNK, 256))],
        out_specs=[pl.BlockSpec((ns, sr, 4 * BRANCH_W), lambda o, c: (o, c, 0)),
                   per_group(GLA_QK, GLA_DV), per_group(GDN_QK, GDN_DV), per_group(HIST, CONV_W)],
        out_shape=[jax.ShapeDtypeStruct((nseqs, seq_len, 4 * BRANCH_W), F32),
                   jax.ShapeDtypeStruct((nseqs, GLA_QK, GLA_DV), F32),
                   jax.ShapeDtypeStruct((nseqs, GDN_QK, GDN_DV), F32),
                   jax.ShapeDtypeStruct((nseqs, HIST, CONV_W), F32)],
        scratch_shapes=[pltpu.VMEM((ns, GLA_QK, GLA_HEADS * GLA_DV), F32), pltpu.VMEM((ns, GDN_QK, GDN_QK), F32),
                        pltpu.VMEM((ns, HIST, CONV_W), F32)],
        compiler_params=pltpu.CompilerParams(dimension_semantics=("arbitrary", "arbitrary"),
                                             vmem_limit_bytes=VMEM_LIMIT),
        name="mixer_prompt",
    )(*([p] * ns), cw, gn_gla, gn_gdn, cmw, cmb)


def _mixer_sample_kernel(p_ref, hist_ref, s0gla_ref, s0gdn_ref, cw_ref, gn_gla_ref, gn_gdn_ref, cmw_ref, cmb_ref,
                         o_ref, sgla_ref, sgdn_ref, *, seg):
    nseq = ROWS // seg
    nb = range(SAMPLE_BLOCKS_PER_STEP)
    m = _Masks(seg)
    g0 = P_GDN
    x = p_ref[:, g0:g0 + CONV_W]
    acc = _conv_sample(x, hist_ref[...], cw_ref[...], seg)
    o_ref[:, 768:1024] = p_ref[:, g0 + 1280:g0 + 1536] * acc[:, 768:1024]
    qkv = _silu(acc[:, 0:768])
    qn, kn = _gdn_qk_norm(qkv)
    gdn = _gdn_prep([qn[_block_rows(c)] for c in nb], [kn[_block_rows(c)] for c in nb],
                    [qkv[_block_rows(c), 512:768] for c in nb],
                    [p_ref[_block_rows(c), g0 + 1536:g0 + 1792] for c in nb],
                    [p_ref[_block_rows(c), g0 + 1792:g0 + 2048] for c in nb], m, seg)
    gla = [_gla_prep(p_ref[_block_rows(c), P_GLA:P_GLA + 896], m, nseq) for c in nb]
    seqs = lambda c: slice(c * nseq, (c + 1) * nseq)
    sg = [_expand_state(s0gla_ref[seqs(c)].reshape(nseq * GLA_QK, GLA_DV), GLA_DV, GLA_HEADS) * m.bd_gla for c in nb]
    sd = [_expand_state(s0gdn_ref[seqs(c)].reshape(nseq * GDN_QK, GDN_DV), GDN_DV, GDN_HEADS) * m.bd_gdn for c in nb]
    og, sg = _gla_scan(gla, sg)
    od, sd = _gdn_scan(gdn, sd, m, nseq)
    for c in nb:
        rows = _block_rows(c)
        sgla_ref[seqs(c)] = _compact_state(sg[c], GLA_DV, GLA_HEADS).reshape(nseq, GLA_QK, GLA_DV)
        sgdn_ref[seqs(c)] = _compact_state(sd[c], GDN_DV, GDN_HEADS).reshape(nseq, GDN_QK, GDN_DV)
        o_ref[rows, 512:768] = _cm_block(p_ref[rows, P_CM:P_CM + 512], cmw_ref[...], cmb_ref[...], seg)
    o_ref[:, 0:256] = _gla_out(jnp.concatenate(og, axis=0), gn_gla_ref[...], p_ref[:, 512:768])
    o_ref[:, 256:512] = _gdn_out(jnp.concatenate(od, axis=0), gn_gdn_ref[...], p_ref[:, g0 + 1024:g0 + 1280])


def _mixer_sample(p, hist, s0gla, s0gdn, cw, gn_gla, gn_gdn, cmw, cmb, *, layer, base_step, nsteps, seg):
    rows = SAMPLE_STEP_ROWS
    nseq_step = rows // seg
    return pl.pallas_call(
        functools.partial(_mixer_sample_kernel, seg=seg),
        grid=(nsteps,),
        in_specs=[pl.BlockSpec((rows, P_COLS), lambda i: (base_step + i, 0)),
                  pl.BlockSpec((rows, CONV_W), lambda i: (i, 0)),
                  pl.BlockSpec((None, nseq_step, GLA_QK, GLA_DV), lambda i: (layer, i, 0, 0)),
                  pl.BlockSpec((None, nseq_step, GDN_QK, GDN_DV), lambda i: (layer, i, 0, 0)),
                  _const_spec((GDN_CONV, CONV_W)), _const_spec((1, 256)), _const_spec((1, 256)),
                  _const_spec((ROWS, CM_GROUPS * ROWS)), _const_spec((ROWS, 256))],
        out_specs=[pl.BlockSpec((rows, 4 * BRANCH_W), lambda i: (i, 0)),
                   pl.BlockSpec((nseq_step, GLA_QK, GLA_DV), lambda i: (i, 0, 0)),
                   pl.BlockSpec((nseq_step, GDN_QK, GDN_DV), lambda i: (i, 0, 0))],
        out_shape=[jax.ShapeDtypeStruct((nsteps * rows, 4 * BRANCH_W), F32),
                   jax.ShapeDtypeStruct((nsteps * nseq_step, GLA_QK, GLA_DV), F32),
                   jax.ShapeDtypeStruct((nsteps * nseq_step, GDN_QK, GDN_DV), F32)],
        compiler_params=pltpu.CompilerParams(dimension_semantics=("parallel",), vmem_limit_bytes=VMEM_LIMIT),
        name="mixer_sample",
    )(p, hist, s0gla, s0gdn, cw, gn_gla, gn_gdn, cmw, cmb)


def _merge_kernel(h_ref, bp_ref, bs_ref, nw_ref, wg_ref, wb_ref, wo_ref, o_ref, *, prompt_tiles):
    h = h_ref[...]
    xn = _rms(h, nw_ref[...]).astype(BF16)
    br = jnp.where(pl.program_id(0) < prompt_tiles, bp_ref[...], bs_ref[...]).astype(BF16)
    merged = None
    for gi in range(N_BRANCH):
        gate = _sigmoid(_dot(xn, wg_ref[:, gi * D_MODEL:(gi + 1) * D_MODEL]))
        term = _dot(br[:, gi * BRANCH_W:(gi + 1) * BRANCH_W], wb_ref[gi]) * gate
        merged = term if merged is None else merged + term
    o_ref[...] = h + _dot(merged.astype(BF16), wo_ref[...])


def _stream_specs(tm, width, prompt_tiles, lead=()):
    nlead = (None,) * len(lead)
    return (pl.BlockSpec(nlead + (tm, width), lambda i: lead + (jnp.minimum(i, prompt_tiles - 1), 0)),
            pl.BlockSpec(nlead + (tm, width), lambda i: lead + (jnp.maximum(i - prompt_tiles, 0), 0)))


def _merge(h, br_p, br_s, nw, wg, wb, wo, *, layer):
    ntok = h.shape[0]
    tm = TOKEN_TILE
    prompt_tiles = br_p.shape[0] // tm
    row = lambda n: pl.BlockSpec((tm, n), lambda i: (i, 0))
    return pl.pallas_call(
        functools.partial(_merge_kernel, prompt_tiles=prompt_tiles),
        grid=(ntok // tm,),
        in_specs=[row(D_MODEL), *_stream_specs(tm, 4 * BRANCH_W, prompt_tiles), _const_spec((1, D_MODEL)),
                  _layer_spec((D_MODEL, N_BRANCH * D_MODEL), layer),
                  _layer_spec((N_BRANCH, BRANCH_W, D_MODEL), layer), _layer_spec((D_MODEL, D_MODEL), layer)],
        out_specs=row(D_MODEL),
        out_shape=jax.ShapeDtypeStruct((ntok, D_MODEL), F32),
        compiler_params=pltpu.CompilerParams(dimension_semantics=("parallel",), vmem_limit_bytes=VMEM_LIMIT),
        name="merge",
    )(h, br_p, br_s, nw, wg, wb, wo)


def _ffn_kernel(h_ref, pp_ref, ps_ref, nf_ref, wfg_ref, wfu_ref, wfd_ref, np_ref, wpg_ref, wp_ref, nfin_ref, o_ref, *,
                final, prompt_tiles):
    h = h_ref[...]
    xf = _rms(h, nf_ref[...]).astype(BF16)
    act = _silu(_dot(xf, wfg_ref[...])) * _dot(xf, wfu_ref[...])
    h = h + _dot(act.astype(BF16), wfd_ref[...])
    pg = _sigmoid(_dot(_rms(h, np_ref[...]).astype(BF16), wpg_ref[...]))
    pe = jnp.where(pl.program_id(0) < prompt_tiles, pp_ref[...], ps_ref[...]).astype(BF16)
    h = h + pg * _dot(pe, wp_ref[...])
    o_ref[...] = _rms(h, nfin_ref[...]) if final else h


def _ffn(h, pe_p, pe_s, nf, wfg, wfu, wfd, npl, wpg, wp, nfin, *, layer, final):
    ntok = h.shape[0]
    tm = TOKEN_TILE
    prompt_tiles = pe_p.shape[1] // tm
    row = lambda n: pl.BlockSpec((tm, n), lambda i: (i, 0))
    once = lambda shape: _layer_spec(shape, layer, buffers=1)
    return pl.pallas_call(
        functools.partial(_ffn_kernel, final=final, prompt_tiles=prompt_tiles),
        grid=(ntok // tm,),
        in_specs=[row(D_MODEL), *_stream_specs(tm, PLE_DIM, prompt_tiles, lead=(layer,)), _const_spec((1, D_MODEL)),
                  once((D_MODEL, D_FF)), once((D_MODEL, D_FF)), once((D_FF, D_MODEL)),
                  _const_spec((1, D_MODEL)), once((D_MODEL, D_MODEL)), once((PLE_DIM, D_MODEL)),
                  _const_spec((1, D_MODEL))],
        out_specs=row(D_MODEL),
        out_shape=jax.ShapeDtypeStruct((ntok, D_MODEL), F32),
        compiler_params=pltpu.CompilerParams(dimension_semantics=("parallel",), vmem_limit_bytes=VMEM_LIMIT),
        name="ffn_final" if final else "ffn",
    )(h, pe_p, pe_s, nf, wfg, wfu, wfd, npl, wpg, wp, nfin)


def _perm_w_in(w_in):
    offs, acc = [], 0
    for s in (128, 128, 256, 256, 16, 256, 256, 256, 256, 4, 4, 256, 256, 256, 256, 256):
        offs.append(acc)
        acc += s
    (o_gq, o_gk, o_gv, o_gr, o_ga, o_dq, o_dk, o_dv, o_dz, o_da, o_db, o_cu, o_cv, o_sh, o_sb, o_sc) = offs
    col = lambda o, n: w_in[..., o:o + n]
    zeros = jnp.zeros(w_in.shape[:-1] + (LANE - GLA_RANK,), w_in.dtype)
    return jnp.concatenate([
        col(o_gq, 128), col(o_gk, 128), col(o_gv, 256), col(o_gr, 256), col(o_ga, GLA_RANK), zeros,
        col(o_dq, 768), col(o_dz, 256),
        jnp.repeat(col(o_da, GDN_HEADS), GDN_DK, axis=-1), jnp.repeat(col(o_db, GDN_HEADS), GDN_DK, axis=-1),
        col(o_cu, 256), col(o_cv, 256), col(o_sh, 256), col(o_sb, 256), col(o_sc, 256)], axis=-1)


def kernel(x_prompt, x_sample, state_gla, state_gdn, state_gdn_conv, state_sconv, p_prompt, p_sample, norm_mix, w_in, gla_wa2, gla_ba, gla_norm, gdn_conv_w, gdn_a_log, gdn_dt_bias, gdn_norm, cm_ln_g, cm_ln_b, cm_ws, cm_bs, sc_conv_w, w_gate, w_branch, w_o, norm_ffn, w_ffn_gate, w_ffn_up, w_ffn_down, norm_ple, w_ple_gate, w_ple, norm_final):
    depth = w_in.shape[0]
    bp, tp, _ = x_prompt.shape
    bs, ts, _ = x_sample.shape
    npt, nst = bp * tp, bs * ts
    sseq = ROWS // ts
    assert ts == HIST and nst % SAMPLE_STEP_ROWS == 0
    assert bp % PROMPT_SEQS == 0 and tp % PROMPT_SEQ_ROWS == 0 and PROMPT_SEQ_ROWS % CM_CHUNK == 0
    assert npt % TOKEN_TILE == 0 and nst % TOKEN_TILE == 0

    h = jnp.concatenate([x_prompt.reshape(npt, D_MODEL), x_sample.reshape(nst, D_MODEL)], axis=0)
    pe_p = p_prompt.reshape(depth, npt, PLE_DIM)
    pe_s = p_sample.reshape(depth, nst, PLE_DIM)
    s0_gla = state_gla.reshape(depth, bs, GLA_QK, GLA_DV)
    s0_gdn = state_gdn.reshape(depth, bs, GDN_QK, GDN_DV)
    row = lambda a: a.reshape(1, -1)
    w1 = _perm_w_in(w_in).astype(BF16)
    wg, wb, wo = w_gate.astype(BF16), w_branch.astype(BF16), w_o.astype(BF16)
    wfg, wfu, wfd = w_ffn_gate.astype(BF16), w_ffn_up.astype(BF16), w_ffn_down.astype(BF16)
    wpg, wp = w_ple_gate.astype(BF16), w_ple.astype(BF16)
    outs = {k: [] for k in ("gla_p", "gla_s", "gdn_p", "gdn_s", "gc_p", "gc_s", "sc_p", "sc_s", "cv_s")}
    for i in range(depth):
        wa2 = jnp.concatenate([gla_wa2[i], jnp.zeros((LANE - GLA_RANK, GLA_QK), F32)], axis=0).astype(BF16)
        p = _inproj(h, row(norm_mix[i]), w1, wa2, row(gla_ba[i]), row(jnp.repeat(gdn_a_log[i], GDN_DK)),
                    row(jnp.repeat(gdn_dt_bias[i], GDN_DK)), row(cm_ln_g[i]), row(cm_ln_b[i]), layer=i)

        cw = jnp.concatenate([gdn_conv_w[i], jnp.concatenate([jnp.zeros((1, BRANCH_W), F32), sc_conv_w[i]], axis=0)],
                             axis=1)
        hist_s = jnp.concatenate([
            jnp.pad(state_gdn_conv[i], ((0, 0), (HIST - (GDN_CONV - 1), 0), (0, 0))),
            jnp.pad(state_sconv[i], ((0, 0), (HIST - (SC_WIDTH - 1), 0), (0, 0)))], axis=2).reshape(nst, CONV_W)
        gn_gla = row(jnp.tile(gla_norm[i], GLA_HEADS))
        gn_gdn = row(jnp.tile(gdn_norm[i], GDN_HEADS))
        cmw_p = jnp.transpose(cm_ws[i], (1, 0, 2)).reshape(CM_CHUNK, CM_GROUPS * CM_CHUNK)
        cmb_p = jnp.repeat(cm_bs[i].T, BRANCH_W // CM_GROUPS, axis=1)
        cmw_s = jnp.transpose(jnp.tile(cm_ws[i][:, :ts, :ts], (1, sseq, sseq)), (1, 0, 2)).reshape(ROWS, CM_GROUPS * ROWS)
        cmb_s = jnp.tile(cmb_p[:ts], (sseq, 1))

        br_p, gla_p, gdn_p, tail_p = _mixer_prompt(p, cw, gn_gla, gn_gdn, cmw_p, cmb_p, nseqs=bp, seq_len=tp)
        br_p = br_p.reshape(npt, N_BRANCH * BRANCH_W)
        br_s, gla_s, gdn_s = _mixer_sample(p, hist_s, s0_gla, s0_gdn, cw, gn_gla, gn_gdn, cmw_s, cmb_s, layer=i,
                                           base_step=npt // SAMPLE_STEP_ROWS, nsteps=nst // SAMPLE_STEP_ROWS, seg=ts)
        outs["gla_p"].append(gla_p.reshape(bp, GLA_HEADS, GLA_DK, GLA_DV))
        outs["gla_s"].append(gla_s.reshape(bs, GLA_HEADS, GLA_DK, GLA_DV))
        outs["gdn_p"].append(gdn_p.reshape(bp, GDN_HEADS, GDN_DK, GDN_DV))
        outs["gdn_s"].append(gdn_s.reshape(bs, GDN_HEADS, GDN_DK, GDN_DV))
        xs3 = p[npt:, P_GDN:P_GDN + CONV_W].reshape(bs, ts, CONV_W)
        outs["gc_p"].append(tail_p[:, HIST - (GDN_CONV - 1):, 0:768])
        outs["gc_s"].append(xs3[:, ts - (GDN_CONV - 1):, 0:768])
        outs["sc_p"].append(tail_p[:, HIST - (SC_WIDTH - 1):, 768:])
        outs["sc_s"].append(xs3[:, ts - (SC_WIDTH - 1):, 768:])
        outs["cv_s"].append(p[npt:, P_CM + 256:P_CM + 512].reshape(bs, ts, BRANCH_W))

        h = _merge(h, br_p, br_s, row(norm_mix[i]), wg, wb, wo, layer=i)
        h = _ffn(h, pe_p, pe_s, row(norm_ffn[i]), wfg, wfu, wfd, row(norm_ple[i]), wpg, wp,
                 row(norm_final), layer=i, final=(i == depth - 1))

    y_prompt = h[:npt].reshape(bp, tp, D_MODEL)
    y_sample = h[npt:].reshape(bs, ts, D_MODEL)
    st = lambda k: jnp.stack(outs[k])
    return (y_prompt, y_sample, st("gla_p"), st("gla_s"), st("gdn_p"), st("gdn_s"),
            st("gc_p"), st("gc_s"), st("sc_p"), st("sc_s"), st("cv_s"))
```

```python
import functools

import jax
import jax.numpy as jnp
from jax import lax
from jax.experimental import pallas as pl
from jax.experimental.pallas import tpu as pltpu

F32 = jnp.float32
BF16 = jnp.bfloat16

D_MODEL = 1024
PLE_DIM = 256
BRANCH_W = 256
N_BRANCH = 4
GLA_HEADS = 4
GLA_DK = 32
GLA_DV = 64
GLA_RANK = 16
GLA_TAU = 16.0
GDN_HEADS = 4
GDN_DK = 64
GDN_DV = 64
GDN_CONV = 4
CM_GROUPS = 4
CM_CHUNK = 128
SC_WIDTH = 3
D_FF = 2816
EPS = 1e-6

ROWS = 64
PROMPT_SEQS = 4
PROMPT_CHUNKS = 2
PROMPT_SEQ_ROWS = ROWS * PROMPT_CHUNKS
SAMPLE_BLOCKS_PER_STEP = 2
SAMPLE_STEP_ROWS = ROWS * SAMPLE_BLOCKS_PER_STEP
GLA_QK = GLA_HEADS * GLA_DK
GDN_QK = GDN_HEADS * GDN_DK
LANE = 128
CONV_W = 3 * BRANCH_W + BRANCH_W
HIST = 8
SUM_PIECES = 2

W_IN_GLA, W_IN_GDN, W_IN_REST = (0, 768), (784, 1808), (1816, 3096)
W_IN_GA, W_IN_DA, W_IN_DB = 768, 1808, 1812
SMALL_COLS = LANE + 2 * 256
P_GLA = 0
P_GDN = 896
P_CM = 2944
P_COLS = 3456

VMEM_LIMIT = 56 * 1024 * 1024
TOKEN_TILE = 512


def _dot(a, b):
    return jnp.dot(a, b, preferred_element_type=F32)


def _dot_nt(a, b):
    return lax.dot_general(a, b, (((1,), (1,)), ((), ())), preferred_element_type=F32)


def _dot_tn(a, b):
    return lax.dot_general(a, b, (((0,), (0,)), ((), ())), preferred_element_type=F32)


def _split(x, n):
    parts, r = [], x
    for i in range(n):
        p = r.astype(BF16)
        parts.append(p)
        if i + 1 < n:
            r = r - p.astype(F32)
    return parts


def _dot1(a, b, dot=_dot):
    return dot(a.astype(BF16), b.astype(BF16))


def _dot3(a, b_pieces):
    ah, al = _split(a, 2)
    bh, bl = b_pieces
    return _dot(jnp.concatenate([ah, ah, al], axis=1), jnp.concatenate([bh, bl, bh], axis=0))


def _mask_dot(mask, x, n):
    return _dot(jnp.concatenate([mask.astype(BF16)] * n, axis=1), jnp.concatenate(_split(x, n), axis=0))


def _mask_dot_nt(mask, x, n):
    return _dot_nt(jnp.concatenate([mask.astype(BF16)] * n, axis=1), jnp.concatenate(_split(x, n), axis=1))


def _mask_dot_rhs(x, mask, n):
    return _dot(jnp.concatenate(_split(x, n), axis=1), jnp.concatenate([mask.astype(BF16)] * n, axis=0))


def _sigmoid(x):
    return 1.0 / (1.0 + jnp.exp(-x))


def _silu(x):
    return x * _sigmoid(x)


def _softplus(x):
    return jnp.maximum(x, 0.0) + jnp.log1p(jnp.exp(-jnp.abs(x)))


def _gelu_tanh(x):
    return 0.5 * x * (1.0 + jnp.tanh(0.7978845608028654 * (x + 0.044715 * (x * x * x))))


def _rms(x, w):
    return x * lax.rsqrt(jnp.mean(x * x, axis=-1, keepdims=True) + EPS) * w


def _idiv(x, n):
    assert n & (n - 1) == 0
    return lax.shift_right_logical(x, n.bit_length() - 1)


def _imod(x, n):
    assert n & (n - 1) == 0
    return lax.bitwise_and(x, n - 1)


def _const_spec(shape):
    return pl.BlockSpec(shape, lambda *_: (0,) * len(shape))


def _layer_spec(shape, layer, buffers=None):
    mode = {} if buffers is None else {"pipeline_mode": pl.Buffered(buffers)}
    return pl.BlockSpec((None,) + shape, lambda *_: (layer,) + (0,) * len(shape), **mode)


def _iota2(shape):
    return lax.broadcasted_iota(jnp.int32, shape, 0), lax.broadcasted_iota(jnp.int32, shape, 1)


def _stream_rows(hp_ref, hs_ref, prompt_tiles):
    return jnp.where(pl.program_id(0) < prompt_tiles, hp_ref[...], hs_ref[...])


def _inproj_kernel(hp_ref, hs_ref, nw_ref, wgla_ref, wgdn_ref, wrest_ref, wsmall_ref, wa2_ref, ba_ref, alog_ref,
                   dtb_ref, lng_ref, lnb_ref, p_ref, *, prompt_tiles):
    xn = _rms(_stream_rows(hp_ref, hs_ref, prompt_tiles), nw_ref[...]).astype(BF16)
    pg = _dot(xn, wgla_ref[...])
    p_ref[:, 0:128] = pg[:, 0:128] * (GLA_DK ** -0.5)
    p_ref[:, 128:512] = pg[:, 128:512]
    p_ref[:, 512:768] = _silu(pg[:, 512:768])
    ps = _dot(xn, wsmall_ref[...])
    za = _dot(ps[:, 0:LANE].astype(BF16), wa2_ref[...]) + ba_ref[...]
    p_ref[:, 768:896] = -_softplus(-za) * (1.0 / GLA_TAU)
    pd = _dot(xn, wgdn_ref[...])
    pr = _dot(xn, wrest_ref[...])
    g0 = P_GDN
    p_ref[:, g0:g0 + 768] = pd[:, 0:768]
    p_ref[:, g0 + 768:g0 + 1024] = pr[:, 1024:1280] * pr[:, 512:768]
    p_ref[:, g0 + 1024:g0 + 1280] = _silu(pd[:, 768:1024])
    p_ref[:, g0 + 1280:g0 + 1536] = pr[:, 768:1024]
    p_ref[:, g0 + 1536:g0 + 1792] = -jnp.exp(alog_ref[...]) * _softplus(ps[:, LANE:LANE + 256] + dtb_ref[...])
    p_ref[:, g0 + 1792:g0 + 2048] = _sigmoid(ps[:, LANE + 256:LANE + 512])
    p_ref[:, P_CM:P_CM + 256] = _gelu_tanh(pr[:, 0:256])
    gv = _gelu_tanh(pr[:, 256:512])
    mu = jnp.mean(gv, axis=-1, keepdims=True)
    d = gv - mu
    var = jnp.mean(d * d, axis=-1, keepdims=True)
    p_ref[:, P_CM + 256:P_CM + 512] = d * lax.rsqrt(var + EPS) * lng_ref[...] + lnb_ref[...]


def _inproj(hp, hs, nw, wgla, wgdn, wrest, wsmall, wa2, ba, alog, dtb, lng, lnb, *, layer):
    tm = TOKEN_TILE
    prompt_tiles = hp.shape[0] // tm
    ntok = hp.shape[0] + hs.shape[0]
    cols = lambda span: span[1] - span[0]
    return pl.pallas_call(
        functools.partial(_inproj_kernel, prompt_tiles=prompt_tiles),
        grid=(ntok // tm,),
        in_specs=[*_stream_specs(tm, D_MODEL, prompt_tiles), _const_spec((1, D_MODEL)),
                  _layer_spec((D_MODEL, cols(W_IN_GLA)), layer), _layer_spec((D_MODEL, cols(W_IN_GDN)), layer),
                  _layer_spec((D_MODEL, cols(W_IN_REST)), layer), _layer_spec((D_MODEL, SMALL_COLS), layer),
                  _const_spec((LANE, GLA_QK)), _const_spec((1, GLA_QK)), _const_spec((1, 256)),
                  _const_spec((1, 256)), _const_spec((1, 256)), _const_spec((1, 256))],
        out_specs=pl.BlockSpec((tm, P_COLS), lambda i: (i, 0)),
        out_shape=jax.ShapeDtypeStruct((ntok, P_COLS), F32),
        compiler_params=pltpu.CompilerParams(dimension_semantics=("parallel",), vmem_limit_bytes=VMEM_LIMIT),
        name="inproj",
    )(hp, hs, nw, wgla, wgdn, wrest, wsmall, wa2, ba, alog, dtb, lng, lnb)


def _stack_heads(x, group, nheads, period=None):
    w = x.shape[1]
    li = lax.broadcasted_iota(jnp.int32, (1, w), 1)
    if period is not None:
        li = _imod(li, period)
    hid = _idiv(li, group)
    zero = jnp.zeros_like(x)
    return jnp.concatenate([jnp.where(hid == h, x, zero) for h in range(nheads)], axis=0)


def _block_diag(x, nblocks):
    r = x.shape[0]
    ri, ci = _iota2((nblocks * r, nblocks * r))
    return jnp.where(_idiv(ri, r) == _idiv(ci, r), jnp.concatenate([x] * nblocks, axis=0), jnp.zeros((), x.dtype))


def _widen(x, nseq):
    if nseq == 1:
        return x
    seg = ROWS // nseq
    sid = _idiv(lax.broadcasted_iota(jnp.int32, (ROWS, 1), 0), seg)
    zero = jnp.zeros_like(x)
    return jnp.concatenate([jnp.where(sid == j, x, zero) for j in range(nseq)], axis=1)


def _group_mean(x, group):
    w = x.shape[1]
    ri, ci = _iota2((w, w))
    avg = jnp.where(_idiv(ri, group) == _idiv(ci, group), 1.0 / group, 0.0).astype(BF16)
    return _dot(x.astype(BF16), avg)


def _head_diag_mask(rows, cols, rhead, chead, rper):
    ri, ci = _iota2((rows, cols))
    return (_idiv(_imod(ri, rper), rhead) == _idiv(ci, chead)).astype(F32)


def _expand_state(s, width, reps):
    ri, ci = _iota2((width, reps * width))
    return _mask_dot_rhs(s, ri == _imod(ci, width), 3)


def _compact_state(st, width, reps):
    out = st[:, 0:width]
    for h in range(1, reps):
        out = out + st[:, h * width:(h + 1) * width]
    return out


class _Masks:
    def __init__(self, seg):
        ri, ci = _iota2((ROWS, ROWS))
        self.same = _idiv(ri, seg) == _idiv(ci, seg)
        self.tri = self.same & (ci <= ri)
        rl, cl = _iota2((ROWS, GDN_HEADS * ROWS))
        cl = _imod(cl, ROWS)
        same_l = _idiv(rl, seg) == _idiv(cl, seg)
        self.tri_l = same_l & (cl <= rl)
        self.strict_l = same_l & (cl < rl)
        self.eye_l = (rl == cl).astype(F32)
        self.ones = jnp.ones((ROWS, ROWS), BF16)
        self.seg = seg
        nseq = ROWS // seg
        self.bd_gla = _head_diag_mask(nseq * GLA_QK, GLA_HEADS * GLA_DV, GLA_DK, GLA_DV, GLA_QK)
        self.bd_gdn = _head_diag_mask(nseq * GDN_QK, GDN_QK, GDN_DK, GDN_DV, GDN_QK)


def _gla_prep(p, m, nseq):
    q, k, v, la = p[:, 0:128], p[:, 128:256], p[:, 256:512], p[:, 768:896]
    b = _mask_dot(m.tri, la, SUM_PIECES)
    if m.seg == ROWS:
        btot = jnp.broadcast_to(b[ROWS - 1:ROWS], b.shape)
    else:
        btot = _mask_dot(m.same, la, SUM_PIECES)
    qd = q * jnp.exp(b)
    kd = k * jnp.exp(-b)
    ke = k * jnp.exp(btot - b)
    a = jnp.where(m.tri_l, _dot1(qd, _stack_heads(kd, GLA_DK, GLA_HEADS), _dot_nt), 0.0)
    o_intra = _dot1(a, _stack_heads(v, GLA_DV, GLA_HEADS))
    dec = jnp.exp(_dot_tn(jnp.concatenate(_split(_widen(la, nseq), SUM_PIECES), axis=0),
                          jnp.ones((SUM_PIECES * ROWS, GLA_HEADS * GLA_DV), BF16)))
    ds = _dot1(_widen(ke, nseq), v, _dot_tn) * m.bd_gla
    return o_intra, _widen(qd, nseq), dec, ds


def _gla_scan(preps, sts):
    outs = [prep[0] + _dot1(prep[1], st) for prep, st in zip(preps, sts)]
    return outs, [st * prep[2] + prep[3] for prep, st in zip(preps, sts)]


def _gla_out(o, gn, rs):
    return o * lax.rsqrt(_group_mean(o * o, GLA_DV) + EPS) * gn * rs


def _gdn_qk_norm(qkv):
    hd = GDN_DK
    r = qkv.shape[0]
    cqk = jnp.concatenate([qkv[:, 0:256], qkv[:, 256:512]], axis=0)
    nrm = lax.rsqrt(_group_mean(cqk * cqk, hd) * hd + EPS)
    return qkv[:, 0:256] * nrm[0:r] * (hd ** -0.5), qkv[:, 256:512] * nrm[r:2 * r]


def _gdn_prep(q, k, cv, gs, betas, m, seg):
    nh, hd = GDN_HEADS, GDN_DK
    n = range(len(q))
    gc = [_mask_dot(m.tri, gs[i], SUM_PIECES) for i in n]
    if seg == ROWS:
        gtot = [jnp.broadcast_to(gc[i][ROWS - 1:ROWS], gc[i].shape) for i in n]
    else:
        gtot = [_mask_dot(m.same, gs[i], SUM_PIECES) for i in n]
    qkk = [_dot1(jnp.concatenate([q[i], k[i]], axis=0), _stack_heads(k[i], hd, nh), _dot_nt) for i in n]
    grow = [_mask_dot(m.ones, gc[i] * m.eye_l, SUM_PIECES) for i in n]
    decay = [jnp.where(m.tri_l, jnp.exp(jnp.where(m.tri_l, gc[i] - grow[i], 0.0)), 0.0) for i in n]
    amat = [jnp.where(m.strict_l, betas[i] * decay[i] * qkk[i][ROWS:2 * ROWS], 0.0) for i in n]
    inv = [m.eye_l - amat[i] for i in n]
    pw = amat
    pw_bd = [_block_diag(pw[i].astype(BF16), nh) for i in n]
    for _ in range(max(1, (seg - 1).bit_length()) - 1):
        pw = [_dot(pw[i].astype(BF16), pw_bd[i]) for i in n]
        pw_bd = [_block_diag(pw[i].astype(BF16), nh) for i in n]
        inv = [inv[i] + _dot(inv[i].astype(BF16), pw_bd[i]) for i in n]
    eg = [jnp.exp(gc[i]) for i in n]
    rhs = [jnp.concatenate([betas[i] * cv[i], betas[i] * eg[i] * k[i]], axis=1) for i in n]
    uw = [_dot(inv[i].astype(BF16), _stack_heads(rhs[i].astype(BF16), hd, nh, period=GDN_QK)) for i in n]
    return [(uw[i][:, 0:256], uw[i][:, 256:512], q[i] * eg[i], qkk[i][0:ROWS] * decay[i],
             k[i] * jnp.exp(gtot[i] - gc[i]), jnp.exp(gtot[i])) for i in n]


def _gdn_scan(preps, sts, m, nseq):
    seg = ROWS // nseq
    n = range(len(preps))
    ws = [_dot1(jnp.concatenate([_widen(preps[i][1], nseq), _widen(preps[i][2], nseq)], axis=0), sts[i]) for i in n]
    u = [preps[i][0] - ws[i][0:ROWS] for i in n]
    outs = [ws[i][ROWS:2 * ROWS] + _dot1(preps[i][3], _stack_heads(u[i], GDN_DV, GDN_HEADS)) for i in n]
    new = []
    for i in n:
        dn = preps[i][5]
        dn_tall = jnp.concatenate(
            [jnp.broadcast_to(dn[j * seg:j * seg + 1], (GDN_QK, GDN_QK)) for j in range(nseq)], axis=0)
        new.append(sts[i] * dn_tall + _dot1(_widen(preps[i][4], nseq), u[i], _dot_tn) * m.bd_gdn)
    return outs, new


def _gdn_out(o, gn, zs):
    return o * lax.rsqrt(_group_mean(o * o, GDN_DV) + EPS) * gn * zs


def _cm_block(p, ws, bias, seg):
    r = p.shape[0]
    gu, vn = p[:, 0:256], p[:, 256:512]
    ri, ci = _iota2((r, CM_GROUPS * r))
    ci = _imod(ci, r)
    wm = jnp.where((_idiv(ri, seg) == _idiv(ci, seg)) & (ci <= ri), ws, 0.0)
    return gu * (_dot1(wm, _stack_heads(vn, BRANCH_W // CM_GROUPS, CM_GROUPS)) + bias)


def _conv_taps(cw_ref):
    return [cw_ref[GDN_CONV - 1 - d:GDN_CONV - d, :] for d in range(GDN_CONV)]


def _conv_prompt(x, hist, taps):
    n = x.shape[0]
    t8 = lax.broadcasted_iota(jnp.int32, (HIST, 1), 0)
    acc = taps[0] * x
    for d in range(1, GDN_CONV):
        xr = pltpu.roll(x, d, 0)
        head = jnp.where(t8 < d, pltpu.roll(hist, d, 0), xr[0:HIST])
        acc = acc + taps[d] * jnp.concatenate([head, xr[HIST:n]], axis=0)
    return acc


def _conv_sample(x, hist, taps, seg):
    n = x.shape[0]
    tloc = _imod(lax.broadcasted_iota(jnp.int32, (n, 1), 0), seg)
    acc = taps[0] * x
    for d in range(1, GDN_CONV):
        prev = jnp.where(tloc < d, pltpu.roll(hist, (d - seg) % n, 0), pltpu.roll(x, d, 0))
        acc = acc + taps[d] * prev
    return acc


def _block_rows(c):
    return slice(c * ROWS, (c + 1) * ROWS)


def _mixer_prompt_kernel(*refs):
    ns, nc, sr = PROMPT_SEQS, PROMPT_CHUNKS, PROMPT_SEQ_ROWS
    p_refs = refs[:ns]
    (cw_ref, gn_gla_ref, gn_gdn_ref, cmw_ref, cmb_ref,
     o_ref, sgla_ref, sgdn_ref, tail_ref, st_gla, st_gdn, hist) = refs[ns:]

    @pl.when(pl.program_id(1) == 0)
    def _():
        st_gla[...] = jnp.zeros_like(st_gla)
        st_gdn[...] = jnp.zeros_like(st_gdn)
        hist[...] = jnp.zeros_like(hist)

    m = _Masks(ROWS)
    g0 = P_GDN
    taps = _conv_taps(cw_ref)
    xs, qkvs = [], []
    for s in range(ns):
        x = p_refs[s][:, g0:g0 + CONV_W]
        acc = _conv_prompt(x, hist[s], taps)
        hist[s] = x[sr - HIST:sr]
        o_ref[s, :, 768:1024] = p_refs[s][:, g0 + 1280:g0 + 1536] * acc[:, 768:1024]
        xs.append(x)
        qkvs.append(_silu(acc[:, 0:768]))
    qkv = jnp.concatenate(qkvs, axis=0)
    qn, kn = _gdn_qk_norm(qkv)
    blocks = [(s, c) for c in range(nc) for s in range(ns)]
    rows_of = lambda s, c: slice(s * sr + c * ROWS, s * sr + (c + 1) * ROWS)
    gdn = _gdn_prep([qn[rows_of(s, c)] for s, c in blocks], [kn[rows_of(s, c)] for s, c in blocks],
                    [qkv[rows_of(s, c), 512:768] for s, c in blocks],
                    [p_refs[s][_block_rows(c), g0 + 1536:g0 + 1792] for s, c in blocks],
                    [p_refs[s][_block_rows(c), g0 + 1792:g0 + 2048] for s, c in blocks], m, ROWS)
    gla = [_gla_prep(p_refs[s][_block_rows(c), P_GLA:P_GLA + 896], m, 1) for s, c in blocks]
    sg = [st_gla[s] for s in range(ns)]
    sd = [st_gdn[s] for s in range(ns)]
    og, od = {}, {}
    for c in range(nc):
        o, sg = _gla_scan(gla[c * ns:(c + 1) * ns], sg)
        og.update({(s, c): o[s] for s in range(ns)})
        o, sd = _gdn_scan(gdn[c * ns:(c + 1) * ns], sd, m, 1)
        od.update({(s, c): o[s] for s in range(ns)})
    by_rows = lambda d: jnp.concatenate([d[(s, c)] for s in range(ns) for c in range(nc)], axis=0)
    rs = jnp.concatenate([p_refs[s][:, 512:768] for s in range(ns)], axis=0)
    zs = jnp.concatenate([p_refs[s][:, g0 + 1024:g0 + 1280] for s in range(ns)], axis=0)
    o_gla = _gla_out(by_rows(og), gn_gla_ref[...], rs)
    o_gdn = _gdn_out(by_rows(od), gn_gdn_ref[...], zs)
    for s in range(ns):
        st_gla[s] = sg[s]
        st_gdn[s] = sd[s]
        o_ref[s, :, 0:256] = o_gla[s * sr:(s + 1) * sr]
        o_ref[s, :, 256:512] = o_gdn[s * sr:(s + 1) * sr]
        for c in range(sr // CM_CHUNK):
            rows = slice(c * CM_CHUNK, (c + 1) * CM_CHUNK)
            o_ref[s, rows, 512:768] = _cm_block(p_refs[s][rows, P_CM:P_CM + 512], cmw_ref[...], cmb_ref[...], CM_CHUNK)

    @pl.when(pl.program_id(1) == pl.num_programs(1) - 1)
    def _():
        for s in range(ns):
            sgla_ref[s] = _compact_state(sg[s], GLA_DV, GLA_HEADS)
            sgdn_ref[s] = _compact_state(sd[s], GDN_DV, GDN_HEADS)
            tail_ref[s] = xs[s][sr - HIST:sr]


def _mixer_prompt(p, cw, gn_gla, gn_gdn, cmw, cmb, *, nseqs, seq_len):
    ns, sr = PROMPT_SEQS, PROMPT_SEQ_ROWS
    nsteps = seq_len // sr
    seq_spec = lambda s: pl.BlockSpec((sr, P_COLS), lambda o, c: ((o * ns + s) * nsteps + c, 0))
    per_group = lambda *shape: pl.BlockSpec((ns,) + shape, lambda o, c: (o,) + (0,) * len(shape))
    return pl.pallas_call(
        _mixer_prompt_kernel,
        grid=(nseqs // ns, nsteps),
        in_specs=[seq_spec(s) for s in range(ns)] + [
            _const_spec((GDN_CONV, CONV_W)), _const_spec((1, 256)), _const_spec((1, 256)),
            _const_spec((CM_CHUNK, CM_GROUPS * CM_CHUNK)), _const_spec((CM_CHUNK, 256))],
        out_specs=[pl.BlockSpec((ns, sr, 4 * BRANCH_W), lambda o, c: (o, c, 0)),
                   per_group(GLA_QK, GLA_DV), per_group(GDN_QK, GDN_DV), per_group(HIST, CONV_W)],
        out_shape=[jax.ShapeDtypeStruct((nseqs, seq_len, 4 * BRANCH_W), F32),
                   jax.ShapeDtypeStruct((nseqs, GLA_QK, GLA_DV), F32),
                   jax.ShapeDtypeStruct((nseqs, GDN_QK, GDN_DV), F32),
                   jax.ShapeDtypeStruct((nseqs, HIST, CONV_W), F32)],
        scratch_shapes=[pltpu.VMEM((ns, GLA_QK, GLA_HEADS * GLA_DV), F32), pltpu.VMEM((ns, GDN_QK, GDN_QK), F32),
                        pltpu.VMEM((ns, HIST, CONV_W), F32)],
        compiler_params=pltpu.CompilerParams(dimension_semantics=("arbitrary", "arbitrary"),
                                             vmem_limit_bytes=VMEM_LIMIT),
        name="mixer_prompt",
    )(*([p] * ns), cw, gn_gla, gn_gdn, cmw, cmb)


def _mixer_sample_kernel(p_ref, hist_ref, s0gla_ref, s0gdn_ref, cw_ref, gn_gla_ref, gn_gdn_ref, cmw_ref, cmb_ref,
                         o_ref, sgla_ref, sgdn_ref, *, seg):
    nseq = ROWS // seg
    nb = range(SAMPLE_BLOCKS_PER_STEP)
    m = _Masks(seg)
    g0 = P_GDN
    x = p_ref[:, g0:g0 + CONV_W]
    acc = _conv_sample(x, hist_ref[...], _conv_taps(cw_ref), seg)
    o_ref[:, 768:1024] = p_ref[:, g0 + 1280:g0 + 1536] * acc[:, 768:1024]
    qkv = _silu(acc[:, 0:768])
    qn, kn = _gdn_qk_norm(qkv)
    gdn = _gdn_prep([qn[_block_rows(c)] for c in nb], [kn[_block_rows(c)] for c in nb],
                    [qkv[_block_rows(c), 512:768] for c in nb],
                    [p_ref[_block_rows(c), g0 + 1536:g0 + 1792] for c in nb],
                    [p_ref[_block_rows(c), g0 + 1792:g0 + 2048] for c in nb], m, seg)
    gla = [_gla_prep(p_ref[_block_rows(c), P_GLA:P_GLA + 896], m, nseq) for c in nb]
    seqs = lambda c: slice(c * nseq, (c + 1) * nseq)
    sg = [_expand_state(s0gla_ref[seqs(c)].reshape(nseq * GLA_QK, GLA_DV), GLA_DV, GLA_HEADS) * m.bd_gla for c in nb]
    sd = [_expand_state(s0gdn_ref[seqs(c)].reshape(nseq * GDN_QK, GDN_DV), GDN_DV, GDN_HEADS) * m.bd_gdn for c in nb]
    og, sg = _gla_scan(gla, sg)
    od, sd = _gdn_scan(gdn, sd, m, nseq)
    for c in nb:
        rows = _block_rows(c)
        sgla_ref[seqs(c)] = _compact_state(sg[c], GLA_DV, GLA_HEADS).reshape(nseq, GLA_QK, GLA_DV)
        sgdn_ref[seqs(c)] = _compact_state(sd[c], GDN_DV, GDN_HEADS).reshape(nseq, GDN_QK, GDN_DV)
        o_ref[rows, 512:768] = _cm_block(p_ref[rows, P_CM:P_CM + 512], cmw_ref[...], cmb_ref[...], seg)
    o_ref[:, 0:256] = _gla_out(jnp.concatenate(og, axis=0), gn_gla_ref[...], p_ref[:, 512:768])
    o_ref[:, 256:512] = _gdn_out(jnp.concatenate(od, axis=0), gn_gdn_ref[...], p_ref[:, g0 + 1024:g0 + 1280])


def _mixer_sample(p, hist, s0gla, s0gdn, cw, gn_gla, gn_gdn, cmw, cmb, *, layer, base_step, nsteps, seg):
    rows = SAMPLE_STEP_ROWS
    nseq_step = rows // seg
    return pl.pallas_call(
        functools.partial(_mixer_sample_kernel, seg=seg),
        grid=(nsteps,),
        in_specs=[pl.BlockSpec((rows, P_COLS), lambda i: (base_step + i, 0)),
                  pl.BlockSpec((rows, CONV_W), lambda i: (i, 0)),
                  pl.BlockSpec((None, nseq_step, GLA_QK, GLA_DV), lambda i: (layer, i, 0, 0)),
                  pl.BlockSpec((None, nseq_step, GDN_QK, GDN_DV), lambda i: (layer, i, 0, 0)),
                  _const_spec((GDN_CONV, CONV_W)), _const_spec((1, 256)), _const_spec((1, 256)),
                  _const_spec((ROWS, CM_GROUPS * ROWS)), _const_spec((ROWS, 256))],
        out_specs=[pl.BlockSpec((rows, 4 * BRANCH_W), lambda i: (i, 0)),
                   pl.BlockSpec((nseq_step, GLA_QK, GLA_DV), lambda i: (i, 0, 0)),
                   pl.BlockSpec((nseq_step, GDN_QK, GDN_DV), lambda i: (i, 0, 0))],
        out_shape=[jax.ShapeDtypeStruct((nsteps * rows, 4 * BRANCH_W), F32),
                   jax.ShapeDtypeStruct((nsteps * nseq_step, GLA_QK, GLA_DV), F32),
                   jax.ShapeDtypeStruct((nsteps * nseq_step, GDN_QK, GDN_DV), F32)],
        compiler_params=pltpu.CompilerParams(dimension_semantics=("parallel",), vmem_limit_bytes=VMEM_LIMIT),
        name="mixer_sample",
    )(p, hist, s0gla, s0gdn, cw, gn_gla, gn_gdn, cmw, cmb)


def _merge_kernel(hp_ref, hs_ref, bp_ref, bs_ref, nw_ref, wg_ref, wb_ref, wo_ref, o_ref, *, prompt_tiles):
    h = _stream_rows(hp_ref, hs_ref, prompt_tiles)
    xn = _rms(h, nw_ref[...]).astype(BF16)
    br = _stream_rows(bp_ref, bs_ref, prompt_tiles).astype(BF16)
    merged = None
    for gi in range(N_BRANCH):
        gate = _sigmoid(_dot(xn, wg_ref[:, gi * D_MODEL:(gi + 1) * D_MODEL]))
        term = _dot(br[:, gi * BRANCH_W:(gi + 1) * BRANCH_W], wb_ref[gi]) * gate
        merged = term if merged is None else merged + term
    o_ref[...] = h + _dot(merged.astype(BF16), wo_ref[...])


def _stream_specs(tm, width, prompt_tiles, lead=()):
    nlead = (None,) * len(lead)
    return (pl.BlockSpec(nlead + (tm, width), lambda i: lead + (jnp.minimum(i, prompt_tiles - 1), 0)),
            pl.BlockSpec(nlead + (tm, width), lambda i: lead + (jnp.maximum(i - prompt_tiles, 0), 0)))


def _merge(hp, hs, br_p, br_s, nw, wg, wb, wo, *, layer):
    ntok = hp.shape[0] + hs.shape[0]
    tm = TOKEN_TILE
    prompt_tiles = hp.shape[0] // tm
    row = lambda n: pl.BlockSpec((tm, n), lambda i: (i, 0))
    return pl.pallas_call(
        functools.partial(_merge_kernel, prompt_tiles=prompt_tiles),
        grid=(ntok // tm,),
        in_specs=[*_stream_specs(tm, D_MODEL, prompt_tiles), *_stream_specs(tm, 4 * BRANCH_W, prompt_tiles),
                  _const_spec((1, D_MODEL)),
                  _layer_spec((D_MODEL, N_BRANCH * D_MODEL), layer),
                  _layer_spec((N_BRANCH, BRANCH_W, D_MODEL), layer), _layer_spec((D_MODEL, D_MODEL), layer)],
        out_specs=row(D_MODEL),
        out_shape=jax.ShapeDtypeStruct((ntok, D_MODEL), F32),
        compiler_params=pltpu.CompilerParams(dimension_semantics=("parallel",), vmem_limit_bytes=VMEM_LIMIT),
        name="merge",
    )(hp, hs, br_p, br_s, nw, wg, wb, wo)


def _ffn_kernel(h_ref, pp_ref, ps_ref, nf_ref, wfg_ref, wfu_ref, wfd_ref, np_ref, wpg_ref, wp_ref, nfin_ref,
                op_ref, os_ref, *, final, prompt_tiles):
    h = h_ref[...]
    xf = _rms(h, nf_ref[...]).astype(BF16)
    act = _silu(_dot(xf, wfg_ref[...])) * _dot(xf, wfu_ref[...])
    h = h + _dot(act.astype(BF16), wfd_ref[...])
    pg = _sigmoid(_dot(_rms(h, np_ref[...]).astype(BF16), wpg_ref[...]))
    pe = _stream_rows(pp_ref, ps_ref, prompt_tiles).astype(BF16)
    h = h + pg * _dot(pe, wp_ref[...])
    out = _rms(h, nfin_ref[...]) if final else h

    @pl.when(pl.program_id(0) < prompt_tiles)
    def _():
        op_ref[...] = out

    @pl.when(pl.program_id(0) >= prompt_tiles)
    def _():
        os_ref[...] = out


def _ffn(h, pe_p, pe_s, nf, wfg, wfu, wfd, npl, wpg, wp, nfin, *, layer, final):
    ntok = h.shape[0]
    tm = TOKEN_TILE
    npt = pe_p.shape[1]
    prompt_tiles = npt // tm
    row = lambda n: pl.BlockSpec((tm, n), lambda i: (i, 0))
    once = lambda shape: _layer_spec(shape, layer, buffers=1)
    return pl.pallas_call(
        functools.partial(_ffn_kernel, final=final, prompt_tiles=prompt_tiles),
        grid=(ntok // tm,),
        in_specs=[row(D_MODEL), *_stream_specs(tm, PLE_DIM, prompt_tiles, lead=(layer,)), _const_spec((1, D_MODEL)),
                  once((D_MODEL, D_FF)), once((D_MODEL, D_FF)), once((D_FF, D_MODEL)),
                  _const_spec((1, D_MODEL)), once((D_MODEL, D_MODEL)), once((PLE_DIM, D_MODEL)),
                  _const_spec((1, D_MODEL))],
        out_specs=list(_stream_specs(tm, D_MODEL, prompt_tiles)),
        out_shape=[jax.ShapeDtypeStruct((npt, D_MODEL), F32), jax.ShapeDtypeStruct((ntok - npt, D_MODEL), F32)],
        compiler_params=pltpu.CompilerParams(dimension_semantics=("arbitrary",), vmem_limit_bytes=VMEM_LIMIT),
        name="ffn_final" if final else "ffn",
    )(h, pe_p, pe_s, nf, wfg, wfu, wfd, npl, wpg, wp, nfin)


def _split_w_in(w_in):
    span = lambda s: w_in[..., s[0]:s[1]].astype(BF16)
    rep = lambda o: jnp.repeat(w_in[..., o:o + GDN_HEADS], GDN_DK, axis=-1)
    small = jnp.concatenate([w_in[..., W_IN_GA:W_IN_GA + GLA_RANK],
                             jnp.zeros(w_in.shape[:-1] + (LANE - GLA_RANK,), w_in.dtype),
                             rep(W_IN_DA), rep(W_IN_DB)], axis=-1).astype(BF16)
    return span(W_IN_GLA), span(W_IN_GDN), span(W_IN_REST), small


def kernel(x_prompt, x_sample, state_gla, state_gdn, state_gdn_conv, state_sconv, p_prompt, p_sample, norm_mix, w_in, gla_wa2, gla_ba, gla_norm, gdn_conv_w, gdn_a_log, gdn_dt_bias, gdn_norm, cm_ln_g, cm_ln_b, cm_ws, cm_bs, sc_conv_w, w_gate, w_branch, w_o, norm_ffn, w_ffn_gate, w_ffn_up, w_ffn_down, norm_ple, w_ple_gate, w_ple, norm_final):
    depth = w_in.shape[0]
    bp, tp, _ = x_prompt.shape
    bs, ts, _ = x_sample.shape
    npt, nst = bp * tp, bs * ts
    sseq = ROWS // ts
    assert ts == HIST and nst % SAMPLE_STEP_ROWS == 0
    assert bp % PROMPT_SEQS == 0 and tp % PROMPT_SEQ_ROWS == 0 and PROMPT_SEQ_ROWS % CM_CHUNK == 0
    assert npt % TOKEN_TILE == 0 and nst % TOKEN_TILE == 0

    hp, hs = x_prompt.reshape(npt, D_MODEL), x_sample.reshape(nst, D_MODEL)
    pe_p = p_prompt.reshape(depth, npt, PLE_DIM)
    pe_s = p_sample.reshape(depth, nst, PLE_DIM)
    s0_gla = state_gla.reshape(depth, bs, GLA_QK, GLA_DV)
    s0_gdn = state_gdn.reshape(depth, bs, GDN_QK, GDN_DV)
    row = lambda a: a.reshape(1, -1)
    w_in_groups = _split_w_in(w_in)
    wg, wb, wo = w_gate.astype(BF16), w_branch.astype(BF16), w_o.astype(BF16)
    wfg, wfu, wfd = w_ffn_gate.astype(BF16), w_ffn_up.astype(BF16), w_ffn_down.astype(BF16)
    wpg, wp = w_ple_gate.astype(BF16), w_ple.astype(BF16)
    outs = {k: [] for k in ("gla_p", "gla_s", "gdn_p", "gdn_s", "gc_p", "gc_s", "sc_p", "sc_s", "cv_s")}
    for i in range(depth):
        wa2 = jnp.concatenate([gla_wa2[i], jnp.zeros((LANE - GLA_RANK, GLA_QK), F32)], axis=0).astype(BF16)
        p = _inproj(hp, hs, row(norm_mix[i]), *w_in_groups, wa2, row(gla_ba[i]),
                    row(jnp.repeat(gdn_a_log[i], GDN_DK)), row(jnp.repeat(gdn_dt_bias[i], GDN_DK)),
                    row(cm_ln_g[i]), row(cm_ln_b[i]), layer=i)

        cw = jnp.concatenate([gdn_conv_w[i], jnp.concatenate([jnp.zeros((1, BRANCH_W), F32), sc_conv_w[i]], axis=0)],
                             axis=1)
        hist_s = jnp.concatenate([
            jnp.pad(state_gdn_conv[i], ((0, 0), (HIST - (GDN_CONV - 1), 0), (0, 0))),
            jnp.pad(state_sconv[i], ((0, 0), (HIST - (SC_WIDTH - 1), 0), (0, 0)))], axis=2).reshape(nst, CONV_W)
        gn_gla = row(jnp.tile(gla_norm[i], GLA_HEADS))
        gn_gdn = row(jnp.tile(gdn_norm[i], GDN_HEADS))
        cmw_p = jnp.transpose(cm_ws[i], (1, 0, 2)).reshape(CM_CHUNK, CM_GROUPS * CM_CHUNK)
        cmb_p = jnp.repeat(cm_bs[i].T, BRANCH_W // CM_GROUPS, axis=1)
        cmw_s = jnp.transpose(jnp.tile(cm_ws[i][:, :ts, :ts], (1, sseq, sseq)), (1, 0, 2)).reshape(ROWS, CM_GROUPS * ROWS)
        cmb_s = jnp.tile(cmb_p[:ts], (sseq, 1))

        br_p, gla_p, gdn_p, tail_p = _mixer_prompt(p, cw, gn_gla, gn_gdn, cmw_p, cmb_p, nseqs=bp, seq_len=tp)
        br_p = br_p.reshape(npt, N_BRANCH * BRANCH_W)
        br_s, gla_s, gdn_s = _mixer_sample(p, hist_s, s0_gla, s0_gdn, cw, gn_gla, gn_gdn, cmw_s, cmb_s, layer=i,
                                           base_step=npt // SAMPLE_STEP_ROWS, nsteps=nst // SAMPLE_STEP_ROWS, seg=ts)
        outs["gla_p"].append(gla_p.reshape(bp, GLA_HEADS, GLA_DK, GLA_DV))
        outs["gla_s"].append(gla_s.reshape(bs, GLA_HEADS, GLA_DK, GLA_DV))
        outs["gdn_p"].append(gdn_p.reshape(bp, GDN_HEADS, GDN_DK, GDN_DV))
        outs["gdn_s"].append(gdn_s.reshape(bs, GDN_HEADS, GDN_DK, GDN_DV))
        xs3 = p[npt:, P_GDN:P_GDN + CONV_W].reshape(bs, ts, CONV_W)
        outs["gc_p"].append(tail_p[:, HIST - (GDN_CONV - 1):, 0:768])
        outs["gc_s"].append(xs3[:, ts - (GDN_CONV - 1):, 0:768])
        outs["sc_p"].append(tail_p[:, HIST - (SC_WIDTH - 1):, 768:])
        outs["sc_s"].append(xs3[:, ts - (SC_WIDTH - 1):, 768:])
        outs["cv_s"].append(p[npt:, P_CM + 256:P_CM + 512].reshape(bs, ts, BRANCH_W))

        h1 = _merge(hp, hs, br_p, br_s, row(norm_mix[i]), wg, wb, wo, layer=i)
        hp, hs = _ffn(h1, pe_p, pe_s, row(norm_ffn[i]), wfg, wfu, wfd, row(norm_ple[i]), wpg, wp,
                      row(norm_final), layer=i, final=(i == depth - 1))

    y_prompt = hp.reshape(bp, tp, D_MODEL)
    y_sample = hs.reshape(bs, ts, D_MODEL)
    st = lambda k: jnp.stack(outs[k])
    return (y_prompt, y_sample, st("gla_p"), st("gla_s"), st("gdn_p"), st("gdn_s"),
            st("gc_p"), st("gc_s"), st("sc_p"), st("sc_s"), st("cv_s"))
```

```python
import functools

import jax
import jax.numpy as jnp
from jax import lax
from jax.experimental import pallas as pl
from jax.experimental.pallas import tpu as pltpu

F32 = jnp.float32
BF16 = jnp.bfloat16

D_MODEL = 1024
PLE_DIM = 256
BRANCH_W = 256
N_BRANCH = 4
GLA_HEADS = 4
GLA_DK = 32
GLA_DV = 64
GLA_RANK = 16
GLA_TAU = 16.0
GDN_HEADS = 4
GDN_DK = 64
GDN_DV = 64
GDN_CONV = 4
CM_GROUPS = 4
CM_CHUNK = 128
SC_WIDTH = 3
D_FF = 2816
EPS = 1e-6

ROWS = 64
PROMPT_SEQS = 4
PROMPT_CHUNKS = 2
PROMPT_SEQ_ROWS = ROWS * PROMPT_CHUNKS
SAMPLE_BLOCKS_PER_STEP = 2
SAMPLE_STEP_ROWS = ROWS * SAMPLE_BLOCKS_PER_STEP
GLA_QK = GLA_HEADS * GLA_DK
GDN_QK = GDN_HEADS * GDN_DK
LANE = 128
CONV_W = 3 * BRANCH_W + BRANCH_W
HIST = 8
SUM_PIECES = 2

W_IN_GLA, W_IN_GDN, W_IN_REST = (0, 768), (784, 1808), (1816, 3096)
W_IN_GA, W_IN_DA, W_IN_DB = 768, 1808, 1812
SMALL_COLS = LANE + 2 * 256
P_GLA = 0
P_GDN = 896
P_CM = 2944
P_COLS = 3456

VMEM_LIMIT = 56 * 1024 * 1024
TOKEN_TILE = 512


def _dot(a, b):
    return jnp.dot(a, b, preferred_element_type=F32)


def _dot_nt(a, b):
    return lax.dot_general(a, b, (((1,), (1,)), ((), ())), preferred_element_type=F32)


def _dot_tn(a, b):
    return lax.dot_general(a, b, (((0,), (0,)), ((), ())), preferred_element_type=F32)


def _split(x, n):
    parts, r = [], x
    for i in range(n):
        p = r.astype(BF16)
        parts.append(p)
        if i + 1 < n:
            r = r - p.astype(F32)
    return parts


def _dot1(a, b, dot=_dot):
    return dot(a.astype(BF16), b.astype(BF16))


def _dot3(a, b_pieces):
    ah, al = _split(a, 2)
    bh, bl = b_pieces
    return _dot(jnp.concatenate([ah, ah, al], axis=1), jnp.concatenate([bh, bl, bh], axis=0))


def _mask_dot(mask, x, n):
    return _dot(jnp.concatenate([mask.astype(BF16)] * n, axis=1), jnp.concatenate(_split(x, n), axis=0))


def _mask_dot_nt(mask, x, n):
    return _dot_nt(jnp.concatenate([mask.astype(BF16)] * n, axis=1), jnp.concatenate(_split(x, n), axis=1))


def _mask_dot_rhs(x, mask, n):
    return _dot(jnp.concatenate(_split(x, n), axis=1), jnp.concatenate([mask.astype(BF16)] * n, axis=0))


def _sigmoid(x):
    return 1.0 / (1.0 + jnp.exp(-x))


def _silu(x):
    return x * _sigmoid(x)


def _softplus(x):
    return jnp.maximum(x, 0.0) + jnp.log1p(jnp.exp(-jnp.abs(x)))


def _gelu_tanh(x):
    return 0.5 * x * (1.0 + jnp.tanh(0.7978845608028654 * (x + 0.044715 * (x * x * x))))


def _rms(x, w):
    return x * lax.rsqrt(jnp.mean(x * x, axis=-1, keepdims=True) + EPS) * w


def _idiv(x, n):
    assert n & (n - 1) == 0
    return lax.shift_right_logical(x, n.bit_length() - 1)


def _imod(x, n):
    assert n & (n - 1) == 0
    return lax.bitwise_and(x, n - 1)


def _const_spec(shape):
    return pl.BlockSpec(shape, lambda *_: (0,) * len(shape))


def _layer_spec(shape, layer, buffers=None):
    mode = {} if buffers is None else {"pipeline_mode": pl.Buffered(buffers)}
    return pl.BlockSpec((None,) + shape, lambda *_: (layer,) + (0,) * len(shape), **mode)


def _iota2(shape):
    return lax.broadcasted_iota(jnp.int32, shape, 0), lax.broadcasted_iota(jnp.int32, shape, 1)


def _stream_rows(hp_ref, hs_ref, prompt_tiles):
    return jnp.where(pl.program_id(0) < prompt_tiles, hp_ref[...], hs_ref[...])


def _residual_rows(h_refs, prompt_tiles):
    return h_refs[0][...] if len(h_refs) == 1 else _stream_rows(*h_refs, prompt_tiles)


def _residual_specs(h, tm):
    if isinstance(h, tuple):
        prompt_tiles = h[0].shape[0] // tm
        return list(h), list(_stream_specs(tm, D_MODEL, prompt_tiles)), prompt_tiles, h[0].shape[0] + h[1].shape[0]
    return [h], [pl.BlockSpec((tm, D_MODEL), lambda i: (i, 0))], None, h.shape[0]


def _inproj_kernel(*refs, prompt_tiles):
    nh = 1 if prompt_tiles is None else 2
    (nw_ref, wgla_ref, wgdn_ref, wrest_ref, wsmall_ref, wa2_ref, ba_ref, alog_ref,
     dtb_ref, lng_ref, lnb_ref, p_ref) = refs[nh:]
    xn = _rms(_residual_rows(refs[:nh], prompt_tiles), nw_ref[...]).astype(BF16)
    pg = _dot(xn, wgla_ref[...])
    p_ref[:, 0:128] = pg[:, 0:128] * (GLA_DK ** -0.5)
    p_ref[:, 128:512] = pg[:, 128:512]
    p_ref[:, 512:768] = _silu(pg[:, 512:768])
    ps = _dot(xn, wsmall_ref[...])
    za = _dot(ps[:, 0:LANE].astype(BF16), wa2_ref[...]) + ba_ref[...]
    p_ref[:, 768:896] = -_softplus(-za) * (1.0 / GLA_TAU)
    pd = _dot(xn, wgdn_ref[...])
    pr = _dot(xn, wrest_ref[...])
    g0 = P_GDN
    p_ref[:, g0:g0 + 768] = pd[:, 0:768]
    p_ref[:, g0 + 768:g0 + 1024] = pr[:, 1024:1280] * pr[:, 512:768]
    p_ref[:, g0 + 1024:g0 + 1280] = _silu(pd[:, 768:1024])
    p_ref[:, g0 + 1280:g0 + 1536] = pr[:, 768:1024]
    p_ref[:, g0 + 1536:g0 + 1792] = -jnp.exp(alog_ref[...]) * _softplus(ps[:, LANE:LANE + 256] + dtb_ref[...])
    p_ref[:, g0 + 1792:g0 + 2048] = _sigmoid(ps[:, LANE + 256:LANE + 512])
    p_ref[:, P_CM:P_CM + 256] = _gelu_tanh(pr[:, 0:256])
    gv = _gelu_tanh(pr[:, 256:512])
    mu = jnp.mean(gv, axis=-1, keepdims=True)
    d = gv - mu
    var = jnp.mean(d * d, axis=-1, keepdims=True)
    p_ref[:, P_CM + 256:P_CM + 512] = d * lax.rsqrt(var + EPS) * lng_ref[...] + lnb_ref[...]


def _inproj(h, nw, wgla, wgdn, wrest, wsmall, wa2, ba, alog, dtb, lng, lnb, *, layer):
    tm = TOKEN_TILE
    h_arrays, h_specs, prompt_tiles, ntok = _residual_specs(h, tm)
    cols = lambda span: span[1] - span[0]
    per_layer = lambda *shape: _layer_spec(shape, layer)
    return pl.pallas_call(
        functools.partial(_inproj_kernel, prompt_tiles=prompt_tiles),
        grid=(ntok // tm,),
        in_specs=[*h_specs, per_layer(1, D_MODEL),
                  per_layer(D_MODEL, cols(W_IN_GLA)), per_layer(D_MODEL, cols(W_IN_GDN)),
                  per_layer(D_MODEL, cols(W_IN_REST)), per_layer(D_MODEL, SMALL_COLS),
                  per_layer(LANE, GLA_QK), per_layer(1, GLA_QK), per_layer(1, 256),
                  per_layer(1, 256), per_layer(1, 256), per_layer(1, 256)],
        out_specs=pl.BlockSpec((tm, P_COLS), lambda i: (i, 0)),
        out_shape=jax.ShapeDtypeStruct((ntok, P_COLS), F32),
        compiler_params=pltpu.CompilerParams(dimension_semantics=("parallel",), vmem_limit_bytes=VMEM_LIMIT),
        name="inproj",
    )(*h_arrays, nw, wgla, wgdn, wrest, wsmall, wa2, ba, alog, dtb, lng, lnb)


def _stack_heads(x, group, nheads, period=None):
    w = x.shape[1]
    li = lax.broadcasted_iota(jnp.int32, (1, w), 1)
    if period is not None:
        li = _imod(li, period)
    hid = _idiv(li, group)
    zero = jnp.zeros_like(x)
    return jnp.concatenate([jnp.where(hid == h, x, zero) for h in range(nheads)], axis=0)


def _block_diag(x, nblocks):
    r = x.shape[0]
    ri, ci = _iota2((nblocks * r, nblocks * r))
    return jnp.where(_idiv(ri, r) == _idiv(ci, r), jnp.concatenate([x] * nblocks, axis=0), jnp.zeros((), x.dtype))


def _widen(x, nseq):
    if nseq == 1:
        return x
    seg = ROWS // nseq
    sid = _idiv(lax.broadcasted_iota(jnp.int32, (ROWS, 1), 0), seg)
    zero = jnp.zeros_like(x)
    return jnp.concatenate([jnp.where(sid == j, x, zero) for j in range(nseq)], axis=1)


def _group_mean(x, group):
    w = x.shape[1]
    ri, ci = _iota2((w, w))
    avg = jnp.where(_idiv(ri, group) == _idiv(ci, group), 1.0 / group, 0.0).astype(BF16)
    return _dot(x.astype(BF16), avg)


def _head_diag_mask(rows, cols, rhead, chead, rper):
    ri, ci = _iota2((rows, cols))
    return (_idiv(_imod(ri, rper), rhead) == _idiv(ci, chead)).astype(F32)


def _expand_state(s, width, reps):
    ri, ci = _iota2((width, reps * width))
    return _mask_dot_rhs(s, ri == _imod(ci, width), 3)


def _compact_state(st, width, reps):
    out = st[:, 0:width]
    for h in range(1, reps):
        out = out + st[:, h * width:(h + 1) * width]
    return out


class _Masks:
    def __init__(self, seg):
        ri, ci = _iota2((ROWS, ROWS))
        self.same = _idiv(ri, seg) == _idiv(ci, seg)
        self.tri = self.same & (ci <= ri)
        rl, cl = _iota2((ROWS, GDN_HEADS * ROWS))
        cl = _imod(cl, ROWS)
        same_l = _idiv(rl, seg) == _idiv(cl, seg)
        self.tri_l = same_l & (cl <= rl)
        self.strict_l = same_l & (cl < rl)
        self.eye_l = (rl == cl).astype(F32)
        self.ones = jnp.ones((ROWS, ROWS), BF16)
        self.seg = seg
        nseq = ROWS // seg
        self.bd_gla = _head_diag_mask(nseq * GLA_QK, GLA_HEADS * GLA_DV, GLA_DK, GLA_DV, GLA_QK)
        self.bd_gdn = _head_diag_mask(nseq * GDN_QK, GDN_QK, GDN_DK, GDN_DV, GDN_QK)


def _gla_prep(p, m, nseq):
    q, k, v, la = p[:, 0:128], p[:, 128:256], p[:, 256:512], p[:, 768:896]
    b = _mask_dot(m.tri, la, SUM_PIECES)
    if m.seg == ROWS:
        btot = jnp.broadcast_to(b[ROWS - 1:ROWS], b.shape)
    else:
        btot = _mask_dot(m.same, la, SUM_PIECES)
    qd = q * jnp.exp(b)
    kd = k * jnp.exp(-b)
    ke = k * jnp.exp(btot - b)
    a = jnp.where(m.tri_l, _dot1(qd, _stack_heads(kd, GLA_DK, GLA_HEADS), _dot_nt), 0.0)
    o_intra = _dot1(a, _stack_heads(v, GLA_DV, GLA_HEADS))
    dec = jnp.exp(_dot_tn(jnp.concatenate(_split(_widen(la, nseq), SUM_PIECES), axis=0),
                          jnp.ones((SUM_PIECES * ROWS, GLA_HEADS * GLA_DV), BF16)))
    ds = _dot1(_widen(ke, nseq), v, _dot_tn) * m.bd_gla
    return o_intra, _widen(qd, nseq), dec, ds


def _gla_scan(preps, sts):
    outs = [prep[0] + _dot1(prep[1], st) for prep, st in zip(preps, sts)]
    return outs, [st * prep[2] + prep[3] for prep, st in zip(preps, sts)]


def _gla_out(o, gn, rs):
    return o * lax.rsqrt(_group_mean(o * o, GLA_DV) + EPS) * gn * rs


def _gdn_qk_norm(qkv):
    hd = GDN_DK
    r = qkv.shape[0]
    cqk = jnp.concatenate([qkv[:, 0:256], qkv[:, 256:512]], axis=0)
    nrm = lax.rsqrt(_group_mean(cqk * cqk, hd) * hd + EPS)
    return qkv[:, 0:256] * nrm[0:r] * (hd ** -0.5), qkv[:, 256:512] * nrm[r:2 * r]


def _gdn_prep(q, k, cv, gs, betas, m, seg):
    nh, hd = GDN_HEADS, GDN_DK
    n = range(len(q))
    gc = [_mask_dot(m.tri, gs[i], SUM_PIECES) for i in n]
    if seg == ROWS:
        gtot = [jnp.broadcast_to(gc[i][ROWS - 1:ROWS], gc[i].shape) for i in n]
    else:
        gtot = [_mask_dot(m.same, gs[i], SUM_PIECES) for i in n]
    qkk = [_dot1(jnp.concatenate([q[i], k[i]], axis=0), _stack_heads(k[i], hd, nh), _dot_nt) for i in n]
    grow = [_mask_dot(m.ones, gc[i] * m.eye_l, SUM_PIECES) for i in n]
    decay = [jnp.where(m.tri_l, jnp.exp(jnp.where(m.tri_l, gc[i] - grow[i], 0.0)), 0.0) for i in n]
    amat = [jnp.where(m.strict_l, betas[i] * decay[i] * qkk[i][ROWS:2 * ROWS], 0.0) for i in n]
    inv = [m.eye_l - amat[i] for i in n]
    pw = amat
    pw_bd = [_block_diag(pw[i].astype(BF16), nh) for i in n]
    for _ in range(max(1, (seg - 1).bit_length()) - 1):
        pw = [_dot(pw[i].astype(BF16), pw_bd[i]) for i in n]
        pw_bd = [_block_diag(pw[i].astype(BF16), nh) for i in n]
        inv = [inv[i] + _dot(inv[i].astype(BF16), pw_bd[i]) for i in n]
    eg = [jnp.exp(gc[i]) for i in n]
    rhs = [jnp.concatenate([betas[i] * cv[i], betas[i] * eg[i] * k[i]], axis=1) for i in n]
    uw = [_dot(inv[i].astype(BF16), _stack_heads(rhs[i].astype(BF16), hd, nh, period=GDN_QK)) for i in n]
    return [(uw[i][:, 0:256], uw[i][:, 256:512], q[i] * eg[i], qkk[i][0:ROWS] * decay[i],
             k[i] * jnp.exp(gtot[i] - gc[i]), jnp.exp(gtot[i])) for i in n]


def _gdn_scan(preps, sts, m, nseq):
    seg = ROWS // nseq
    n = range(len(preps))
    ws = [_dot1(jnp.concatenate([_widen(preps[i][1], nseq), _widen(preps[i][2], nseq)], axis=0), sts[i]) for i in n]
    u = [preps[i][0] - ws[i][0:ROWS] for i in n]
    outs = [ws[i][ROWS:2 * ROWS] + _dot1(preps[i][3], _stack_heads(u[i], GDN_DV, GDN_HEADS)) for i in n]
    new = []
    for i in n:
        dn = preps[i][5]
        dn_tall = jnp.concatenate(
            [jnp.broadcast_to(dn[j * seg:j * seg + 1], (GDN_QK, GDN_QK)) for j in range(nseq)], axis=0)
        new.append(sts[i] * dn_tall + _dot1(_widen(preps[i][4], nseq), u[i], _dot_tn) * m.bd_gdn)
    return outs, new


def _gdn_out(o, gn, zs):
    return o * lax.rsqrt(_group_mean(o * o, GDN_DV) + EPS) * gn * zs


def _cm_block(p, ws, bias, seg):
    r = p.shape[0]
    gu, vn = p[:, 0:256], p[:, 256:512]
    ri, ci = _iota2((r, CM_GROUPS * r))
    ci = _imod(ci, r)
    wm = jnp.where((_idiv(ri, seg) == _idiv(ci, seg)) & (ci <= ri), ws, 0.0)
    return gu * (_dot1(wm, _stack_heads(vn, BRANCH_W // CM_GROUPS, CM_GROUPS)) + bias)


def _conv_taps(cw_ref):
    return [cw_ref[GDN_CONV - 1 - d:GDN_CONV - d, :] for d in range(GDN_CONV)]


def _conv_prompt(x, hist, taps):
    n = x.shape[0]
    t8 = lax.broadcasted_iota(jnp.int32, (HIST, 1), 0)
    acc = taps[0] * x
    for d in range(1, GDN_CONV):
        xr = pltpu.roll(x, d, 0)
        head = jnp.where(t8 < d, pltpu.roll(hist, d, 0), xr[0:HIST])
        acc = acc + taps[d] * jnp.concatenate([head, xr[HIST:n]], axis=0)
    return acc


def _conv_sample(x, hist, taps, seg):
    n = x.shape[0]
    tloc = _imod(lax.broadcasted_iota(jnp.int32, (n, 1), 0), seg)
    acc = taps[0] * x
    for d in range(1, GDN_CONV):
        prev = jnp.where(tloc < d, pltpu.roll(hist, (d - seg) % n, 0), pltpu.roll(x, d, 0))
        acc = acc + taps[d] * prev
    return acc


def _block_rows(c):
    return slice(c * ROWS, (c + 1) * ROWS)


def _mixer_prompt_kernel(*refs):
    ns, nc, sr = PROMPT_SEQS, PROMPT_CHUNKS, PROMPT_SEQ_ROWS
    p_refs = refs[:ns]
    (cw_ref, gn_gla_ref, gn_gdn_ref, cmw_ref, cmb_ref,
     o_ref, sgla_ref, sgdn_ref, tail_ref, st_gla, st_gdn, hist) = refs[ns:]

    @pl.when(pl.program_id(1) == 0)
    def _():
        st_gla[...] = jnp.zeros_like(st_gla)
        st_gdn[...] = jnp.zeros_like(st_gdn)
        hist[...] = jnp.zeros_like(hist)

    m = _Masks(ROWS)
    g0 = P_GDN
    taps = _conv_taps(cw_ref)
    xs, qkvs = [], []
    for s in range(ns):
        x = p_refs[s][:, g0:g0 + CONV_W]
        acc = _conv_prompt(x, hist[s], taps)
        hist[s] = x[sr - HIST:sr]
        o_ref[s, :, 768:1024] = p_refs[s][:, g0 + 1280:g0 + 1536] * acc[:, 768:1024]
        xs.append(x)
        qkvs.append(_silu(acc[:, 0:768]))
    qkv = jnp.concatenate(qkvs, axis=0)
    qn, kn = _gdn_qk_norm(qkv)
    blocks = [(s, c) for c in range(nc) for s in range(ns)]
    rows_of = lambda s, c: slice(s * sr + c * ROWS, s * sr + (c + 1) * ROWS)
    gdn = _gdn_prep([qn[rows_of(s, c)] for s, c in blocks], [kn[rows_of(s, c)] for s, c in blocks],
                    [qkv[rows_of(s, c), 512:768] for s, c in blocks],
                    [p_refs[s][_block_rows(c), g0 + 1536:g0 + 1792] for s, c in blocks],
                    [p_refs[s][_block_rows(c), g0 + 1792:g0 + 2048] for s, c in blocks], m, ROWS)
    gla = [_gla_prep(p_refs[s][_block_rows(c), P_GLA:P_GLA + 896], m, 1) for s, c in blocks]
    sg = [st_gla[s] for s in range(ns)]
    sd = [st_gdn[s] for s in range(ns)]
    og, od = {}, {}
    for c in range(nc):
        o, sg = _gla_scan(gla[c * ns:(c + 1) * ns], sg)
        og.update({(s, c): o[s] for s in range(ns)})
        o, sd = _gdn_scan(gdn[c * ns:(c + 1) * ns], sd, m, 1)
        od.update({(s, c): o[s] for s in range(ns)})
    by_rows = lambda d: jnp.concatenate([d[(s, c)] for s in range(ns) for c in range(nc)], axis=0)
    rs = jnp.concatenate([p_refs[s][:, 512:768] for s in range(ns)], axis=0)
    zs = jnp.concatenate([p_refs[s][:, g0 + 1024:g0 + 1280] for s in range(ns)], axis=0)
    o_gla = _gla_out(by_rows(og), gn_gla_ref[...], rs)
    o_gdn = _gdn_out(by_rows(od), gn_gdn_ref[...], zs)
    for s in range(ns):
        st_gla[s] = sg[s]
        st_gdn[s] = sd[s]
        o_ref[s, :, 0:256] = o_gla[s * sr:(s + 1) * sr]
        o_ref[s, :, 256:512] = o_gdn[s * sr:(s + 1) * sr]
        for c in range(sr // CM_CHUNK):
            rows = slice(c * CM_CHUNK, (c + 1) * CM_CHUNK)
            o_ref[s, rows, 512:768] = _cm_block(p_refs[s][rows, P_CM:P_CM + 512], cmw_ref[...], cmb_ref[...], CM_CHUNK)

    @pl.when(pl.program_id(1) == pl.num_programs(1) - 1)
    def _():
        for s in range(ns):
            sgla_ref[s] = _compact_state(sg[s], GLA_DV, GLA_HEADS)
            sgdn_ref[s] = _compact_state(sd[s], GDN_DV, GDN_HEADS)
            tail_ref[s] = xs[s][sr - HIST:sr]


def _mixer_prompt(p, cw, gn_gla, gn_gdn, cmw, cmb, *, layer, nseqs, seq_len):
    ns, sr = PROMPT_SEQS, PROMPT_SEQ_ROWS
    nsteps = seq_len // sr
    seq_spec = lambda s: pl.BlockSpec((sr, P_COLS), lambda o, c: ((o * ns + s) * nsteps + c, 0))
    per_group = lambda *shape: pl.BlockSpec((ns,) + shape, lambda o, c: (o,) + (0,) * len(shape))
    per_layer = lambda *shape: _layer_spec(shape, layer)
    return pl.pallas_call(
        _mixer_prompt_kernel,
        grid=(nseqs // ns, nsteps),
        in_specs=[seq_spec(s) for s in range(ns)] + [
            per_layer(GDN_CONV, CONV_W), per_layer(1, 256), per_layer(1, 256),
            per_layer(CM_CHUNK, CM_GROUPS * CM_CHUNK), per_layer(CM_CHUNK, 256)],
        out_specs=[pl.BlockSpec((ns, sr, 4 * BRANCH_W), lambda o, c: (o, c, 0)),
                   per_group(GLA_QK, GLA_DV), per_group(GDN_QK, GDN_DV), per_group(HIST, CONV_W)],
        out_shape=[jax.ShapeDtypeStruct((nseqs, seq_len, 4 * BRANCH_W), F32),
                   jax.ShapeDtypeStruct((nseqs, GLA_QK, GLA_DV), F32),
                   jax.ShapeDtypeStruct((nseqs, GDN_QK, GDN_DV), F32),
                   jax.ShapeDtypeStruct((nseqs, HIST, CONV_W), F32)],
        scratch_shapes=[pltpu.VMEM((ns, GLA_QK, GLA_HEADS * GLA_DV), F32), pltpu.VMEM((ns, GDN_QK, GDN_QK), F32),
                        pltpu.VMEM((ns, HIST, CONV_W), F32)],
        compiler_params=pltpu.CompilerParams(dimension_semantics=("arbitrary", "arbitrary"),
                                             vmem_limit_bytes=VMEM_LIMIT),
        name="mixer_prompt",
    )(*([p] * ns), cw, gn_gla, gn_gdn, cmw, cmb)


def _mixer_sample_kernel(p_ref, hist_ref, s0gla_ref, s0gdn_ref, cw_ref, gn_gla_ref, gn_gdn_ref, cmw_ref, cmb_ref,
                         o_ref, sgla_ref, sgdn_ref, *, seg):
    nseq = ROWS // seg
    nb = range(SAMPLE_BLOCKS_PER_STEP)
    m = _Masks(seg)
    g0 = P_GDN
    x = p_ref[:, g0:g0 + CONV_W]
    acc = _conv_sample(x, hist_ref[...], _conv_taps(cw_ref), seg)
    o_ref[:, 768:1024] = p_ref[:, g0 + 1280:g0 + 1536] * acc[:, 768:1024]
    qkv = _silu(acc[:, 0:768])
    qn, kn = _gdn_qk_norm(qkv)
    gdn = _gdn_prep([qn[_block_rows(c)] for c in nb], [kn[_block_rows(c)] for c in nb],
                    [qkv[_block_rows(c), 512:768] for c in nb],
                    [p_ref[_block_rows(c), g0 + 1536:g0 + 1792] for c in nb],
                    [p_ref[_block_rows(c), g0 + 1792:g0 + 2048] for c in nb], m, seg)
    gla = [_gla_prep(p_ref[_block_rows(c), P_GLA:P_GLA + 896], m, nseq) for c in nb]
    seqs = lambda c: slice(c * nseq, (c + 1) * nseq)
    sg = [_expand_state(s0gla_ref[seqs(c)].reshape(nseq * GLA_QK, GLA_DV), GLA_DV, GLA_HEADS) * m.bd_gla for c in nb]
    sd = [_expand_state(s0gdn_ref[seqs(c)].reshape(nseq * GDN_QK, GDN_DV), GDN_DV, GDN_HEADS) * m.bd_gdn for c in nb]
    og, sg = _gla_scan(gla, sg)
    od, sd = _gdn_scan(gdn, sd, m, nseq)
    for c in nb:
        rows = _block_rows(c)
        sgla_ref[seqs(c)] = _compact_state(sg[c], GLA_DV, GLA_HEADS).reshape(nseq, GLA_QK, GLA_DV)
        sgdn_ref[seqs(c)] = _compact_state(sd[c], GDN_DV, GDN_HEADS).reshape(nseq, GDN_QK, GDN_DV)
        o_ref[rows, 512:768] = _cm_block(p_ref[rows, P_CM:P_CM + 512], cmw_ref[...], cmb_ref[...], seg)
    o_ref[:, 0:256] = _gla_out(jnp.concatenate(og, axis=0), gn_gla_ref[...], p_ref[:, 512:768])
    o_ref[:, 256:512] = _gdn_out(jnp.concatenate(od, axis=0), gn_gdn_ref[...], p_ref[:, g0 + 1024:g0 + 1280])


def _mixer_sample(p, hist, s0gla, s0gdn, cw, gn_gla, gn_gdn, cmw, cmb, *, layer, base_step, nsteps, seg):
    rows = SAMPLE_STEP_ROWS
    nseq_step = rows // seg
    per_layer = lambda *shape: _layer_spec(shape, layer)
    return pl.pallas_call(
        functools.partial(_mixer_sample_kernel, seg=seg),
        grid=(nsteps,),
        in_specs=[pl.BlockSpec((rows, P_COLS), lambda i: (base_step + i, 0)),
                  pl.BlockSpec((None, rows, CONV_W), lambda i: (layer, i, 0)),
                  pl.BlockSpec((None, nseq_step, GLA_QK, GLA_DV), lambda i: (layer, i, 0, 0)),
                  pl.BlockSpec((None, nseq_step, GDN_QK, GDN_DV), lambda i: (layer, i, 0, 0)),
                  per_layer(GDN_CONV, CONV_W), per_layer(1, 256), per_layer(1, 256),
                  per_layer(ROWS, CM_GROUPS * ROWS), per_layer(ROWS, 256)],
        out_specs=[pl.BlockSpec((rows, 4 * BRANCH_W), lambda i: (i, 0)),
                   pl.BlockSpec((nseq_step, GLA_QK, GLA_DV), lambda i: (i, 0, 0)),
                   pl.BlockSpec((nseq_step, GDN_QK, GDN_DV), lambda i: (i, 0, 0))],
        out_shape=[jax.ShapeDtypeStruct((nsteps * rows, 4 * BRANCH_W), F32),
                   jax.ShapeDtypeStruct((nsteps * nseq_step, GLA_QK, GLA_DV), F32),
                   jax.ShapeDtypeStruct((nsteps * nseq_step, GDN_QK, GDN_DV), F32)],
        compiler_params=pltpu.CompilerParams(dimension_semantics=("parallel",), vmem_limit_bytes=VMEM_LIMIT),
        name="mixer_sample",
    )(p, hist, s0gla, s0gdn, cw, gn_gla, gn_gdn, cmw, cmb)


def _merge_kernel(*refs, prompt_tiles, pair):
    nh = 2 if pair else 1
    bp_ref, bs_ref, nw_ref, wg_ref, wb_ref, wo_ref, o_ref = refs[nh:]
    h = _residual_rows(refs[:nh], prompt_tiles)
    xn = _rms(h, nw_ref[...]).astype(BF16)
    br = _stream_rows(bp_ref, bs_ref, prompt_tiles).astype(BF16)
    merged = None
    for gi in range(N_BRANCH):
        gate = _sigmoid(_dot(xn, wg_ref[:, gi * D_MODEL:(gi + 1) * D_MODEL]))
        term = _dot(br[:, gi * BRANCH_W:(gi + 1) * BRANCH_W], wb_ref[gi]) * gate
        merged = term if merged is None else merged + term
    o_ref[...] = h + _dot(merged.astype(BF16), wo_ref[...])


def _stream_specs(tm, width, prompt_tiles, lead=()):
    nlead = (None,) * len(lead)
    return (pl.BlockSpec(nlead + (tm, width), lambda i: lead + (jnp.minimum(i, prompt_tiles - 1), 0)),
            pl.BlockSpec(nlead + (tm, width), lambda i: lead + (jnp.maximum(i - prompt_tiles, 0), 0)))


def _merge(h, br_p, br_s, nw, wg, wb, wo, *, layer):
    tm = TOKEN_TILE
    h_arrays, h_specs, _, ntok = _residual_specs(h, tm)
    prompt_tiles = br_p.shape[0] // tm
    row = lambda n: pl.BlockSpec((tm, n), lambda i: (i, 0))
    return pl.pallas_call(
        functools.partial(_merge_kernel, prompt_tiles=prompt_tiles, pair=isinstance(h, tuple)),
        grid=(ntok // tm,),
        in_specs=[*h_specs, *_stream_specs(tm, 4 * BRANCH_W, prompt_tiles), _layer_spec((1, D_MODEL), layer),
                  _layer_spec((D_MODEL, N_BRANCH * D_MODEL), layer),
                  _layer_spec((N_BRANCH, BRANCH_W, D_MODEL), layer), _layer_spec((D_MODEL, D_MODEL), layer)],
        out_specs=row(D_MODEL),
        out_shape=jax.ShapeDtypeStruct((ntok, D_MODEL), F32),
        compiler_params=pltpu.CompilerParams(dimension_semantics=("parallel",), vmem_limit_bytes=VMEM_LIMIT),
        name="merge",
    )(*h_arrays, br_p, br_s, nw, wg, wb, wo)


def _ffn_kernel(h_ref, pp_ref, ps_ref, nf_ref, wfg_ref, wfu_ref, wfd_ref, np_ref, wpg_ref, wp_ref, nfin_ref,
                *o_refs, final, prompt_tiles):
    h = h_ref[...]
    xf = _rms(h, nf_ref[...]).astype(BF16)
    act = _silu(_dot(xf, wfg_ref[...])) * _dot(xf, wfu_ref[...])
    h = h + _dot(act.astype(BF16), wfd_ref[...])
    pg = _sigmoid(_dot(_rms(h, np_ref[...]).astype(BF16), wpg_ref[...]))
    pe = _stream_rows(pp_ref, ps_ref, prompt_tiles).astype(BF16)
    h = h + pg * _dot(pe, wp_ref[...])
    if not final:
        o_refs[0][...] = h
        return
    out = _rms(h, nfin_ref[...])
    op_ref, os_ref = o_refs

    @pl.when(pl.program_id(0) < prompt_tiles)
    def _():
        op_ref[...] = out

    @pl.when(pl.program_id(0) >= prompt_tiles)
    def _():
        os_ref[...] = out


def _ffn(h, pe_p, pe_s, nf, wfg, wfu, wfd, npl, wpg, wp, nfin, *, layer, final):
    ntok = h.shape[0]
    tm = TOKEN_TILE
    npt = pe_p.shape[1]
    prompt_tiles = npt // tm
    row = lambda n: pl.BlockSpec((tm, n), lambda i: (i, 0))
    once = lambda shape: _layer_spec(shape, layer, buffers=1)
    if final:
        out_specs = list(_stream_specs(tm, D_MODEL, prompt_tiles))
        out_shape = [jax.ShapeDtypeStruct((npt, D_MODEL), F32), jax.ShapeDtypeStruct((ntok - npt, D_MODEL), F32)]
    else:
        out_specs, out_shape = row(D_MODEL), jax.ShapeDtypeStruct((ntok, D_MODEL), F32)
    return pl.pallas_call(
        functools.partial(_ffn_kernel, final=final, prompt_tiles=prompt_tiles),
        grid=(ntok // tm,),
        in_specs=[row(D_MODEL), *_stream_specs(tm, PLE_DIM, prompt_tiles, lead=(layer,)),
                  _layer_spec((1, D_MODEL), layer),
                  once((D_MODEL, D_FF)), once((D_MODEL, D_FF)), once((D_FF, D_MODEL)),
                  _layer_spec((1, D_MODEL), layer), once((D_MODEL, D_MODEL)), once((PLE_DIM, D_MODEL)),
                  _const_spec((1, D_MODEL))],
        out_specs=out_specs,
        out_shape=out_shape,
        compiler_params=pltpu.CompilerParams(dimension_semantics=("arbitrary",), vmem_limit_bytes=VMEM_LIMIT),
        name="ffn_final" if final else "ffn",
    )(h, pe_p, pe_s, nf, wfg, wfu, wfd, npl, wpg, wp, nfin)


def _split_w_in(w_in):
    span = lambda s: w_in[..., s[0]:s[1]].astype(BF16)
    rep = lambda o: jnp.repeat(w_in[..., o:o + GDN_HEADS], GDN_DK, axis=-1)
    small = jnp.concatenate([w_in[..., W_IN_GA:W_IN_GA + GLA_RANK],
                             jnp.zeros(w_in.shape[:-1] + (LANE - GLA_RANK,), w_in.dtype),
                             rep(W_IN_DA), rep(W_IN_DB)], axis=-1).astype(BF16)
    return span(W_IN_GLA), span(W_IN_GDN), span(W_IN_REST), small


def kernel(x_prompt, x_sample, state_gla, state_gdn, state_gdn_conv, state_sconv, p_prompt, p_sample, norm_mix, w_in, gla_wa2, gla_ba, gla_norm, gdn_conv_w, gdn_a_log, gdn_dt_bias, gdn_norm, cm_ln_g, cm_ln_b, cm_ws, cm_bs, sc_conv_w, w_gate, w_branch, w_o, norm_ffn, w_ffn_gate, w_ffn_up, w_ffn_down, norm_ple, w_ple_gate, w_ple, norm_final):
    depth = w_in.shape[0]
    bp, tp, _ = x_prompt.shape
    bs, ts, _ = x_sample.shape
    npt, nst = bp * tp, bs * ts
    sseq = ROWS // ts
    assert ts == HIST and nst % SAMPLE_STEP_ROWS == 0
    assert bp % PROMPT_SEQS == 0 and tp % PROMPT_SEQ_ROWS == 0 and PROMPT_SEQ_ROWS % CM_CHUNK == 0
    assert npt % TOKEN_TILE == 0 and nst % TOKEN_TILE == 0

    h = (x_prompt.reshape(npt, D_MODEL), x_sample.reshape(nst, D_MODEL))
    pe_p = p_prompt.reshape(depth, npt, PLE_DIM)
    pe_s = p_sample.reshape(depth, nst, PLE_DIM)
    s0_gla = state_gla.reshape(depth, bs, GLA_QK, GLA_DV)
    s0_gdn = state_gdn.reshape(depth, bs, GDN_QK, GDN_DV)
    rows = lambda a: a.reshape(depth, 1, -1)
    w_in_groups = _split_w_in(w_in)
    wg, wb, wo = w_gate.astype(BF16), w_branch.astype(BF16), w_o.astype(BF16)
    wfg, wfu, wfd = w_ffn_gate.astype(BF16), w_ffn_up.astype(BF16), w_ffn_down.astype(BF16)
    wpg, wp = w_ple_gate.astype(BF16), w_ple.astype(BF16)
    wa2 = jnp.pad(gla_wa2, ((0, 0), (0, LANE - GLA_RANK), (0, 0))).astype(BF16)
    nmix, nffn, nple = rows(norm_mix), rows(norm_ffn), rows(norm_ple)
    inproj_vecs = (rows(gla_ba), rows(jnp.repeat(gdn_a_log, GDN_DK, axis=1)),
                   rows(jnp.repeat(gdn_dt_bias, GDN_DK, axis=1)), rows(cm_ln_g), rows(cm_ln_b))
    cw = jnp.concatenate([gdn_conv_w, jnp.pad(sc_conv_w, ((0, 0), (GDN_CONV - SC_WIDTH, 0), (0, 0)))], axis=2)
    hist_s = jnp.concatenate([
        jnp.pad(state_gdn_conv, ((0, 0), (0, 0), (HIST - (GDN_CONV - 1), 0), (0, 0))),
        jnp.pad(state_sconv, ((0, 0), (0, 0), (HIST - (SC_WIDTH - 1), 0), (0, 0)))], axis=3).reshape(depth, nst, CONV_W)
    gn_gla = rows(jnp.tile(gla_norm, (1, GLA_HEADS)))
    gn_gdn = rows(jnp.tile(gdn_norm, (1, GDN_HEADS)))
    cmw_p = jnp.transpose(cm_ws, (0, 2, 1, 3)).reshape(depth, CM_CHUNK, CM_GROUPS * CM_CHUNK)
    cmb_p = jnp.repeat(jnp.swapaxes(cm_bs, 1, 2), BRANCH_W // CM_GROUPS, axis=2)
    cmw_s = jnp.transpose(jnp.tile(cm_ws[:, :, :ts, :ts], (1, 1, sseq, sseq)),
                          (0, 2, 1, 3)).reshape(depth, ROWS, CM_GROUPS * ROWS)
    cmb_s = jnp.tile(cmb_p[:, :ts], (1, sseq, 1))

    outs = {k: [] for k in ("gla_p", "gla_s", "gdn_p", "gdn_s", "gc_p", "gc_s", "sc_p", "sc_s", "cv_s")}
    for i in range(depth):
        p = _inproj(h, nmix, *w_in_groups, wa2, *inproj_vecs, layer=i)
        br_p, gla_p, gdn_p, tail_p = _mixer_prompt(p, cw, gn_gla, gn_gdn, cmw_p, cmb_p, layer=i, nseqs=bp, seq_len=tp)
        br_p = br_p.reshape(npt, N_BRANCH * BRANCH_W)
        br_s, gla_s, gdn_s = _mixer_sample(p, hist_s, s0_gla, s0_gdn, cw, gn_gla, gn_gdn, cmw_s, cmb_s, layer=i,
                                           base_step=npt // SAMPLE_STEP_ROWS, nsteps=nst // SAMPLE_STEP_ROWS, seg=ts)
        outs["gla_p"].append(gla_p.reshape(bp, GLA_HEADS, GLA_DK, GLA_DV))
        outs["gla_s"].append(gla_s.reshape(bs, GLA_HEADS, GLA_DK, GLA_DV))
        outs["gdn_p"].append(gdn_p.reshape(bp, GDN_HEADS, GDN_DK, GDN_DV))
        outs["gdn_s"].append(gdn_s.reshape(bs, GDN_HEADS, GDN_DK, GDN_DV))
        xs3 = p[npt:, P_GDN:P_GDN + CONV_W].reshape(bs, ts, CONV_W)
        outs["gc_p"].append(tail_p[:, HIST - (GDN_CONV - 1):, 0:768])
        outs["gc_s"].append(xs3[:, ts - (GDN_CONV - 1):, 0:768])
        outs["sc_p"].append(tail_p[:, HIST - (SC_WIDTH - 1):, 768:])
        outs["sc_s"].append(xs3[:, ts - (SC_WIDTH - 1):, 768:])
        outs["cv_s"].append(p[npt:, P_CM + 256:P_CM + 512].reshape(bs, ts, BRANCH_W))

        h1 = _merge(h, br_p, br_s, nmix, wg, wb, wo, layer=i)
        h = _ffn(h1, pe_p, pe_s, nffn, wfg, wfu, wfd, nple, wpg, wp, norm_final.reshape(1, D_MODEL),
                 layer=i, final=(i == depth - 1))

    y_prompt = h[0].reshape(bp, tp, D_MODEL)
    y_sample = h[1].reshape(bs, ts, D_MODEL)
    st = lambda k: jnp.stack(outs[k])
    return (y_prompt, y_sample, st("gla_p"), st("gla_s"), st("gdn_p"), st("gdn_s"),
            st("gc_p"), st("gc_s"), st("sc_p"), st("sc_s"), st("cv_s"))
```

```python
import functools

import jax
import jax.numpy as jnp
from jax import lax
from jax.experimental import pallas as pl
from jax.experimental.pallas import tpu as pltpu

F32 = jnp.float32
BF16 = jnp.bfloat16

D_MODEL = 1024
PLE_DIM = 256
BRANCH_W = 256
N_BRANCH = 4
GLA_HEADS = 4
GLA_DK = 32
GLA_DV = 64
GLA_RANK = 16
GLA_TAU = 16.0
GDN_HEADS = 4
GDN_DK = 64
GDN_DV = 64
GDN_CONV = 4
CM_GROUPS = 4
CM_CHUNK = 128
SC_WIDTH = 3
D_FF = 2816
EPS = 1e-6

ROWS = 64
PROMPT_SEQS = 4
PROMPT_CHUNKS = 2
PROMPT_SEQ_ROWS = ROWS * PROMPT_CHUNKS
SAMPLE_BLOCKS_PER_STEP = 2
SAMPLE_STEP_ROWS = ROWS * SAMPLE_BLOCKS_PER_STEP
GLA_QK = GLA_HEADS * GLA_DK
GDN_QK = GDN_HEADS * GDN_DK
LANE = 128
CONV_W = 3 * BRANCH_W + BRANCH_W
HIST = 8
SUM_PIECES = 2

W_IN_GLA, W_IN_GDN, W_IN_REST = (0, 768), (784, 1808), (1816, 3096)
W_IN_GA, W_IN_DA, W_IN_DB = 768, 1808, 1812
SMALL_COLS = LANE + 2 * 256
P_GLA = 0
P_GDN = 896
P_CM = 2944
P_COLS = 3456

VMEM_LIMIT = 56 * 1024 * 1024
TOKEN_TILE = 512


def _dot(a, b):
    return jnp.dot(a, b, preferred_element_type=F32)


def _dot_nt(a, b):
    return lax.dot_general(a, b, (((1,), (1,)), ((), ())), preferred_element_type=F32)


def _dot_tn(a, b):
    return lax.dot_general(a, b, (((0,), (0,)), ((), ())), preferred_element_type=F32)


def _split(x, n):
    parts, r = [], x
    for i in range(n):
        p = r.astype(BF16)
        parts.append(p)
        if i + 1 < n:
            r = r - p.astype(F32)
    return parts


def _dot1(a, b, dot=_dot):
    return dot(a.astype(BF16), b.astype(BF16))


def _dot3(a, b_pieces):
    ah, al = _split(a, 2)
    bh, bl = b_pieces
    return _dot(jnp.concatenate([ah, ah, al], axis=1), jnp.concatenate([bh, bl, bh], axis=0))


def _mask_dot(mask, x, n):
    return _dot(jnp.concatenate([mask.astype(BF16)] * n, axis=1), jnp.concatenate(_split(x, n), axis=0))


def _mask_dot_nt(mask, x, n):
    return _dot_nt(jnp.concatenate([mask.astype(BF16)] * n, axis=1), jnp.concatenate(_split(x, n), axis=1))


def _mask_dot_rhs(x, mask, n):
    return _dot(jnp.concatenate(_split(x, n), axis=1), jnp.concatenate([mask.astype(BF16)] * n, axis=0))


def _sigmoid(x):
    return 1.0 / (1.0 + jnp.exp(-x))


def _silu(x):
    return x * _sigmoid(x)


def _softplus(x):
    return jnp.maximum(x, 0.0) + jnp.log1p(jnp.exp(-jnp.abs(x)))


def _gelu_tanh(x):
    return 0.5 * x * (1.0 + jnp.tanh(0.7978845608028654 * (x + 0.044715 * (x * x * x))))


def _rms(x, w):
    return x * lax.rsqrt(jnp.mean(x * x, axis=-1, keepdims=True) + EPS) * w


def _idiv(x, n):
    assert n & (n - 1) == 0
    return lax.shift_right_logical(x, n.bit_length() - 1)


def _imod(x, n):
    assert n & (n - 1) == 0
    return lax.bitwise_and(x, n - 1)


def _const_spec(shape):
    return pl.BlockSpec(shape, lambda *_: (0,) * len(shape))


def _layer_spec(shape, layer, buffers=None):
    mode = {} if buffers is None else {"pipeline_mode": pl.Buffered(buffers)}
    return pl.BlockSpec((None,) + shape, lambda *_: (layer,) + (0,) * len(shape), **mode)


def _iota2(shape):
    return lax.broadcasted_iota(jnp.int32, shape, 0), lax.broadcasted_iota(jnp.int32, shape, 1)


def _stream_rows(hp_ref, hs_ref, prompt_tiles):
    return jnp.where(pl.program_id(0) < prompt_tiles, hp_ref[...], hs_ref[...])


def _residual_rows(h_refs, prompt_tiles):
    return h_refs[0][...] if len(h_refs) == 1 else _stream_rows(*h_refs, prompt_tiles)


def _residual_specs(h, tm):
    if isinstance(h, tuple):
        prompt_tiles = h[0].shape[0] // tm
        return list(h), list(_stream_specs(tm, D_MODEL, prompt_tiles)), prompt_tiles, h[0].shape[0] + h[1].shape[0]
    return [h], [pl.BlockSpec((tm, D_MODEL), lambda i: (i, 0))], None, h.shape[0]


def _inproj_kernel(*refs, prompt_tiles):
    nh = 1 if prompt_tiles is None else 2
    (nw_ref, wgla_ref, wgdn_ref, wrest_ref, wsmall_ref, wa2_ref, ba_ref, alog_ref,
     dtb_ref, lng_ref, lnb_ref, p_ref) = refs[nh:]
    xn = _rms(_residual_rows(refs[:nh], prompt_tiles), nw_ref[...]).astype(BF16)
    pg = _dot(xn, wgla_ref[...])
    p_ref[:, 0:128] = pg[:, 0:128] * (GLA_DK ** -0.5)
    p_ref[:, 128:512] = pg[:, 128:512]
    p_ref[:, 512:768] = _silu(pg[:, 512:768])
    ps = _dot(xn, wsmall_ref[...])
    za = _dot(ps[:, 0:LANE].astype(BF16), wa2_ref[...]) + ba_ref[...]
    p_ref[:, 768:896] = -_softplus(-za) * (1.0 / GLA_TAU)
    pd = _dot(xn, wgdn_ref[...])
    pr = _dot(xn, wrest_ref[...])
    g0 = P_GDN
    p_ref[:, g0:g0 + 768] = pd[:, 0:768]
    p_ref[:, g0 + 768:g0 + 1024] = pr[:, 1024:1280] * pr[:, 512:768]
    p_ref[:, g0 + 1024:g0 + 1280] = _silu(pd[:, 768:1024])
    p_ref[:, g0 + 1280:g0 + 1536] = pr[:, 768:1024]
    p_ref[:, g0 + 1536:g0 + 1792] = -jnp.exp(alog_ref[...]) * _softplus(ps[:, LANE:LANE + 256] + dtb_ref[...])
    p_ref[:, g0 + 1792:g0 + 2048] = _sigmoid(ps[:, LANE + 256:LANE + 512])
    p_ref[:, P_CM:P_CM + 256] = _gelu_tanh(pr[:, 0:256])
    gv = _gelu_tanh(pr[:, 256:512])
    mu = jnp.mean(gv, axis=-1, keepdims=True)
    d = gv - mu
    var = jnp.mean(d * d, axis=-1, keepdims=True)
    p_ref[:, P_CM + 256:P_CM + 512] = d * lax.rsqrt(var + EPS) * lng_ref[...] + lnb_ref[...]


def _inproj(h, nw, wgla, wgdn, wrest, wsmall, wa2, ba, alog, dtb, lng, lnb, *, layer):
    tm = TOKEN_TILE
    h_arrays, h_specs, prompt_tiles, ntok = _residual_specs(h, tm)
    cols = lambda span: span[1] - span[0]
    per_layer = lambda *shape: _layer_spec(shape, layer)
    return pl.pallas_call(
        functools.partial(_inproj_kernel, prompt_tiles=prompt_tiles),
        grid=(ntok // tm,),
        in_specs=[*h_specs, per_layer(1, D_MODEL),
                  per_layer(D_MODEL, cols(W_IN_GLA)), per_layer(D_MODEL, cols(W_IN_GDN)),
                  per_layer(D_MODEL, cols(W_IN_REST)), per_layer(D_MODEL, SMALL_COLS),
                  per_layer(LANE, GLA_QK), per_layer(1, GLA_QK), per_layer(1, 256),
                  per_layer(1, 256), per_layer(1, 256), per_layer(1, 256)],
        out_specs=pl.BlockSpec((tm, P_COLS), lambda i: (i, 0)),
        out_shape=jax.ShapeDtypeStruct((ntok, P_COLS), F32),
        compiler_params=pltpu.CompilerParams(dimension_semantics=("parallel",), vmem_limit_bytes=VMEM_LIMIT),
        name="inproj",
    )(*h_arrays, nw, wgla, wgdn, wrest, wsmall, wa2, ba, alog, dtb, lng, lnb)


def _stack_heads(x, group, nheads, period=None):
    w = x.shape[1]
    li = lax.broadcasted_iota(jnp.int32, (1, w), 1)
    if period is not None:
        li = _imod(li, period)
    hid = _idiv(li, group)
    zero = jnp.zeros_like(x)
    return jnp.concatenate([jnp.where(hid == h, x, zero) for h in range(nheads)], axis=0)


def _block_diag(x, nblocks):
    r = x.shape[0]
    ri, ci = _iota2((nblocks * r, nblocks * r))
    return jnp.where(_idiv(ri, r) == _idiv(ci, r), jnp.concatenate([x] * nblocks, axis=0), jnp.zeros((), x.dtype))


def _widen(x, nseq):
    if nseq == 1:
        return x
    seg = ROWS // nseq
    sid = _idiv(lax.broadcasted_iota(jnp.int32, (ROWS, 1), 0), seg)
    zero = jnp.zeros_like(x)
    return jnp.concatenate([jnp.where(sid == j, x, zero) for j in range(nseq)], axis=1)


def _group_mean(x, group):
    w = x.shape[1]
    ri, ci = _iota2((w, w))
    avg = jnp.where(_idiv(ri, group) == _idiv(ci, group), 1.0 / group, 0.0).astype(BF16)
    return _dot(x.astype(BF16), avg)


def _head_diag_mask(rows, cols, rhead, chead, rper):
    ri, ci = _iota2((rows, cols))
    return (_idiv(_imod(ri, rper), rhead) == _idiv(ci, chead)).astype(F32)


def _expand_state(s, width, reps):
    ri, ci = _iota2((width, reps * width))
    return _mask_dot_rhs(s, ri == _imod(ci, width), 3)


def _compact_state(st, width, reps):
    out = st[:, 0:width]
    for h in range(1, reps):
        out = out + st[:, h * width:(h + 1) * width]
    return out


class _Masks:
    def __init__(self, seg):
        ri, ci = _iota2((ROWS, ROWS))
        self.same = _idiv(ri, seg) == _idiv(ci, seg)
        self.tri = self.same & (ci <= ri)
        rl, cl = _iota2((ROWS, GDN_HEADS * ROWS))
        cl = _imod(cl, ROWS)
        same_l = _idiv(rl, seg) == _idiv(cl, seg)
        self.tri_l = same_l & (cl <= rl)
        self.strict_l = same_l & (cl < rl)
        self.eye_l = (rl == cl).astype(F32)
        self.levels = [(_idiv(rl, 2 * s) == _idiv(cl, 2 * s)) & (_idiv(rl, s) != _idiv(cl, s))
                       for s in (1 << k for k in range(seg.bit_length() - 1))]
        self.ones = jnp.ones((ROWS, ROWS), BF16)
        self.seg = seg
        nseq = ROWS // seg
        self.bd_gla = _head_diag_mask(nseq * GLA_QK, GLA_HEADS * GLA_DV, GLA_DK, GLA_DV, GLA_QK)
        self.bd_gdn = _head_diag_mask(nseq * GDN_QK, GDN_QK, GDN_DK, GDN_DV, GDN_QK)


def _gla_prep(p, m, nseq):
    q, k, v, la = p[:, 0:128], p[:, 128:256], p[:, 256:512], p[:, 768:896]
    b = _mask_dot(m.tri, la, SUM_PIECES)
    if m.seg == ROWS:
        btot = jnp.broadcast_to(b[ROWS - 1:ROWS], b.shape)
    else:
        btot = _mask_dot(m.same, la, SUM_PIECES)
    qd = q * jnp.exp(b)
    kd = k * jnp.exp(-b)
    ke = k * jnp.exp(btot - b)
    a = jnp.where(m.tri_l, _dot1(qd, _stack_heads(kd, GLA_DK, GLA_HEADS), _dot_nt), 0.0)
    o_intra = _dot1(a, _stack_heads(v, GLA_DV, GLA_HEADS))
    dec = jnp.exp(_dot_tn(jnp.concatenate(_split(_widen(la, nseq), SUM_PIECES), axis=0),
                          jnp.ones((SUM_PIECES * ROWS, GLA_HEADS * GLA_DV), BF16)))
    ds = _dot1(_widen(ke, nseq), v, _dot_tn) * m.bd_gla
    return o_intra, _widen(qd, nseq), dec, ds


def _gla_scan(preps, sts):
    outs = [prep[0] + _dot1(prep[1], st) for prep, st in zip(preps, sts)]
    return outs, [st * prep[2] + prep[3] for prep, st in zip(preps, sts)]


def _gla_out(o, gn, rs):
    return o * lax.rsqrt(_group_mean(o * o, GLA_DV) + EPS) * gn * rs


def _gdn_qk_norm(qkv):
    hd = GDN_DK
    r = qkv.shape[0]
    cqk = jnp.concatenate([qkv[:, 0:256], qkv[:, 256:512]], axis=0)
    nrm = lax.rsqrt(_group_mean(cqk * cqk, hd) * hd + EPS)
    return qkv[:, 0:256] * nrm[0:r] * (hd ** -0.5), qkv[:, 256:512] * nrm[r:2 * r]


def _gdn_prep(q, k, cv, gs, betas, m, seg):
    nh, hd = GDN_HEADS, GDN_DK
    n = range(len(q))
    gc = [_mask_dot(m.tri, gs[i], SUM_PIECES) for i in n]
    if seg == ROWS:
        gtot = [jnp.broadcast_to(gc[i][ROWS - 1:ROWS], gc[i].shape) for i in n]
    else:
        gtot = [_mask_dot(m.same, gs[i], SUM_PIECES) for i in n]
    qkk = [_dot1(jnp.concatenate([q[i], k[i]], axis=0), _stack_heads(k[i], hd, nh), _dot_nt) for i in n]
    grow = [_mask_dot(m.ones, gc[i] * m.eye_l, SUM_PIECES) for i in n]
    decay = [jnp.where(m.tri_l, jnp.exp(jnp.where(m.tri_l, gc[i] - grow[i], 0.0)), 0.0) for i in n]
    amat = [jnp.where(m.strict_l, betas[i] * decay[i] * qkk[i][ROWS:2 * ROWS], 0.0) for i in n]
    inv = [m.eye_l - jnp.where(m.levels[0], amat[i], 0.0) for i in n]
    for lvl in m.levels[1:]:
        low = [_block_diag(jnp.where(lvl, amat[i], 0.0).astype(BF16), nh) for i in n]
        prod = [_dot(inv[i].astype(BF16), low[i]) for i in n]
        inv = [inv[i] - _dot(prod[i].astype(BF16), _block_diag(inv[i].astype(BF16), nh)) for i in n]
    eg = [jnp.exp(gc[i]) for i in n]
    rhs = [jnp.concatenate([betas[i] * cv[i], betas[i] * eg[i] * k[i]], axis=1) for i in n]
    uw = [_dot(inv[i].astype(BF16), _stack_heads(rhs[i].astype(BF16), hd, nh, period=GDN_QK)) for i in n]
    return [(uw[i][:, 0:256], uw[i][:, 256:512], q[i] * eg[i], qkk[i][0:ROWS] * decay[i],
             k[i] * jnp.exp(gtot[i] - gc[i]), jnp.exp(gtot[i])) for i in n]


def _gdn_scan(preps, sts, m, nseq):
    seg = ROWS // nseq
    n = range(len(preps))
    ws = [_dot1(jnp.concatenate([_widen(preps[i][1], nseq), _widen(preps[i][2], nseq)], axis=0), sts[i]) for i in n]
    u = [preps[i][0] - ws[i][0:ROWS] for i in n]
    outs = [ws[i][ROWS:2 * ROWS] + _dot1(preps[i][3], _stack_heads(u[i], GDN_DV, GDN_HEADS)) for i in n]
    new = []
    for i in n:
        dn = preps[i][5]
        dn_tall = jnp.concatenate(
            [jnp.broadcast_to(dn[j * seg:j * seg + 1], (GDN_QK, GDN_QK)) for j in range(nseq)], axis=0)
        new.append(sts[i] * dn_tall + _dot1(_widen(preps[i][4], nseq), u[i], _dot_tn) * m.bd_gdn)
    return outs, new


def _gdn_out(o, gn, zs):
    return o * lax.rsqrt(_group_mean(o * o, GDN_DV) + EPS) * gn * zs


def _cm_block(p, ws, bias, seg):
    r = p.shape[0]
    gu, vn = p[:, 0:256], p[:, 256:512]
    ri, ci = _iota2((r, CM_GROUPS * r))
    ci = _imod(ci, r)
    wm = jnp.where((_idiv(ri, seg) == _idiv(ci, seg)) & (ci <= ri), ws, 0.0)
    return gu * (_dot1(wm, _stack_heads(vn, BRANCH_W // CM_GROUPS, CM_GROUPS)) + bias)


def _conv_taps(cw_ref):
    return [cw_ref[GDN_CONV - 1 - d:GDN_CONV - d, :] for d in range(GDN_CONV)]


def _conv_prompt(x, hist, taps):
    n = x.shape[0]
    t8 = lax.broadcasted_iota(jnp.int32, (HIST, 1), 0)
    acc = taps[0] * x
    for d in range(1, GDN_CONV):
        xr = pltpu.roll(x, d, 0)
        head = jnp.where(t8 < d, pltpu.roll(hist, d, 0), xr[0:HIST])
        acc = acc + taps[d] * jnp.concatenate([head, xr[HIST:n]], axis=0)
    return acc


def _conv_sample(x, hist, taps, seg):
    n = x.shape[0]
    tloc = _imod(lax.broadcasted_iota(jnp.int32, (n, 1), 0), seg)
    acc = taps[0] * x
    for d in range(1, GDN_CONV):
        prev = jnp.where(tloc < d, pltpu.roll(hist, (d - seg) % n, 0), pltpu.roll(x, d, 0))
        acc = acc + taps[d] * prev
    return acc


def _block_rows(c):
    return slice(c * ROWS, (c + 1) * ROWS)


def _mixer_prompt_kernel(*refs):
    ns, nc, sr = PROMPT_SEQS, PROMPT_CHUNKS, PROMPT_SEQ_ROWS
    p_refs = refs[:ns]
    (cw_ref, gn_gla_ref, gn_gdn_ref, cmw_ref, cmb_ref,
     o_ref, sgla_ref, sgdn_ref, tail_ref, st_gla, st_gdn, hist) = refs[ns:]

    @pl.when(pl.program_id(1) == 0)
    def _():
        st_gla[...] = jnp.zeros_like(st_gla)
        st_gdn[...] = jnp.zeros_like(st_gdn)
        hist[...] = jnp.zeros_like(hist)

    m = _Masks(ROWS)
    g0 = P_GDN
    taps = _conv_taps(cw_ref)
    xs, qkvs = [], []
    for s in range(ns):
        x = p_refs[s][:, g0:g0 + CONV_W]
        acc = _conv_prompt(x, hist[s], taps)
        hist[s] = x[sr - HIST:sr]
        o_ref[s, :, 768:1024] = p_refs[s][:, g0 + 1280:g0 + 1536] * acc[:, 768:1024]
        xs.append(x)
        qkvs.append(_silu(acc[:, 0:768]))
    qkv = jnp.concatenate(qkvs, axis=0)
    qn, kn = _gdn_qk_norm(qkv)
    blocks = [(s, c) for c in range(nc) for s in range(ns)]
    rows_of = lambda s, c: slice(s * sr + c * ROWS, s * sr + (c + 1) * ROWS)
    gdn = _gdn_prep([qn[rows_of(s, c)] for s, c in blocks], [kn[rows_of(s, c)] for s, c in blocks],
                    [qkv[rows_of(s, c), 512:768] for s, c in blocks],
                    [p_refs[s][_block_rows(c), g0 + 1536:g0 + 1792] for s, c in blocks],
                    [p_refs[s][_block_rows(c), g0 + 1792:g0 + 2048] for s, c in blocks], m, ROWS)
    gla = [_gla_prep(p_refs[s][_block_rows(c), P_GLA:P_GLA + 896], m, 1) for s, c in blocks]
    sg = [st_gla[s] for s in range(ns)]
    sd = [st_gdn[s] for s in range(ns)]
    og, od = {}, {}
    for c in range(nc):
        o, sg = _gla_scan(gla[c * ns:(c + 1) * ns], sg)
        og.update({(s, c): o[s] for s in range(ns)})
        o, sd = _gdn_scan(gdn[c * ns:(c + 1) * ns], sd, m, 1)
        od.update({(s, c): o[s] for s in range(ns)})
    by_rows = lambda d: jnp.concatenate([d[(s, c)] for s in range(ns) for c in range(nc)], axis=0)
    rs = jnp.concatenate([p_refs[s][:, 512:768] for s in range(ns)], axis=0)
    zs = jnp.concatenate([p_refs[s][:, g0 + 1024:g0 + 1280] for s in range(ns)], axis=0)
    o_gla = _gla_out(by_rows(og), gn_gla_ref[...], rs)
    o_gdn = _gdn_out(by_rows(od), gn_gdn_ref[...], zs)
    for s in range(ns):
        st_gla[s] = sg[s]
        st_gdn[s] = sd[s]
        o_ref[s, :, 0:256] = o_gla[s * sr:(s + 1) * sr]
        o_ref[s, :, 256:512] = o_gdn[s * sr:(s + 1) * sr]
        for c in range(sr // CM_CHUNK):
            rows = slice(c * CM_CHUNK, (c + 1) * CM_CHUNK)
            o_ref[s, rows, 512:768] = _cm_block(p_refs[s][rows, P_CM:P_CM + 512], cmw_ref[...], cmb_ref[...], CM_CHUNK)

    @pl.when(pl.program_id(1) == pl.num_programs(1) - 1)
    def _():
        for s in range(ns):
            sgla_ref[s] = _compact_state(sg[s], GLA_DV, GLA_HEADS)
            sgdn_ref[s] = _compact_state(sd[s], GDN_DV, GDN_HEADS)
            tail_ref[s] = xs[s][sr - HIST:sr]


def _mixer_prompt(p, cw, gn_gla, gn_gdn, cmw, cmb, *, layer, nseqs, seq_len):
    ns, sr = PROMPT_SEQS, PROMPT_SEQ_ROWS
    nsteps = seq_len // sr
    seq_spec = lambda s: pl.BlockSpec((sr, P_COLS), lambda o, c: ((o * ns + s) * nsteps + c, 0))
    per_group = lambda *shape: pl.BlockSpec((ns,) + shape, lambda o, c: (o,) + (0,) * len(shape))
    per_layer = lambda *shape: _layer_spec(shape, layer)
    return pl.pallas_call(
        _mixer_prompt_kernel,
        grid=(nseqs // ns, nsteps),
        in_specs=[seq_spec(s) for s in range(ns)] + [
            per_layer(GDN_CONV, CONV_W), per_layer(1, 256), per_layer(1, 256),
            per_layer(CM_CHUNK, CM_GROUPS * CM_CHUNK), per_layer(CM_CHUNK, 256)],
        out_specs=[pl.BlockSpec((ns, sr, 4 * BRANCH_W), lambda o, c: (o, c, 0)),
                   per_group(GLA_QK, GLA_DV), per_group(GDN_QK, GDN_DV), per_group(HIST, CONV_W)],
        out_shape=[jax.ShapeDtypeStruct((nseqs, seq_len, 4 * BRANCH_W), F32),
                   jax.ShapeDtypeStruct((nseqs, GLA_QK, GLA_DV), F32),
                   jax.ShapeDtypeStruct((nseqs, GDN_QK, GDN_DV), F32),
                   jax.ShapeDtypeStruct((nseqs, HIST, CONV_W), F32)],
        scratch_shapes=[pltpu.VMEM((ns, GLA_QK, GLA_HEADS * GLA_DV), F32), pltpu.VMEM((ns, GDN_QK, GDN_QK), F32),
                        pltpu.VMEM((ns, HIST, CONV_W), F32)],
        compiler_params=pltpu.CompilerParams(dimension_semantics=("arbitrary", "arbitrary"),
                                             vmem_limit_bytes=VMEM_LIMIT),
        name="mixer_prompt",
    )(*([p] * ns), cw, gn_gla, gn_gdn, cmw, cmb)


def _mixer_sample_kernel(p_ref, hist_ref, s0gla_ref, s0gdn_ref, cw_ref, gn_gla_ref, gn_gdn_ref, cmw_ref, cmb_ref,
                         o_ref, sgla_ref, sgdn_ref, *, seg):
    nseq = ROWS // seg
    nb = range(SAMPLE_BLOCKS_PER_STEP)
    m = _Masks(seg)
    g0 = P_GDN
    x = p_ref[:, g0:g0 + CONV_W]
    acc = _conv_sample(x, hist_ref[...], _conv_taps(cw_ref), seg)
    o_ref[:, 768:1024] = p_ref[:, g0 + 1280:g0 + 1536] * acc[:, 768:1024]
    qkv = _silu(acc[:, 0:768])
    qn, kn = _gdn_qk_norm(qkv)
    gdn = _gdn_prep([qn[_block_rows(c)] for c in nb], [kn[_block_rows(c)] for c in nb],
                    [qkv[_block_rows(c), 512:768] for c in nb],
                    [p_ref[_block_rows(c), g0 + 1536:g0 + 1792] for c in nb],
                    [p_ref[_block_rows(c), g0 + 1792:g0 + 2048] for c in nb], m, seg)
    gla = [_gla_prep(p_ref[_block_rows(c), P_GLA:P_GLA + 896], m, nseq) for c in nb]
    seqs = lambda c: slice(c * nseq, (c + 1) * nseq)
    sg = [_expand_state(s0gla_ref[seqs(c)].reshape(nseq * GLA_QK, GLA_DV), GLA_DV, GLA_HEADS) * m.bd_gla for c in nb]
    sd = [_expand_state(s0gdn_ref[seqs(c)].reshape(nseq * GDN_QK, GDN_DV), GDN_DV, GDN_HEADS) * m.bd_gdn for c in nb]
    og, sg = _gla_scan(gla, sg)
    od, sd = _gdn_scan(gdn, sd, m, nseq)
    for c in nb:
        rows = _block_rows(c)
        sgla_ref[seqs(c)] = _compact_state(sg[c], GLA_DV, GLA_HEADS).reshape(nseq, GLA_QK, GLA_DV)
        sgdn_ref[seqs(c)] = _compact_state(sd[c], GDN_DV, GDN_HEADS).reshape(nseq, GDN_QK, GDN_DV)
        o_ref[rows, 512:768] = _cm_block(p_ref[rows, P_CM:P_CM + 512], cmw_ref[...], cmb_ref[...], seg)
    o_ref[:, 0:256] = _gla_out(jnp.concatenate(og, axis=0), gn_gla_ref[...], p_ref[:, 512:768])
    o_ref[:, 256:512] = _gdn_out(jnp.concatenate(od, axis=0), gn_gdn_ref[...], p_ref[:, g0 + 1024:g0 + 1280])


def _mixer_sample(p, hist, s0gla, s0gdn, cw, gn_gla, gn_gdn, cmw, cmb, *, layer, base_step, nsteps, seg):
    rows = SAMPLE_STEP_ROWS
    nseq_step = rows // seg
    per_layer = lambda *shape: _layer_spec(shape, layer)
    return pl.pallas_call(
        functools.partial(_mixer_sample_kernel, seg=seg),
        grid=(nsteps,),
        in_specs=[pl.BlockSpec((rows, P_COLS), lambda i: (base_step + i, 0)),
                  pl.BlockSpec((None, rows, CONV_W), lambda i: (layer, i, 0)),
                  pl.BlockSpec((None, nseq_step, GLA_QK, GLA_DV), lambda i: (layer, i, 0, 0)),
                  pl.BlockSpec((None, nseq_step, GDN_QK, GDN_DV), lambda i: (layer, i, 0, 0)),
                  per_layer(GDN_CONV, CONV_W), per_layer(1, 256), per_layer(1, 256),
                  per_layer(ROWS, CM_GROUPS * ROWS), per_layer(ROWS, 256)],
        out_specs=[pl.BlockSpec((rows, 4 * BRANCH_W), lambda i: (i, 0)),
                   pl.BlockSpec((nseq_step, GLA_QK, GLA_DV), lambda i: (i, 0, 0)),
                   pl.BlockSpec((nseq_step, GDN_QK, GDN_DV), lambda i: (i, 0, 0))],
        out_shape=[jax.ShapeDtypeStruct((nsteps * rows, 4 * BRANCH_W), F32),
                   jax.ShapeDtypeStruct((nsteps * nseq_step, GLA_QK, GLA_DV), F32),
                   jax.ShapeDtypeStruct((nsteps * nseq_step, GDN_QK, GDN_DV), F32)],
        compiler_params=pltpu.CompilerParams(dimension_semantics=("parallel",), vmem_limit_bytes=VMEM_LIMIT),
        name="mixer_sample",
    )(p, hist, s0gla, s0gdn, cw, gn_gla, gn_gdn, cmw, cmb)


def _merge_kernel(*refs, prompt_tiles, pair):
    nh = 2 if pair else 1
    bp_ref, bs_ref, nw_ref, wg_ref, wb_ref, wo_ref, o_ref = refs[nh:]
    h = _residual_rows(refs[:nh], prompt_tiles)
    xn = _rms(h, nw_ref[...]).astype(BF16)
    br = _stream_rows(bp_ref, bs_ref, prompt_tiles).astype(BF16)
    merged = None
    for gi in range(N_BRANCH):
        gate = _sigmoid(_dot(xn, wg_ref[:, gi * D_MODEL:(gi + 1) * D_MODEL]))
        term = _dot(br[:, gi * BRANCH_W:(gi + 1) * BRANCH_W], wb_ref[gi]) * gate
        merged = term if merged is None else merged + term
    o_ref[...] = h + _dot(merged.astype(BF16), wo_ref[...])


def _stream_specs(tm, width, prompt_tiles, lead=()):
    nlead = (None,) * len(lead)
    return (pl.BlockSpec(nlead + (tm, width), lambda i: lead + (jnp.minimum(i, prompt_tiles - 1), 0)),
            pl.BlockSpec(nlead + (tm, width), lambda i: lead + (jnp.maximum(i - prompt_tiles, 0), 0)))


def _merge(h, br_p, br_s, nw, wg, wb, wo, *, layer):
    tm = TOKEN_TILE
    h_arrays, h_specs, _, ntok = _residual_specs(h, tm)
    prompt_tiles = br_p.shape[0] // tm
    row = lambda n: pl.BlockSpec((tm, n), lambda i: (i, 0))
    return pl.pallas_call(
        functools.partial(_merge_kernel, prompt_tiles=prompt_tiles, pair=isinstance(h, tuple)),
        grid=(ntok // tm,),
        in_specs=[*h_specs, *_stream_specs(tm, 4 * BRANCH_W, prompt_tiles), _layer_spec((1, D_MODEL), layer),
                  _layer_spec((D_MODEL, N_BRANCH * D_MODEL), layer),
                  _layer_spec((N_BRANCH, BRANCH_W, D_MODEL), layer), _layer_spec((D_MODEL, D_MODEL), layer)],
        out_specs=row(D_MODEL),
        out_shape=jax.ShapeDtypeStruct((ntok, D_MODEL), F32),
        compiler_params=pltpu.CompilerParams(dimension_semantics=("parallel",), vmem_limit_bytes=VMEM_LIMIT),
        name="merge",
    )(*h_arrays, br_p, br_s, nw, wg, wb, wo)


def _ffn_kernel(h_ref, pp_ref, ps_ref, nf_ref, wfg_ref, wfu_ref, wfd_ref, np_ref, wpg_ref, wp_ref, nfin_ref,
                *o_refs, final, prompt_tiles):
    h = h_ref[...]
    xf = _rms(h, nf_ref[...]).astype(BF16)
    act = _silu(_dot(xf, wfg_ref[...])) * _dot(xf, wfu_ref[...])
    h = h + _dot(act.astype(BF16), wfd_ref[...])
    pg = _sigmoid(_dot(_rms(h, np_ref[...]).astype(BF16), wpg_ref[...]))
    pe = _stream_rows(pp_ref, ps_ref, prompt_tiles).astype(BF16)
    h = h + pg * _dot(pe, wp_ref[...])
    if not final:
        o_refs[0][...] = h
        return
    out = _rms(h, nfin_ref[...])
    op_ref, os_ref = o_refs

    @pl.when(pl.program_id(0) < prompt_tiles)
    def _():
        op_ref[...] = out

    @pl.when(pl.program_id(0) >= prompt_tiles)
    def _():
        os_ref[...] = out


def _ffn(h, pe_p, pe_s, nf, wfg, wfu, wfd, npl, wpg, wp, nfin, *, layer, final):
    ntok = h.shape[0]
    tm = TOKEN_TILE
    npt = pe_p.shape[1]
    prompt_tiles = npt // tm
    row = lambda n: pl.BlockSpec((tm, n), lambda i: (i, 0))
    once = lambda shape: _layer_spec(shape, layer, buffers=1)
    if final:
        out_specs = list(_stream_specs(tm, D_MODEL, prompt_tiles))
        out_shape = [jax.ShapeDtypeStruct((npt, D_MODEL), F32), jax.ShapeDtypeStruct((ntok - npt, D_MODEL), F32)]
    else:
        out_specs, out_shape = row(D_MODEL), jax.ShapeDtypeStruct((ntok, D_MODEL), F32)
    return pl.pallas_call(
        functools.partial(_ffn_kernel, final=final, prompt_tiles=prompt_tiles),
        grid=(ntok // tm,),
        in_specs=[row(D_MODEL), *_stream_specs(tm, PLE_DIM, prompt_tiles, lead=(layer,)),
                  _layer_spec((1, D_MODEL), layer),
                  once((D_MODEL, D_FF)), once((D_MODEL, D_FF)), once((D_FF, D_MODEL)),
                  _layer_spec((1, D_MODEL), layer), once((D_MODEL, D_MODEL)), once((PLE_DIM, D_MODEL)),
                  _const_spec((1, D_MODEL))],
        out_specs=out_specs,
        out_shape=out_shape,
        compiler_params=pltpu.CompilerParams(dimension_semantics=("arbitrary",), vmem_limit_bytes=VMEM_LIMIT),
        name="ffn_final" if final else "ffn",
    )(h, pe_p, pe_s, nf, wfg, wfu, wfd, npl, wpg, wp, nfin)


def _split_w_in(w_in):
    span = lambda s: w_in[..., s[0]:s[1]].astype(BF16)
    rep = lambda o: jnp.repeat(w_in[..., o:o + GDN_HEADS], GDN_DK, axis=-1)
    small = jnp.concatenate([w_in[..., W_IN_GA:W_IN_GA + GLA_RANK],
                             jnp.zeros(w_in.shape[:-1] + (LANE - GLA_RANK,), w_in.dtype),
                             rep(W_IN_DA), rep(W_IN_DB)], axis=-1).astype(BF16)
    return span(W_IN_GLA), span(W_IN_GDN), span(W_IN_REST), small


def kernel(x_prompt, x_sample, state_gla, state_gdn, state_gdn_conv, state_sconv, p_prompt, p_sample, norm_mix, w_in, gla_wa2, gla_ba, gla_norm, gdn_conv_w, gdn_a_log, gdn_dt_bias, gdn_norm, cm_ln_g, cm_ln_b, cm_ws, cm_bs, sc_conv_w, w_gate, w_branch, w_o, norm_ffn, w_ffn_gate, w_ffn_up, w_ffn_down, norm_ple, w_ple_gate, w_ple, norm_final):
    depth = w_in.shape[0]
    bp, tp, _ = x_prompt.shape
    bs, ts, _ = x_sample.shape
    npt, nst = bp * tp, bs * ts
    sseq = ROWS // ts
    assert ts == HIST and nst % SAMPLE_STEP_ROWS == 0
    assert bp % PROMPT_SEQS == 0 and tp % PROMPT_SEQ_ROWS == 0 and PROMPT_SEQ_ROWS % CM_CHUNK == 0
    assert npt % TOKEN_TILE == 0 and nst % TOKEN_TILE == 0

    h = (x_prompt.reshape(npt, D_MODEL), x_sample.reshape(nst, D_MODEL))
    pe_p = p_prompt.reshape(depth, npt, PLE_DIM)
    pe_s = p_sample.reshape(depth, nst, PLE_DIM)
    s0_gla = state_gla.reshape(depth, bs, GLA_QK, GLA_DV)
    s0_gdn = state_gdn.reshape(depth, bs, GDN_QK, GDN_DV)
    rows = lambda a: a.reshape(depth, 1, -1)
    w_in_groups = _split_w_in(w_in)
    wg, wb, wo = w_gate.astype(BF16), w_branch.astype(BF16), w_o.astype(BF16)
    wfg, wfu, wfd = w_ffn_gate.astype(BF16), w_ffn_up.astype(BF16), w_ffn_down.astype(BF16)
    wpg, wp = w_ple_gate.astype(BF16), w_ple.astype(BF16)
    wa2 = jnp.pad(gla_wa2, ((0, 0), (0, LANE - GLA_RANK), (0, 0))).astype(BF16)
    nmix, nffn, nple = rows(norm_mix), rows(norm_ffn), rows(norm_ple)
    inproj_vecs = (rows(gla_ba), rows(jnp.repeat(gdn_a_log, GDN_DK, axis=1)),
                   rows(jnp.repeat(gdn_dt_bias, GDN_DK, axis=1)), rows(cm_ln_g), rows(cm_ln_b))
    cw = jnp.concatenate([gdn_conv_w, jnp.pad(sc_conv_w, ((0, 0), (GDN_CONV - SC_WIDTH, 0), (0, 0)))], axis=2)
    hist_s = jnp.concatenate([
        jnp.pad(state_gdn_conv, ((0, 0), (0, 0), (HIST - (GDN_CONV - 1), 0), (0, 0))),
        jnp.pad(state_sconv, ((0, 0), (0, 0), (HIST - (SC_WIDTH - 1), 0), (0, 0)))], axis=3).reshape(depth, nst, CONV_W)
    gn_gla = rows(jnp.tile(gla_norm, (1, GLA_HEADS)))
    gn_gdn = rows(jnp.tile(gdn_norm, (1, GDN_HEADS)))
    cmw_p = jnp.transpose(cm_ws, (0, 2, 1, 3)).reshape(depth, CM_CHUNK, CM_GROUPS * CM_CHUNK)
    cmb_p = jnp.repeat(jnp.swapaxes(cm_bs, 1, 2), BRANCH_W // CM_GROUPS, axis=2)
    cmw_s = jnp.transpose(jnp.tile(cm_ws[:, :, :ts, :ts], (1, 1, sseq, sseq)),
                          (0, 2, 1, 3)).reshape(depth, ROWS, CM_GROUPS * ROWS)
    cmb_s = jnp.tile(cmb_p[:, :ts], (1, sseq, 1))

    outs = {k: [] for k in ("gla_p", "gla_s", "gdn_p", "gdn_s", "gc_p", "gc_s", "sc_p", "sc_s", "cv_s")}
    for i in range(depth):
        p = _inproj(h, nmix, *w_in_groups, wa2, *inproj_vecs, layer=i)
        br_p, gla_p, gdn_p, tail_p = _mixer_prompt(p, cw, gn_gla, gn_gdn, cmw_p, cmb_p, layer=i, nseqs=bp, seq_len=tp)
        br_p = br_p.reshape(npt, N_BRANCH * BRANCH_W)
        br_s, gla_s, gdn_s = _mixer_sample(p, hist_s, s0_gla, s0_gdn, cw, gn_gla, gn_gdn, cmw_s, cmb_s, layer=i,
                                           base_step=npt // SAMPLE_STEP_ROWS, nsteps=nst // SAMPLE_STEP_ROWS, seg=ts)
        outs["gla_p"].append(gla_p.reshape(bp, GLA_HEADS, GLA_DK, GLA_DV))
        outs["gla_s"].append(gla_s.reshape(bs, GLA_HEADS, GLA_DK, GLA_DV))
        outs["gdn_p"].append(gdn_p.reshape(bp, GDN_HEADS, GDN_DK, GDN_DV))
        outs["gdn_s"].append(gdn_s.reshape(bs, GDN_HEADS, GDN_DK, GDN_DV))
        xs3 = p[npt:, P_GDN:P_GDN + CONV_W].reshape(bs, ts, CONV_W)
        outs["gc_p"].append(tail_p[:, HIST - (GDN_CONV - 1):, 0:768])
        outs["gc_s"].append(xs3[:, ts - (GDN_CONV - 1):, 0:768])
        outs["sc_p"].append(tail_p[:, HIST - (SC_WIDTH - 1):, 768:])
        outs["sc_s"].append(xs3[:, ts - (SC_WIDTH - 1):, 768:])
        outs["cv_s"].append(p[npt:, P_CM + 256:P_CM + 512].reshape(bs, ts, BRANCH_W))

        h1 = _merge(h, br_p, br_s, nmix, wg, wb, wo, layer=i)
        h = _ffn(h1, pe_p, pe_s, nffn, wfg, wfu, wfd, nple, wpg, wp, norm_final.reshape(1, D_MODEL),
                 layer=i, final=(i == depth - 1))

    y_prompt = h[0].reshape(bp, tp, D_MODEL)
    y_sample = h[1].reshape(bs, ts, D_MODEL)
    st = lambda k: jnp.stack(outs[k])
    return (y_prompt, y_sample, st("gla_p"), st("gla_s"), st("gdn_p"), st("gdn_s"),
            st("gc_p"), st("gc_s"), st("sc_p"), st("sc_s"), st("cv_s"))
```

```python
import functools

import jax
import jax.numpy as jnp
from jax import lax
from jax.experimental import pallas as pl
from jax.experimental.pallas import tpu as pltpu

F32 = jnp.float32
BF16 = jnp.bfloat16

D_MODEL = 1024
PLE_DIM = 256
BRANCH_W = 256
N_BRANCH = 4
GLA_HEADS = 4
GLA_DK = 32
GLA_DV = 64
GLA_RANK = 16
GLA_TAU = 16.0
GDN_HEADS = 4
GDN_DK = 64
GDN_DV = 64
GDN_CONV = 4
CM_GROUPS = 4
CM_CHUNK = 128
SC_WIDTH = 3
D_FF = 2816
EPS = 1e-6

ROWS = 64
PROMPT_SEQS = 4
PROMPT_CHUNKS = 2
PROMPT_SEQ_ROWS = ROWS * PROMPT_CHUNKS
SAMPLE_BLOCKS_PER_STEP = 2
SAMPLE_STEP_ROWS = ROWS * SAMPLE_BLOCKS_PER_STEP
GLA_QK = GLA_HEADS * GLA_DK
GDN_QK = GDN_HEADS * GDN_DK
LANE = 128
CONV_W = 3 * BRANCH_W + BRANCH_W
HIST = 8
SUM_PIECES = 2

W_IN_GLA, W_IN_GDN, W_IN_REST = (0, 768), (784, 1808), (1816, 3096)
W_IN_GA, W_IN_DA, W_IN_DB = 768, 1808, 1812
SMALL_COLS = LANE + 2 * 256
P_GLA = 0
P_GDN = 896
P_CM = 2688
P_COLS = 3200

VMEM_LIMIT = 56 * 1024 * 1024
TOKEN_TILE = 512


def _dot(a, b):
    return jnp.dot(a, b, preferred_element_type=F32)


def _dot_nt(a, b):
    return lax.dot_general(a, b, (((1,), (1,)), ((), ())), preferred_element_type=F32)


def _dot_tn(a, b):
    return lax.dot_general(a, b, (((0,), (0,)), ((), ())), preferred_element_type=F32)


def _split(x, n):
    parts, r = [], x
    for i in range(n):
        p = r.astype(BF16)
        parts.append(p)
        if i + 1 < n:
            r = r - p.astype(F32)
    return parts


def _dot1(a, b, dot=_dot):
    return dot(a.astype(BF16), b.astype(BF16))


def _dot3(a, b_pieces):
    ah, al = _split(a, 2)
    bh, bl = b_pieces
    return _dot(jnp.concatenate([ah, ah, al], axis=1), jnp.concatenate([bh, bl, bh], axis=0))


def _mask_dot(mask, x, n):
    return _dot(jnp.concatenate([mask.astype(BF16)] * n, axis=1), jnp.concatenate(_split(x, n), axis=0))


def _mask_dot_nt(mask, x, n):
    return _dot_nt(jnp.concatenate([mask.astype(BF16)] * n, axis=1), jnp.concatenate(_split(x, n), axis=1))


def _mask_dot_rhs(x, mask, n):
    return _dot(jnp.concatenate(_split(x, n), axis=1), jnp.concatenate([mask.astype(BF16)] * n, axis=0))


def _sigmoid(x):
    return 1.0 / (1.0 + jnp.exp(-x))


def _silu(x):
    return x * _sigmoid(x)


def _softplus(x):
    return jnp.maximum(x, 0.0) + jnp.log1p(jnp.exp(-jnp.abs(x)))


def _gelu_tanh(x):
    return 0.5 * x * (1.0 + jnp.tanh(0.7978845608028654 * (x + 0.044715 * (x * x * x))))


def _rms(x, w):
    return x * lax.rsqrt(jnp.mean(x * x, axis=-1, keepdims=True) + EPS) * w


def _idiv(x, n):
    assert n & (n - 1) == 0
    return lax.shift_right_logical(x, n.bit_length() - 1)


def _imod(x, n):
    assert n & (n - 1) == 0
    return lax.bitwise_and(x, n - 1)


def _const_spec(shape):
    return pl.BlockSpec(shape, lambda *_: (0,) * len(shape))


def _layer_spec(shape, layer, buffers=None):
    mode = {} if buffers is None else {"pipeline_mode": pl.Buffered(buffers)}
    return pl.BlockSpec((None,) + shape, lambda *_: (layer,) + (0,) * len(shape), **mode)


def _iota2(shape):
    return lax.broadcasted_iota(jnp.int32, shape, 0), lax.broadcasted_iota(jnp.int32, shape, 1)


def _stream_rows(hp_ref, hs_ref, prompt_tiles):
    return jnp.where(pl.program_id(0) < prompt_tiles, hp_ref[...], hs_ref[...])


def _residual_rows(h_refs, prompt_tiles):
    return h_refs[0][...] if len(h_refs) == 1 else _stream_rows(*h_refs, prompt_tiles)


def _residual_specs(h, tm):
    if isinstance(h, tuple):
        prompt_tiles = h[0].shape[0] // tm
        return list(h), list(_stream_specs(tm, D_MODEL, prompt_tiles)), prompt_tiles, h[0].shape[0] + h[1].shape[0]
    return [h], [pl.BlockSpec((tm, D_MODEL), lambda i: (i, 0))], None, h.shape[0]


def _inproj_kernel(*refs, sample, tiles_per_seq, seg):
    if sample:
        (h_ref, hist_ref, nw_ref, wgla_ref, wgdn_ref, wrest_ref, wsmall_ref, wa2_ref, ba_ref, alog_ref,
         dtb_ref, lng_ref, lnb_ref, cw_ref, p_ref, xraw_ref) = refs
    else:
        (h_ref, nw_ref, wgla_ref, wgdn_ref, wrest_ref, wsmall_ref, wa2_ref, ba_ref, alog_ref,
         dtb_ref, lng_ref, lnb_ref, cw_ref, p_ref, xraw_ref, carry) = refs
    tm = p_ref.shape[0]
    xn = _rms(h_ref[...], nw_ref[...]).astype(BF16)
    pd = _dot(xn, wgdn_ref[...])
    pr = _dot(xn, wrest_ref[...])
    pg = _dot(xn, wgla_ref[...])
    ps = _dot(xn, wsmall_ref[...])
    g0 = P_GDN
    x = jnp.concatenate([pd[:, 0:768], pr[:, 1024:1280] * pr[:, 512:768]], axis=1)
    taps = _conv_taps(cw_ref)
    if sample:
        acc = _conv_sample(x, hist_ref[...], taps, seg)
        xraw_ref[...] = x
    else:
        hist = jnp.where(lax.rem(pl.program_id(0), tiles_per_seq) == 0, 0.0, carry[...])
        acc = _conv_prompt(x, hist, taps)
        carry[...] = x[tm - HIST:tm]
        xraw_ref[...] = x[tm - HIST:tm]
    p_ref[:, g0:g0 + 768] = _silu(acc[:, 0:768])
    p_ref[:, g0 + 768:g0 + 1024] = pr[:, 768:1024] * acc[:, 768:1024]
    p_ref[:, g0 + 1024:g0 + 1280] = _silu(pd[:, 768:1024])
    p_ref[:, 0:128] = pg[:, 0:128] * (GLA_DK ** -0.5)
    p_ref[:, 128:512] = pg[:, 128:512]
    p_ref[:, 512:768] = _silu(pg[:, 512:768])
    za = _dot(ps[:, 0:LANE].astype(BF16), wa2_ref[...]) + ba_ref[...]
    p_ref[:, 768:896] = -_softplus(-za) * (1.0 / GLA_TAU)
    p_ref[:, g0 + 1280:g0 + 1536] = -jnp.exp(alog_ref[...]) * _softplus(ps[:, LANE:LANE + 256] + dtb_ref[...])
    p_ref[:, g0 + 1536:g0 + 1792] = _sigmoid(ps[:, LANE + 256:LANE + 512])
    p_ref[:, P_CM:P_CM + 256] = _gelu_tanh(pr[:, 0:256])
    gv = _gelu_tanh(pr[:, 256:512])
    mu = jnp.mean(gv, axis=-1, keepdims=True)
    d = gv - mu
    var = jnp.mean(d * d, axis=-1, keepdims=True)
    p_ref[:, P_CM + 256:P_CM + 512] = d * lax.rsqrt(var + EPS) * lng_ref[...] + lnb_ref[...]


def _inproj(h, first_tile, rows, hist_s, nw, wgla, wgdn, wrest, wsmall, wa2, ba, alog, dtb, lng, lnb, cw, *,
            layer, sample, seq_len, seg):
    tm = TOKEN_TILE
    ntiles = rows // tm
    cols = lambda span: span[1] - span[0]
    per_layer = lambda *shape: _layer_spec(shape, layer)
    tile = lambda n: pl.BlockSpec((tm, n), lambda i: (i, 0))
    if sample:
        hist = [hist_s]
        hist_spec = [pl.BlockSpec((None, tm, CONV_W), lambda i: (layer, i, 0))]
        raw_spec, raw_shape, scratch = tile(CONV_W), (rows, CONV_W), []
    else:
        hist, hist_spec = [], []
        raw_spec, raw_shape = pl.BlockSpec((None, HIST, CONV_W), lambda i: (i, 0, 0)), (ntiles, HIST, CONV_W)
        scratch = [pltpu.VMEM((HIST, CONV_W), F32)]
    return pl.pallas_call(
        functools.partial(_inproj_kernel, sample=sample, tiles_per_seq=seq_len // tm, seg=seg),
        grid=(ntiles,),
        in_specs=[pl.BlockSpec((tm, D_MODEL), lambda i: (first_tile + i, 0)), *hist_spec, per_layer(1, D_MODEL),
                  per_layer(D_MODEL, cols(W_IN_GLA)), per_layer(D_MODEL, cols(W_IN_GDN)),
                  per_layer(D_MODEL, cols(W_IN_REST)), per_layer(D_MODEL, SMALL_COLS),
                  per_layer(LANE, GLA_QK), per_layer(1, GLA_QK), per_layer(1, 256),
                  per_layer(1, 256), per_layer(1, 256), per_layer(1, 256), per_layer(GDN_CONV, CONV_W)],
        out_specs=[tile(P_COLS), raw_spec],
        out_shape=[jax.ShapeDtypeStruct((rows, P_COLS), F32), jax.ShapeDtypeStruct(raw_shape, F32)],
        scratch_shapes=scratch,
        compiler_params=pltpu.CompilerParams(dimension_semantics=("arbitrary",), vmem_limit_bytes=VMEM_LIMIT),
        name="inproj_sample" if sample else "inproj_prompt",
    )(h, *hist, nw, wgla, wgdn, wrest, wsmall, wa2, ba, alog, dtb, lng, lnb, cw)


def _stack_heads(x, group, nheads, period=None):
    w = x.shape[1]
    li = lax.broadcasted_iota(jnp.int32, (1, w), 1)
    if period is not None:
        li = _imod(li, period)
    hid = _idiv(li, group)
    zero = jnp.zeros_like(x)
    return jnp.concatenate([jnp.where(hid == h, x, zero) for h in range(nheads)], axis=0)


def _block_diag(x, nblocks):
    r = x.shape[0]
    ri, ci = _iota2((nblocks * r, nblocks * r))
    return jnp.where(_idiv(ri, r) == _idiv(ci, r), jnp.concatenate([x] * nblocks, axis=0), jnp.zeros((), x.dtype))


def _widen(x, nseq):
    if nseq == 1:
        return x
    seg = ROWS // nseq
    sid = _idiv(lax.broadcasted_iota(jnp.int32, (ROWS, 1), 0), seg)
    zero = jnp.zeros_like(x)
    return jnp.concatenate([jnp.where(sid == j, x, zero) for j in range(nseq)], axis=1)


def _group_mean(x, group):
    w = x.shape[1]
    ri, ci = _iota2((w, w))
    avg = jnp.where(_idiv(ri, group) == _idiv(ci, group), 1.0 / group, 0.0).astype(BF16)
    return _dot(x.astype(BF16), avg)


def _head_diag_mask(rows, cols, rhead, chead, rper):
    ri, ci = _iota2((rows, cols))
    return (_idiv(_imod(ri, rper), rhead) == _idiv(ci, chead)).astype(F32)


def _expand_state(s, width, reps):
    ri, ci = _iota2((width, reps * width))
    return _mask_dot_rhs(s, ri == _imod(ci, width), 3)


def _compact_state(st, width, reps):
    out = st[:, 0:width]
    for h in range(1, reps):
        out = out + st[:, h * width:(h + 1) * width]
    return out


class _Masks:
    def __init__(self, seg):
        ri, ci = _iota2((ROWS, ROWS))
        self.same = _idiv(ri, seg) == _idiv(ci, seg)
        self.tri = self.same & (ci <= ri)
        rl, cl = _iota2((ROWS, GDN_HEADS * ROWS))
        cl = _imod(cl, ROWS)
        same_l = _idiv(rl, seg) == _idiv(cl, seg)
        self.tri_l = same_l & (cl <= rl)
        self.strict_l = same_l & (cl < rl)
        self.eye_l = (rl == cl).astype(F32)
        self.levels = [(_idiv(rl, 2 * s) == _idiv(cl, 2 * s)) & (_idiv(rl, s) != _idiv(cl, s))
                       for s in (1 << k for k in range(seg.bit_length() - 1))]
        self.ones = jnp.ones((ROWS, ROWS), BF16)
        self.seg = seg
        nseq = ROWS // seg
        self.bd_gla = _head_diag_mask(nseq * GLA_QK, GLA_HEADS * GLA_DV, GLA_DK, GLA_DV, GLA_QK)
        self.bd_gdn = _head_diag_mask(nseq * GDN_QK, GDN_QK, GDN_DK, GDN_DV, GDN_QK)


def _gla_prep(p, m, nseq):
    q, k, v, la = p[:, 0:128], p[:, 128:256], p[:, 256:512], p[:, 768:896]
    b = _mask_dot(m.tri, la, SUM_PIECES)
    if m.seg == ROWS:
        btot = jnp.broadcast_to(b[ROWS - 1:ROWS], b.shape)
    else:
        btot = _mask_dot(m.same, la, SUM_PIECES)
    qd = q * jnp.exp(b)
    kd = k * jnp.exp(-b)
    ke = k * jnp.exp(btot - b)
    a = jnp.where(m.tri_l, _dot1(qd, _stack_heads(kd, GLA_DK, GLA_HEADS), _dot_nt), 0.0)
    o_intra = _dot1(a, _stack_heads(v, GLA_DV, GLA_HEADS))
    dec = jnp.exp(_dot_tn(jnp.concatenate(_split(_widen(la, nseq), SUM_PIECES), axis=0),
                          jnp.ones((SUM_PIECES * ROWS, GLA_HEADS * GLA_DV), BF16)))
    ds = _dot1(_widen(ke, nseq), v, _dot_tn) * m.bd_gla
    return o_intra, _widen(qd, nseq), dec, ds


def _gla_scan(preps, sts):
    outs = [prep[0] + _dot1(prep[1], st) for prep, st in zip(preps, sts)]
    return outs, [st * prep[2] + prep[3] for prep, st in zip(preps, sts)]


def _gla_out(o, gn, rs):
    return o * lax.rsqrt(_group_mean(o * o, GLA_DV) + EPS) * gn * rs


def _gdn_qk_norm(qkv):
    hd = GDN_DK
    r = qkv.shape[0]
    cqk = jnp.concatenate([qkv[:, 0:256], qkv[:, 256:512]], axis=0)
    nrm = lax.rsqrt(_group_mean(cqk * cqk, hd) * hd + EPS)
    return qkv[:, 0:256] * nrm[0:r] * (hd ** -0.5), qkv[:, 256:512] * nrm[r:2 * r]


def _gdn_prep(q, k, cv, gs, betas, m, seg):
    nh, hd = GDN_HEADS, GDN_DK
    n = range(len(q))
    gc = [_mask_dot(m.tri, gs[i], SUM_PIECES) for i in n]
    if seg == ROWS:
        gtot = [jnp.broadcast_to(gc[i][ROWS - 1:ROWS], gc[i].shape) for i in n]
    else:
        gtot = [_mask_dot(m.same, gs[i], SUM_PIECES) for i in n]
    qkk = [_dot1(jnp.concatenate([q[i], k[i]], axis=0), _stack_heads(k[i], hd, nh), _dot_nt) for i in n]
    grow = [_mask_dot(m.ones, gc[i] * m.eye_l, SUM_PIECES) for i in n]
    decay = [jnp.where(m.tri_l, jnp.exp(jnp.where(m.tri_l, gc[i] - grow[i], 0.0)), 0.0) for i in n]
    amat = [jnp.where(m.strict_l, betas[i] * decay[i] * qkk[i][ROWS:2 * ROWS], 0.0) for i in n]
    inv = [m.eye_l - jnp.where(m.levels[0], amat[i], 0.0) for i in n]
    for lvl in m.levels[1:]:
        low = [_block_diag(jnp.where(lvl, amat[i], 0.0).astype(BF16), nh) for i in n]
        prod = [_dot(inv[i].astype(BF16), low[i]) for i in n]
        inv = [inv[i] - _dot(prod[i].astype(BF16), _block_diag(inv[i].astype(BF16), nh)) for i in n]
    eg = [jnp.exp(gc[i]) for i in n]
    rhs = [jnp.concatenate([betas[i] * cv[i], betas[i] * eg[i] * k[i]], axis=1) for i in n]
    uw = [_dot(inv[i].astype(BF16), _stack_heads(rhs[i].astype(BF16), hd, nh, period=GDN_QK)) for i in n]
    return [(uw[i][:, 0:256], uw[i][:, 256:512], q[i] * eg[i], qkk[i][0:ROWS] * decay[i],
             k[i] * jnp.exp(gtot[i] - gc[i]), jnp.exp(gtot[i])) for i in n]


def _gdn_scan(preps, sts, m, nseq):
    seg = ROWS // nseq
    n = range(len(preps))
    ws = [_dot1(jnp.concatenate([_widen(preps[i][1], nseq), _widen(preps[i][2], nseq)], axis=0), sts[i]) for i in n]
    u = [preps[i][0] - ws[i][0:ROWS] for i in n]
    outs = [ws[i][ROWS:2 * ROWS] + _dot1(preps[i][3], _stack_heads(u[i], GDN_DV, GDN_HEADS)) for i in n]
    new = []
    for i in n:
        dn = preps[i][5]
        dn_tall = jnp.concatenate(
            [jnp.broadcast_to(dn[j * seg:j * seg + 1], (GDN_QK, GDN_QK)) for j in range(nseq)], axis=0)
        new.append(sts[i] * dn_tall + _dot1(_widen(preps[i][4], nseq), u[i], _dot_tn) * m.bd_gdn)
    return outs, new


def _gdn_out(o, gn, zs):
    return o * lax.rsqrt(_group_mean(o * o, GDN_DV) + EPS) * gn * zs


def _cm_block(p, ws, bias, seg):
    r = p.shape[0]
    gu, vn = p[:, 0:256], p[:, 256:512]
    ri, ci = _iota2((r, CM_GROUPS * r))
    ci = _imod(ci, r)
    wm = jnp.where((_idiv(ri, seg) == _idiv(ci, seg)) & (ci <= ri), ws, 0.0)
    return gu * (_dot1(wm, _stack_heads(vn, BRANCH_W // CM_GROUPS, CM_GROUPS)) + bias)


def _conv_taps(cw_ref):
    return [cw_ref[GDN_CONV - 1 - d:GDN_CONV - d, :] for d in range(GDN_CONV)]


def _conv_prompt(x, hist, taps):
    n = x.shape[0]
    t8 = lax.broadcasted_iota(jnp.int32, (HIST, 1), 0)
    acc = taps[0] * x
    for d in range(1, GDN_CONV):
        xr = pltpu.roll(x, d, 0)
        head = jnp.where(t8 < d, pltpu.roll(hist, d, 0), xr[0:HIST])
        acc = acc + taps[d] * jnp.concatenate([head, xr[HIST:n]], axis=0)
    return acc


def _conv_sample(x, hist, taps, seg):
    n = x.shape[0]
    tloc = _imod(lax.broadcasted_iota(jnp.int32, (n, 1), 0), seg)
    acc = taps[0] * x
    for d in range(1, GDN_CONV):
        prev = jnp.where(tloc < d, pltpu.roll(hist, (d - seg) % n, 0), pltpu.roll(x, d, 0))
        acc = acc + taps[d] * prev
    return acc


def _block_rows(c):
    return slice(c * ROWS, (c + 1) * ROWS)


def _mixer_prompt_kernel(*refs):
    ns, nc, sr = PROMPT_SEQS, PROMPT_CHUNKS, PROMPT_SEQ_ROWS
    p_refs = refs[:ns]
    gn_gla_ref, gn_gdn_ref, cmw_ref, cmb_ref, o_ref, sgla_ref, sgdn_ref, st_gla, st_gdn = refs[ns:]

    @pl.when(pl.program_id(1) == 0)
    def _():
        st_gla[...] = jnp.zeros_like(st_gla)
        st_gdn[...] = jnp.zeros_like(st_gdn)

    m = _Masks(ROWS)
    g0 = P_GDN
    for s in range(ns):
        o_ref[s, :, 768:1024] = p_refs[s][:, g0 + 768:g0 + 1024]
    qkv = jnp.concatenate([p_refs[s][:, g0:g0 + 768] for s in range(ns)], axis=0)
    qn, kn = _gdn_qk_norm(qkv)
    blocks = [(s, c) for c in range(nc) for s in range(ns)]
    rows_of = lambda s, c: slice(s * sr + c * ROWS, s * sr + (c + 1) * ROWS)
    gdn = _gdn_prep([qn[rows_of(s, c)] for s, c in blocks], [kn[rows_of(s, c)] for s, c in blocks],
                    [qkv[rows_of(s, c), 512:768] for s, c in blocks],
                    [p_refs[s][_block_rows(c), g0 + 1280:g0 + 1536] for s, c in blocks],
                    [p_refs[s][_block_rows(c), g0 + 1536:g0 + 1792] for s, c in blocks], m, ROWS)
    gla = [_gla_prep(p_refs[s][_block_rows(c), P_GLA:P_GLA + 896], m, 1) for s, c in blocks]
    sg = [st_gla[s] for s in range(ns)]
    sd = [st_gdn[s] for s in range(ns)]
    og, od = {}, {}
    for c in range(nc):
        o, sg = _gla_scan(gla[c * ns:(c + 1) * ns], sg)
        og.update({(s, c): o[s] for s in range(ns)})
        o, sd = _gdn_scan(gdn[c * ns:(c + 1) * ns], sd, m, 1)
        od.update({(s, c): o[s] for s in range(ns)})
    by_rows = lambda d: jnp.concatenate([d[(s, c)] for s in range(ns) for c in range(nc)], axis=0)
    rs = jnp.concatenate([p_refs[s][:, 512:768] for s in range(ns)], axis=0)
    zs = jnp.concatenate([p_refs[s][:, g0 + 1024:g0 + 1280] for s in range(ns)], axis=0)
    o_gla = _gla_out(by_rows(og), gn_gla_ref[...], rs)
    o_gdn = _gdn_out(by_rows(od), gn_gdn_ref[...], zs)
    for s in range(ns):
        st_gla[s] = sg[s]
        st_gdn[s] = sd[s]
        o_ref[s, :, 0:256] = o_gla[s * sr:(s + 1) * sr]
        o_ref[s, :, 256:512] = o_gdn[s * sr:(s + 1) * sr]
        for c in range(sr // CM_CHUNK):
            rows = slice(c * CM_CHUNK, (c + 1) * CM_CHUNK)
            o_ref[s, rows, 512:768] = _cm_block(p_refs[s][rows, P_CM:P_CM + 512], cmw_ref[...], cmb_ref[...], CM_CHUNK)

    @pl.when(pl.program_id(1) == pl.num_programs(1) - 1)
    def _():
        for s in range(ns):
            sgla_ref[s] = _compact_state(sg[s], GLA_DV, GLA_HEADS)
            sgdn_ref[s] = _compact_state(sd[s], GDN_DV, GDN_HEADS)


def _mixer_prompt(p, gn_gla, gn_gdn, cmw, cmb, *, layer, nseqs, seq_len):
    ns, sr = PROMPT_SEQS, PROMPT_SEQ_ROWS
    nsteps = seq_len // sr
    seq_spec = lambda s: pl.BlockSpec((sr, P_COLS), lambda o, c: ((o * ns + s) * nsteps + c, 0))
    per_group = lambda *shape: pl.BlockSpec((ns,) + shape, lambda o, c: (o,) + (0,) * len(shape))
    per_layer = lambda *shape: _layer_spec(shape, layer)
    return pl.pallas_call(
        _mixer_prompt_kernel,
        grid=(nseqs // ns, nsteps),
        in_specs=[seq_spec(s) for s in range(ns)] + [
            per_layer(1, 256), per_layer(1, 256),
            per_layer(CM_CHUNK, CM_GROUPS * CM_CHUNK), per_layer(CM_CHUNK, 256)],
        out_specs=[pl.BlockSpec((ns, sr, 4 * BRANCH_W), lambda o, c: (o, c, 0)),
                   per_group(GLA_QK, GLA_DV), per_group(GDN_QK, GDN_DV)],
        out_shape=[jax.ShapeDtypeStruct((nseqs, seq_len, 4 * BRANCH_W), F32),
                   jax.ShapeDtypeStruct((nseqs, GLA_QK, GLA_DV), F32),
                   jax.ShapeDtypeStruct((nseqs, GDN_QK, GDN_DV), F32)],
        scratch_shapes=[pltpu.VMEM((ns, GLA_QK, GLA_HEADS * GLA_DV), F32), pltpu.VMEM((ns, GDN_QK, GDN_QK), F32)],
        compiler_params=pltpu.CompilerParams(dimension_semantics=("arbitrary", "arbitrary"),
                                             vmem_limit_bytes=VMEM_LIMIT),
        name="mixer_prompt",
    )(*([p] * ns), gn_gla, gn_gdn, cmw, cmb)


def _mixer_sample_kernel(p_ref, s0gla_ref, s0gdn_ref, gn_gla_ref, gn_gdn_ref, cmw_ref, cmb_ref,
                         o_ref, sgla_ref, sgdn_ref, *, seg):
    nseq = ROWS // seg
    nb = range(SAMPLE_BLOCKS_PER_STEP)
    m = _Masks(seg)
    g0 = P_GDN
    o_ref[:, 768:1024] = p_ref[:, g0 + 768:g0 + 1024]
    qkv = p_ref[:, g0:g0 + 768]
    qn, kn = _gdn_qk_norm(qkv)
    gdn = _gdn_prep([qn[_block_rows(c)] for c in nb], [kn[_block_rows(c)] for c in nb],
                    [qkv[_block_rows(c), 512:768] for c in nb],
                    [p_ref[_block_rows(c), g0 + 1280:g0 + 1536] for c in nb],
                    [p_ref[_block_rows(c), g0 + 1536:g0 + 1792] for c in nb], m, seg)
    gla = [_gla_prep(p_ref[_block_rows(c), P_GLA:P_GLA + 896], m, nseq) for c in nb]
    seqs = lambda c: slice(c * nseq, (c + 1) * nseq)
    sg = [_expand_state(s0gla_ref[seqs(c)].reshape(nseq * GLA_QK, GLA_DV), GLA_DV, GLA_HEADS) * m.bd_gla for c in nb]
    sd = [_expand_state(s0gdn_ref[seqs(c)].reshape(nseq * GDN_QK, GDN_DV), GDN_DV, GDN_HEADS) * m.bd_gdn for c in nb]
    og, sg = _gla_scan(gla, sg)
    od, sd = _gdn_scan(gdn, sd, m, nseq)
    for c in nb:
        rows = _block_rows(c)
        sgla_ref[seqs(c)] = _compact_state(sg[c], GLA_DV, GLA_HEADS).reshape(nseq, GLA_QK, GLA_DV)
        sgdn_ref[seqs(c)] = _compact_state(sd[c], GDN_DV, GDN_HEADS).reshape(nseq, GDN_QK, GDN_DV)
        o_ref[rows, 512:768] = _cm_block(p_ref[rows, P_CM:P_CM + 512], cmw_ref[...], cmb_ref[...], seg)
    o_ref[:, 0:256] = _gla_out(jnp.concatenate(og, axis=0), gn_gla_ref[...], p_ref[:, 512:768])
    o_ref[:, 256:512] = _gdn_out(jnp.concatenate(od, axis=0), gn_gdn_ref[...], p_ref[:, g0 + 1024:g0 + 1280])


def _mixer_sample(p, s0gla, s0gdn, gn_gla, gn_gdn, cmw, cmb, *, layer, base_step, nsteps, seg):
    rows = SAMPLE_STEP_ROWS
    nseq_step = rows // seg
    per_layer = lambda *shape: _layer_spec(shape, layer)
    return pl.pallas_call(
        functools.partial(_mixer_sample_kernel, seg=seg),
        grid=(nsteps,),
        in_specs=[pl.BlockSpec((rows, P_COLS), lambda i: (base_step + i, 0)),
                  pl.BlockSpec((None, nseq_step, GLA_QK, GLA_DV), lambda i: (layer, i, 0, 0)),
                  pl.BlockSpec((None, nseq_step, GDN_QK, GDN_DV), lambda i: (layer, i, 0, 0)),
                  per_layer(1, 256), per_layer(1, 256),
                  per_layer(ROWS, CM_GROUPS * ROWS), per_layer(ROWS, 256)],
        out_specs=[pl.BlockSpec((rows, 4 * BRANCH_W), lambda i: (i, 0)),
                   pl.BlockSpec((nseq_step, GLA_QK, GLA_DV), lambda i: (i, 0, 0)),
                   pl.BlockSpec((nseq_step, GDN_QK, GDN_DV), lambda i: (i, 0, 0))],
        out_shape=[jax.ShapeDtypeStruct((nsteps * rows, 4 * BRANCH_W), F32),
                   jax.ShapeDtypeStruct((nsteps * nseq_step, GLA_QK, GLA_DV), F32),
                   jax.ShapeDtypeStruct((nsteps * nseq_step, GDN_QK, GDN_DV), F32)],
        compiler_params=pltpu.CompilerParams(dimension_semantics=("parallel",), vmem_limit_bytes=VMEM_LIMIT),
        name="mixer_sample",
    )(p, s0gla, s0gdn, gn_gla, gn_gdn, cmw, cmb)


def _merge_kernel(*refs, prompt_tiles, pair):
    nh = 2 if pair else 1
    bp_ref, bs_ref, nw_ref, wg_ref, wb_ref, wo_ref, o_ref = refs[nh:]
    h = _residual_rows(refs[:nh], prompt_tiles)
    xn = _rms(h, nw_ref[...]).astype(BF16)
    br = _stream_rows(bp_ref, bs_ref, prompt_tiles).astype(BF16)
    merged = None
    for gi in range(N_BRANCH):
        gate = _sigmoid(_dot(xn, wg_ref[:, gi * D_MODEL:(gi + 1) * D_MODEL]))
        term = _dot(br[:, gi * BRANCH_W:(gi + 1) * BRANCH_W], wb_ref[gi]) * gate
        merged = term if merged is None else merged + term
    o_ref[...] = h + _dot(merged.astype(BF16), wo_ref[...])


def _stream_specs(tm, width, prompt_tiles, lead=()):
    nlead = (None,) * len(lead)
    return (pl.BlockSpec(nlead + (tm, width), lambda i: lead + (jnp.minimum(i, prompt_tiles - 1), 0)),
            pl.BlockSpec(nlead + (tm, width), lambda i: lead + (jnp.maximum(i - prompt_tiles, 0), 0)))


def _merge(h, br_p, br_s, nw, wg, wb, wo, *, layer):
    tm = TOKEN_TILE
    h_arrays, h_specs, _, ntok = _residual_specs(h, tm)
    prompt_tiles = br_p.shape[0] // tm
    row = lambda n: pl.BlockSpec((tm, n), lambda i: (i, 0))
    return pl.pallas_call(
        functools.partial(_merge_kernel, prompt_tiles=prompt_tiles, pair=isinstance(h, tuple)),
        grid=(ntok // tm,),
        in_specs=[*h_specs, *_stream_specs(tm, 4 * BRANCH_W, prompt_tiles), _layer_spec((1, D_MODEL), layer),
                  _layer_spec((D_MODEL, N_BRANCH * D_MODEL), layer),
                  _layer_spec((N_BRANCH, BRANCH_W, D_MODEL), layer), _layer_spec((D_MODEL, D_MODEL), layer)],
        out_specs=row(D_MODEL),
        out_shape=jax.ShapeDtypeStruct((ntok, D_MODEL), F32),
        compiler_params=pltpu.CompilerParams(dimension_semantics=("parallel",), vmem_limit_bytes=VMEM_LIMIT),
        name="merge",
    )(*h_arrays, br_p, br_s, nw, wg, wb, wo)


def _ffn_kernel(h_ref, pp_ref, ps_ref, nf_ref, wfg_ref, wfu_ref, wfd_ref, np_ref, wpg_ref, wp_ref, nfin_ref,
                *o_refs, final, prompt_tiles):
    h = h_ref[...]
    xf = _rms(h, nf_ref[...]).astype(BF16)
    act = _silu(_dot(xf, wfg_ref[...])) * _dot(xf, wfu_ref[...])
    h = h + _dot(act.astype(BF16), wfd_ref[...])
    pg = _sigmoid(_dot(_rms(h, np_ref[...]).astype(BF16), wpg_ref[...]))
    pe = _stream_rows(pp_ref, ps_ref, prompt_tiles).astype(BF16)
    h = h + pg * _dot(pe, wp_ref[...])
    if not final:
        o_refs[0][...] = h
        return
    out = _rms(h, nfin_ref[...])
    op_ref, os_ref = o_refs

    @pl.when(pl.program_id(0) < prompt_tiles)
    def _():
        op_ref[...] = out

    @pl.when(pl.program_id(0) >= prompt_tiles)
    def _():
        os_ref[...] = out


def _ffn(h, pe_p, pe_s, nf, wfg, wfu, wfd, npl, wpg, wp, nfin, *, layer, final):
    ntok = h.shape[0]
    tm = TOKEN_TILE
    npt = pe_p.shape[1]
    prompt_tiles = npt // tm
    row = lambda n: pl.BlockSpec((tm, n), lambda i: (i, 0))
    once = lambda shape: _layer_spec(shape, layer, buffers=1)
    if final:
        out_specs = list(_stream_specs(tm, D_MODEL, prompt_tiles))
        out_shape = [jax.ShapeDtypeStruct((npt, D_MODEL), F32), jax.ShapeDtypeStruct((ntok - npt, D_MODEL), F32)]
    else:
        out_specs, out_shape = row(D_MODEL), jax.ShapeDtypeStruct((ntok, D_MODEL), F32)
    return pl.pallas_call(
        functools.partial(_ffn_kernel, final=final, prompt_tiles=prompt_tiles),
        grid=(ntok // tm,),
        in_specs=[row(D_MODEL), *_stream_specs(tm, PLE_DIM, prompt_tiles, lead=(layer,)),
                  _layer_spec((1, D_MODEL), layer),
                  once((D_MODEL, D_FF)), once((D_MODEL, D_FF)), once((D_FF, D_MODEL)),
                  _layer_spec((1, D_MODEL), layer), once((D_MODEL, D_MODEL)), once((PLE_DIM, D_MODEL)),
                  _const_spec((1, D_MODEL))],
        out_specs=out_specs,
        out_shape=out_shape,
        compiler_params=pltpu.CompilerParams(dimension_semantics=("arbitrary",), vmem_limit_bytes=VMEM_LIMIT),
        name="ffn_final" if final else "ffn",
    )(h, pe_p, pe_s, nf, wfg, wfu, wfd, npl, wpg, wp, nfin)


def _split_w_in(w_in):
    span = lambda s: w_in[..., s[0]:s[1]].astype(BF16)
    rep = lambda o: jnp.repeat(w_in[..., o:o + GDN_HEADS], GDN_DK, axis=-1)
    small = jnp.concatenate([w_in[..., W_IN_GA:W_IN_GA + GLA_RANK],
                             jnp.zeros(w_in.shape[:-1] + (LANE - GLA_RANK,), w_in.dtype),
                             rep(W_IN_DA), rep(W_IN_DB)], axis=-1).astype(BF16)
    return span(W_IN_GLA), span(W_IN_GDN), span(W_IN_REST), small


def kernel(x_prompt, x_sample, state_gla, state_gdn, state_gdn_conv, state_sconv, p_prompt, p_sample, norm_mix, w_in, gla_wa2, gla_ba, gla_norm, gdn_conv_w, gdn_a_log, gdn_dt_bias, gdn_norm, cm_ln_g, cm_ln_b, cm_ws, cm_bs, sc_conv_w, w_gate, w_branch, w_o, norm_ffn, w_ffn_gate, w_ffn_up, w_ffn_down, norm_ple, w_ple_gate, w_ple, norm_final):
    depth = w_in.shape[0]
    bp, tp, _ = x_prompt.shape
    bs, ts, _ = x_sample.shape
    npt, nst = bp * tp, bs * ts
    sseq = ROWS // ts
    assert ts == HIST and nst % SAMPLE_STEP_ROWS == 0
    assert bp % PROMPT_SEQS == 0 and tp % PROMPT_SEQ_ROWS == 0 and PROMPT_SEQ_ROWS % CM_CHUNK == 0
    assert tp % TOKEN_TILE == 0 and nst % TOKEN_TILE == 0 and TOKEN_TILE % ts == 0

    h = (x_prompt.reshape(npt, D_MODEL), x_sample.reshape(nst, D_MODEL))
    pe_p = p_prompt.reshape(depth, npt, PLE_DIM)
    pe_s = p_sample.reshape(depth, nst, PLE_DIM)
    s0_gla = state_gla.reshape(depth, bs, GLA_QK, GLA_DV)
    s0_gdn = state_gdn.reshape(depth, bs, GDN_QK, GDN_DV)
    rows = lambda a: a.reshape(depth, 1, -1)
    w_in_groups = _split_w_in(w_in)
    wg, wb, wo = w_gate.astype(BF16), w_branch.astype(BF16), w_o.astype(BF16)
    wfg, wfu, wfd = w_ffn_gate.astype(BF16), w_ffn_up.astype(BF16), w_ffn_down.astype(BF16)
    wpg, wp = w_ple_gate.astype(BF16), w_ple.astype(BF16)
    wa2 = jnp.pad(gla_wa2, ((0, 0), (0, LANE - GLA_RANK), (0, 0))).astype(BF16)
    nmix, nffn, nple = rows(norm_mix), rows(norm_ffn), rows(norm_ple)
    inproj_vecs = (rows(gla_ba), rows(jnp.repeat(gdn_a_log, GDN_DK, axis=1)),
                   rows(jnp.repeat(gdn_dt_bias, GDN_DK, axis=1)), rows(cm_ln_g), rows(cm_ln_b))
    cw = jnp.concatenate([gdn_conv_w, jnp.pad(sc_conv_w, ((0, 0), (GDN_CONV - SC_WIDTH, 0), (0, 0)))], axis=2)
    hist_s = jnp.concatenate([
        jnp.pad(state_gdn_conv, ((0, 0), (0, 0), (HIST - (GDN_CONV - 1), 0), (0, 0))),
        jnp.pad(state_sconv, ((0, 0), (0, 0), (HIST - (SC_WIDTH - 1), 0), (0, 0)))], axis=3).reshape(depth, nst, CONV_W)
    gn_gla = rows(jnp.tile(gla_norm, (1, GLA_HEADS)))
    gn_gdn = rows(jnp.tile(gdn_norm, (1, GDN_HEADS)))
    cmw_p = jnp.transpose(cm_ws, (0, 2, 1, 3)).reshape(depth, CM_CHUNK, CM_GROUPS * CM_CHUNK)
    cmb_p = jnp.repeat(jnp.swapaxes(cm_bs, 1, 2), BRANCH_W // CM_GROUPS, axis=2)
    cmw_s = jnp.transpose(jnp.tile(cm_ws[:, :, :ts, :ts], (1, 1, sseq, sseq)),
                          (0, 2, 1, 3)).reshape(depth, ROWS, CM_GROUPS * ROWS)
    cmb_s = jnp.tile(cmb_p[:, :ts], (1, sseq, 1))

    outs = {k: [] for k in ("gla_p", "gla_s", "gdn_p", "gdn_s", "gc_p", "gc_s", "sc_p", "sc_s", "cv_s")}
    for i in range(depth):
        h_p, h_s, first_s = (h[0], h[1], 0) if isinstance(h, tuple) else (h, h, npt // TOKEN_TILE)
        inproj_args = (hist_s, nmix, *w_in_groups, wa2, *inproj_vecs, cw)
        p_p, tails = _inproj(h_p, 0, npt, *inproj_args, layer=i, sample=False, seq_len=tp, seg=ts)
        p_s, xs_raw = _inproj(h_s, first_s, nst, *inproj_args, layer=i, sample=True, seq_len=tp, seg=ts)
        br_p, gla_p, gdn_p = _mixer_prompt(p_p, gn_gla, gn_gdn, cmw_p, cmb_p, layer=i, nseqs=bp, seq_len=tp)
        br_p = br_p.reshape(npt, N_BRANCH * BRANCH_W)
        br_s, gla_s, gdn_s = _mixer_sample(p_s, s0_gla, s0_gdn, gn_gla, gn_gdn, cmw_s, cmb_s, layer=i,
                                           base_step=0, nsteps=nst // SAMPLE_STEP_ROWS, seg=ts)
        outs["gla_p"].append(gla_p.reshape(bp, GLA_HEADS, GLA_DK, GLA_DV))
        outs["gla_s"].append(gla_s.reshape(bs, GLA_HEADS, GLA_DK, GLA_DV))
        outs["gdn_p"].append(gdn_p.reshape(bp, GDN_HEADS, GDN_DK, GDN_DV))
        outs["gdn_s"].append(gdn_s.reshape(bs, GDN_HEADS, GDN_DK, GDN_DV))
        tiles_per_seq = tp // TOKEN_TILE
        tail_p = tails[tiles_per_seq - 1:bp * tiles_per_seq:tiles_per_seq]
        xs3 = xs_raw.reshape(bs, ts, CONV_W)
        outs["gc_p"].append(tail_p[:, HIST - (GDN_CONV - 1):, 0:768])
        outs["gc_s"].append(xs3[:, ts - (GDN_CONV - 1):, 0:768])
        outs["sc_p"].append(tail_p[:, HIST - (SC_WIDTH - 1):, 768:])
        outs["sc_s"].append(xs3[:, ts - (SC_WIDTH - 1):, 768:])
        outs["cv_s"].append(p_s[:, P_CM + 256:P_CM + 512].reshape(bs, ts, BRANCH_W))

        h1 = _merge(h, br_p, br_s, nmix, wg, wb, wo, layer=i)
        h = _ffn(h1, pe_p, pe_s, nffn, wfg, wfu, wfd, nple, wpg, wp, norm_final.reshape(1, D_MODEL),
                 layer=i, final=(i == depth - 1))

    y_prompt = h[0].reshape(bp, tp, D_MODEL)
    y_sample = h[1].reshape(bs, ts, D_MODEL)
    st = lambda k: jnp.stack(outs[k])
    return (y_prompt, y_sample, st("gla_p"), st("gla_s"), st("gdn_p"), st("gdn_s"),
            st("gc_p"), st("gc_s"), st("sc_p"), st("sc_s"), st("cv_s"))
```

```python
import functools

import jax
import jax.numpy as jnp
from jax import lax
from jax.experimental import pallas as pl
from jax.experimental.pallas import tpu as pltpu

F32 = jnp.float32
BF16 = jnp.bfloat16

D_MODEL = 1024
PLE_DIM = 256
BRANCH_W = 256
N_BRANCH = 4
GLA_HEADS = 4
GLA_DK = 32
GLA_DV = 64
GLA_RANK = 16
GLA_TAU = 16.0
GDN_HEADS = 4
GDN_DK = 64
GDN_DV = 64
GDN_CONV = 4
CM_GROUPS = 4
CM_CHUNK = 128
SC_WIDTH = 3
D_FF = 2816
EPS = 1e-6

ROWS = 64
PROMPT_SEQS = 8
PROMPT_CHUNKS = 2
PROMPT_SEQ_ROWS = ROWS * PROMPT_CHUNKS
SAMPLE_BLOCKS_PER_STEP = 2
SAMPLE_STEP_ROWS = ROWS * SAMPLE_BLOCKS_PER_STEP
GLA_QK = GLA_HEADS * GLA_DK
GDN_QK = GDN_HEADS * GDN_DK
LANE = 128
CONV_W = 3 * BRANCH_W + BRANCH_W
HIST = 8
SUM_PIECES = 2

W_IN_GLA, W_IN_GDN, W_IN_REST = (0, 768), (784, 1808), (1816, 3096)
W_IN_GA, W_IN_DA, W_IN_DB = 768, 1808, 1812
SMALL_COLS = LANE + 2 * 256
P_GLA = 0
P_GDN = 896
P_CM = 2688
P_COLS = 3200

VMEM_LIMIT = 56 * 1024 * 1024
TOKEN_TILE = 512


def _dot(a, b):
    return jnp.dot(a, b, preferred_element_type=F32)


def _dot_nt(a, b):
    return lax.dot_general(a, b, (((1,), (1,)), ((), ())), preferred_element_type=F32)


def _dot_tn(a, b):
    return lax.dot_general(a, b, (((0,), (0,)), ((), ())), preferred_element_type=F32)


def _split(x, n):
    parts, r = [], x
    for i in range(n):
        p = r.astype(BF16)
        parts.append(p)
        if i + 1 < n:
            r = r - p.astype(F32)
    return parts


def _dot1(a, b, dot=_dot):
    return dot(a.astype(BF16), b.astype(BF16))


def _dot3(a, b_pieces):
    ah, al = _split(a, 2)
    bh, bl = b_pieces
    return _dot(jnp.concatenate([ah, ah, al], axis=1), jnp.concatenate([bh, bl, bh], axis=0))


def _mask_dot(mask, x, n):
    return _dot(jnp.concatenate([mask.astype(BF16)] * n, axis=1), jnp.concatenate(_split(x, n), axis=0))


def _mask_dot_nt(mask, x, n):
    return _dot_nt(jnp.concatenate([mask.astype(BF16)] * n, axis=1), jnp.concatenate(_split(x, n), axis=1))


def _mask_dot_rhs(x, mask, n):
    return _dot(jnp.concatenate(_split(x, n), axis=1), jnp.concatenate([mask.astype(BF16)] * n, axis=0))


def _sigmoid(x):
    return 1.0 / (1.0 + jnp.exp(-x))


def _silu(x):
    return x * _sigmoid(x)


def _softplus(x):
    return jnp.maximum(x, 0.0) + jnp.log(1.0 + jnp.exp(-jnp.abs(x)))


def _gelu_tanh(x):
    return 0.5 * x * (1.0 + jnp.tanh(0.7978845608028654 * (x + 0.044715 * (x * x * x))))


def _rms(x, w):
    return x * lax.rsqrt(jnp.mean(x * x, axis=-1, keepdims=True) + EPS) * w


def _idiv(x, n):
    assert n & (n - 1) == 0
    return lax.shift_right_logical(x, n.bit_length() - 1)


def _imod(x, n):
    assert n & (n - 1) == 0
    return lax.bitwise_and(x, n - 1)


def _const_spec(shape):
    return pl.BlockSpec(shape, lambda *_: (0,) * len(shape))


def _layer_spec(shape, layer, buffers=None):
    mode = {} if buffers is None else {"pipeline_mode": pl.Buffered(buffers)}
    return pl.BlockSpec((None,) + shape, lambda *_: (layer,) + (0,) * len(shape), **mode)


def _iota2(shape):
    return lax.broadcasted_iota(jnp.int32, shape, 0), lax.broadcasted_iota(jnp.int32, shape, 1)


def _stream_rows(hp_ref, hs_ref, prompt_tiles):
    return jnp.where(pl.program_id(0) < prompt_tiles, hp_ref[...], hs_ref[...])


def _residual_rows(h_refs, prompt_tiles):
    return h_refs[0][...] if len(h_refs) == 1 else _stream_rows(*h_refs, prompt_tiles)


def _residual_specs(h, tm):
    if isinstance(h, tuple):
        prompt_tiles = h[0].shape[0] // tm
        return list(h), list(_stream_specs(tm, D_MODEL, prompt_tiles)), prompt_tiles, h[0].shape[0] + h[1].shape[0]
    return [h], [pl.BlockSpec((tm, D_MODEL), lambda i: (i, 0))], None, h.shape[0]


def _inproj_kernel(*refs, sample, tiles_per_seq, seg):
    if sample:
        (h_ref, hist_ref, nw_ref, wgla_ref, wgdn_ref, wrest_ref, wsmall_ref, wa2_ref, ba_ref, alog_ref,
         dtb_ref, lng_ref, lnb_ref, cw_ref, p_ref, xraw_ref) = refs
    else:
        (h_ref, nw_ref, wgla_ref, wgdn_ref, wrest_ref, wsmall_ref, wa2_ref, ba_ref, alog_ref,
         dtb_ref, lng_ref, lnb_ref, cw_ref, p_ref, xraw_ref, carry) = refs
    tm = p_ref.shape[0]
    xn = _rms(h_ref[...], nw_ref[...]).astype(BF16)
    pd = _dot(xn, wgdn_ref[...])
    pr = _dot(xn, wrest_ref[...])
    pg = _dot(xn, wgla_ref[...])
    ps = _dot(xn, wsmall_ref[...])
    g0 = P_GDN
    x = jnp.concatenate([pd[:, 0:768], pr[:, 1024:1280] * pr[:, 512:768]], axis=1)
    taps = _conv_taps(cw_ref)
    if sample:
        acc = _conv_sample(x, hist_ref[...], taps, seg)
        xraw_ref[...] = x
    else:
        hist = jnp.where(lax.rem(pl.program_id(0), tiles_per_seq) == 0, 0.0, carry[...])
        acc = _conv_prompt(x, hist, taps)
        carry[...] = x[tm - HIST:tm]
        xraw_ref[...] = x[tm - HIST:tm]
    p_ref[:, g0:g0 + 768] = _silu(acc[:, 0:768])
    p_ref[:, g0 + 768:g0 + 1024] = pr[:, 768:1024] * acc[:, 768:1024]
    p_ref[:, g0 + 1024:g0 + 1280] = _silu(pd[:, 768:1024])
    p_ref[:, 0:128] = pg[:, 0:128] * (GLA_DK ** -0.5)
    p_ref[:, 128:512] = pg[:, 128:512]
    p_ref[:, 512:768] = _silu(pg[:, 512:768])
    za = _dot(ps[:, 0:LANE].astype(BF16), wa2_ref[...]) + ba_ref[...]
    p_ref[:, 768:896] = -_softplus(-za) * (1.0 / GLA_TAU)
    p_ref[:, g0 + 1280:g0 + 1536] = -jnp.exp(alog_ref[...]) * _softplus(ps[:, LANE:LANE + 256] + dtb_ref[...])
    p_ref[:, g0 + 1536:g0 + 1792] = _sigmoid(ps[:, LANE + 256:LANE + 512])
    p_ref[:, P_CM:P_CM + 256] = _gelu_tanh(pr[:, 0:256])
    gv = _gelu_tanh(pr[:, 256:512])
    mu = jnp.mean(gv, axis=-1, keepdims=True)
    d = gv - mu
    var = jnp.mean(d * d, axis=-1, keepdims=True)
    p_ref[:, P_CM + 256:P_CM + 512] = d * lax.rsqrt(var + EPS) * lng_ref[...] + lnb_ref[...]


def _inproj(h, first_tile, rows, hist_s, nw, wgla, wgdn, wrest, wsmall, wa2, ba, alog, dtb, lng, lnb, cw, *,
            layer, sample, seq_len, seg):
    tm = TOKEN_TILE
    ntiles = rows // tm
    cols = lambda span: span[1] - span[0]
    per_layer = lambda *shape: _layer_spec(shape, layer)
    tile = lambda n: pl.BlockSpec((tm, n), lambda i: (i, 0))
    if sample:
        hist = [hist_s]
        hist_spec = [pl.BlockSpec((None, tm, CONV_W), lambda i: (layer, i, 0))]
        raw_spec, raw_shape, scratch = tile(CONV_W), (rows, CONV_W), []
    else:
        hist, hist_spec = [], []
        raw_spec, raw_shape = pl.BlockSpec((None, HIST, CONV_W), lambda i: (i, 0, 0)), (ntiles, HIST, CONV_W)
        scratch = [pltpu.VMEM((HIST, CONV_W), F32)]
    return pl.pallas_call(
        functools.partial(_inproj_kernel, sample=sample, tiles_per_seq=seq_len // tm, seg=seg),
        grid=(ntiles,),
        in_specs=[pl.BlockSpec((tm, D_MODEL), lambda i: (first_tile + i, 0)), *hist_spec, per_layer(1, D_MODEL),
                  per_layer(D_MODEL, cols(W_IN_GLA)), per_layer(D_MODEL, cols(W_IN_GDN)),
                  per_layer(D_MODEL, cols(W_IN_REST)), per_layer(D_MODEL, SMALL_COLS),
                  per_layer(LANE, GLA_QK), per_layer(1, GLA_QK), per_layer(1, 256),
                  per_layer(1, 256), per_layer(1, 256), per_layer(1, 256), per_layer(GDN_CONV, CONV_W)],
        out_specs=[tile(P_COLS), raw_spec],
        out_shape=[jax.ShapeDtypeStruct((rows, P_COLS), F32), jax.ShapeDtypeStruct(raw_shape, F32)],
        scratch_shapes=scratch,
        compiler_params=pltpu.CompilerParams(dimension_semantics=("arbitrary",), vmem_limit_bytes=VMEM_LIMIT),
        name="inproj_sample" if sample else "inproj_prompt",
    )(h, *hist, nw, wgla, wgdn, wrest, wsmall, wa2, ba, alog, dtb, lng, lnb, cw)


def _stack_heads(x, group, nheads, period=None):
    w = x.shape[1]
    li = lax.broadcasted_iota(jnp.int32, (1, w), 1)
    if period is not None:
        li = _imod(li, period)
    hid = _idiv(li, group)
    zero = jnp.zeros_like(x)
    return jnp.concatenate([jnp.where(hid == h, x, zero) for h in range(nheads)], axis=0)


def _block_diag(x, nblocks):
    r = x.shape[0]
    ri, ci = _iota2((nblocks * r, nblocks * r))
    return jnp.where(_idiv(ri, r) == _idiv(ci, r), jnp.concatenate([x] * nblocks, axis=0), jnp.zeros((), x.dtype))


def _widen(x, nseq):
    if nseq == 1:
        return x
    seg = ROWS // nseq
    sid = _idiv(lax.broadcasted_iota(jnp.int32, (ROWS, 1), 0), seg)
    zero = jnp.zeros_like(x)
    return jnp.concatenate([jnp.where(sid == j, x, zero) for j in range(nseq)], axis=1)


def _group_mean(x, group):
    w = x.shape[1]
    ri, ci = _iota2((w, w))
    avg = jnp.where(_idiv(ri, group) == _idiv(ci, group), 1.0 / group, 0.0).astype(BF16)
    return _dot(x.astype(BF16), avg)


def _head_diag_mask(rows, cols, rhead, chead, rper):
    ri, ci = _iota2((rows, cols))
    return (_idiv(_imod(ri, rper), rhead) == _idiv(ci, chead)).astype(F32)


def _expand_state(s, width, reps):
    ri, ci = _iota2((width, reps * width))
    return _mask_dot_rhs(s, ri == _imod(ci, width), 3)


def _compact_state(st, width, reps):
    out = st[:, 0:width]
    for h in range(1, reps):
        out = out + st[:, h * width:(h + 1) * width]
    return out


class _Masks:
    def __init__(self, seg):
        ri, ci = _iota2((ROWS, ROWS))
        self.same = _idiv(ri, seg) == _idiv(ci, seg)
        self.tri = self.same & (ci <= ri)
        rl, cl = _iota2((ROWS, GDN_HEADS * ROWS))
        cl = _imod(cl, ROWS)
        same_l = _idiv(rl, seg) == _idiv(cl, seg)
        self.tri_l = same_l & (cl <= rl)
        self.strict_l = same_l & (cl < rl)
        self.eye_l = (rl == cl).astype(F32)
        self.levels = [(_idiv(rl, 2 * s) == _idiv(cl, 2 * s)) & (_idiv(rl, s) != _idiv(cl, s))
                       for s in (1 << k for k in range(seg.bit_length() - 1))]
        self.ones = jnp.ones((ROWS, ROWS), BF16)
        self.seg = seg
        nseq = ROWS // seg
        self.bd_gla = _head_diag_mask(nseq * GLA_QK, GLA_HEADS * GLA_DV, GLA_DK, GLA_DV, GLA_QK)
        self.bd_gdn = _head_diag_mask(nseq * GDN_QK, GDN_QK, GDN_DK, GDN_DV, GDN_QK)


def _gla_prep(p, m, nseq):
    q, k, v, la = p[:, 0:128], p[:, 128:256], p[:, 256:512], p[:, 768:896]
    b = _mask_dot(m.tri, la, SUM_PIECES)
    if m.seg == ROWS:
        btot = jnp.broadcast_to(b[ROWS - 1:ROWS], b.shape)
    else:
        btot = _mask_dot(m.same, la, SUM_PIECES)
    qd = q * jnp.exp(b)
    kd = k * jnp.exp(-b)
    ke = k * jnp.exp(btot - b)
    a = jnp.where(m.tri_l, _dot1(qd, _stack_heads(kd, GLA_DK, GLA_HEADS), _dot_nt), 0.0)
    o_intra = _dot1(a, _stack_heads(v, GLA_DV, GLA_HEADS))
    dec = jnp.exp(_dot_tn(jnp.concatenate(_split(_widen(la, nseq), SUM_PIECES), axis=0),
                          jnp.ones((SUM_PIECES * ROWS, GLA_HEADS * GLA_DV), BF16)))
    ds = _dot1(_widen(ke, nseq), v, _dot_tn) * m.bd_gla
    return o_intra, _widen(qd, nseq), dec, ds


def _gla_scan(preps, sts):
    outs = [prep[0] + _dot1(prep[1], st) for prep, st in zip(preps, sts)]
    return outs, [st * prep[2] + prep[3] for prep, st in zip(preps, sts)]


def _gla_out(o, gn, rs):
    return o * lax.rsqrt(_group_mean(o * o, GLA_DV) + EPS) * gn * rs


def _gdn_qk_norm(qkv):
    hd = GDN_DK
    r = qkv.shape[0]
    cqk = jnp.concatenate([qkv[:, 0:256], qkv[:, 256:512]], axis=0)
    nrm = lax.rsqrt(_group_mean(cqk * cqk, hd) * hd + EPS)
    return qkv[:, 0:256] * nrm[0:r] * (hd ** -0.5), qkv[:, 256:512] * nrm[r:2 * r]


def _gdn_prep(q, k, cv, gs, betas, m, seg):
    nh, hd = GDN_HEADS, GDN_DK
    n = range(len(q))
    gc = [_mask_dot(m.tri, gs[i], SUM_PIECES) for i in n]
    if seg == ROWS:
        gtot = [jnp.broadcast_to(gc[i][ROWS - 1:ROWS], gc[i].shape) for i in n]
    else:
        gtot = [_mask_dot(m.same, gs[i], SUM_PIECES) for i in n]
    qkk = [_dot1(jnp.concatenate([q[i], k[i]], axis=0), _stack_heads(k[i], hd, nh), _dot_nt) for i in n]
    grow = [_mask_dot(m.ones, gc[i] * m.eye_l, SUM_PIECES) for i in n]
    decay = [jnp.where(m.tri_l, jnp.exp(jnp.where(m.tri_l, gc[i] - grow[i], 0.0)), 0.0) for i in n]
    amat = [jnp.where(m.strict_l, betas[i] * decay[i] * qkk[i][ROWS:2 * ROWS], 0.0) for i in n]
    inv = [m.eye_l - jnp.where(m.levels[0], amat[i], 0.0) for i in n]
    for lvl in m.levels[1:]:
        low = [_block_diag(jnp.where(lvl, amat[i], 0.0).astype(BF16), nh) for i in n]
        prod = [_dot(inv[i].astype(BF16), low[i]) for i in n]
        inv = [inv[i] - _dot(prod[i].astype(BF16), _block_diag(inv[i].astype(BF16), nh)) for i in n]
    eg = [jnp.exp(gc[i]) for i in n]
    rhs = [jnp.concatenate([betas[i] * cv[i], betas[i] * eg[i] * k[i]], axis=1) for i in n]
    uw = [_dot(inv[i].astype(BF16), _stack_heads(rhs[i].astype(BF16), hd, nh, period=GDN_QK)) for i in n]
    return [(uw[i][:, 0:256], uw[i][:, 256:512], q[i] * eg[i], qkk[i][0:ROWS] * decay[i],
             k[i] * jnp.exp(gtot[i] - gc[i]), jnp.exp(gtot[i])) for i in n]


def _gdn_scan(preps, sts, m, nseq):
    seg = ROWS // nseq
    n = range(len(preps))
    ws = [_dot1(jnp.concatenate([_widen(preps[i][1], nseq), _widen(preps[i][2], nseq)], axis=0), sts[i]) for i in n]
    u = [preps[i][0] - ws[i][0:ROWS] for i in n]
    outs = [ws[i][ROWS:2 * ROWS] + _dot1(preps[i][3], _stack_heads(u[i], GDN_DV, GDN_HEADS)) for i in n]
    new = []
    for i in n:
        dn = preps[i][5]
        dn_tall = jnp.concatenate(
            [jnp.broadcast_to(dn[j * seg:j * seg + 1], (GDN_QK, GDN_QK)) for j in range(nseq)], axis=0)
        new.append(sts[i] * dn_tall + _dot1(_widen(preps[i][4], nseq), u[i], _dot_tn) * m.bd_gdn)
    return outs, new


def _gdn_out(o, gn, zs):
    return o * lax.rsqrt(_group_mean(o * o, GDN_DV) + EPS) * gn * zs


def _cm_block(p, ws, bias, seg):
    r = p.shape[0]
    gu, vn = p[:, 0:256], p[:, 256:512]
    ri, ci = _iota2((r, CM_GROUPS * r))
    ci = _imod(ci, r)
    wm = jnp.where((_idiv(ri, seg) == _idiv(ci, seg)) & (ci <= ri), ws, 0.0)
    return gu * (_dot1(wm, _stack_heads(vn, BRANCH_W // CM_GROUPS, CM_GROUPS)) + bias)


def _conv_taps(cw_ref):
    return [cw_ref[GDN_CONV - 1 - d:GDN_CONV - d, :] for d in range(GDN_CONV)]


def _conv_prompt(x, hist, taps):
    n = x.shape[0]
    t8 = lax.broadcasted_iota(jnp.int32, (HIST, 1), 0)
    acc = taps[0] * x
    for d in range(1, GDN_CONV):
        xr = pltpu.roll(x, d, 0)
        head = jnp.where(t8 < d, pltpu.roll(hist, d, 0), xr[0:HIST])
        acc = acc + taps[d] * jnp.concatenate([head, xr[HIST:n]], axis=0)
    return acc


def _conv_sample(x, hist, taps, seg):
    n = x.shape[0]
    tloc = _imod(lax.broadcasted_iota(jnp.int32, (n, 1), 0), seg)
    acc = taps[0] * x
    for d in range(1, GDN_CONV):
        prev = jnp.where(tloc < d, pltpu.roll(hist, (d - seg) % n, 0), pltpu.roll(x, d, 0))
        acc = acc + taps[d] * prev
    return acc


def _block_rows(c):
    return slice(c * ROWS, (c + 1) * ROWS)


def _mixer_prompt_kernel(*refs):
    ns, nc, sr = PROMPT_SEQS, PROMPT_CHUNKS, PROMPT_SEQ_ROWS
    p_refs = refs[:ns]
    gn_gla_ref, gn_gdn_ref, cmw_ref, cmb_ref, o_ref, sgla_ref, sgdn_ref, st_gla, st_gdn = refs[ns:]

    @pl.when(pl.program_id(1) == 0)
    def _():
        st_gla[...] = jnp.zeros_like(st_gla)
        st_gdn[...] = jnp.zeros_like(st_gdn)

    m = _Masks(ROWS)
    g0 = P_GDN
    for s in range(ns):
        o_ref[s, :, 768:1024] = p_refs[s][:, g0 + 768:g0 + 1024]
    qkv = jnp.concatenate([p_refs[s][:, g0:g0 + 768] for s in range(ns)], axis=0)
    qn, kn = _gdn_qk_norm(qkv)
    blocks = [(s, c) for c in range(nc) for s in range(ns)]
    rows_of = lambda s, c: slice(s * sr + c * ROWS, s * sr + (c + 1) * ROWS)
    gdn = _gdn_prep([qn[rows_of(s, c)] for s, c in blocks], [kn[rows_of(s, c)] for s, c in blocks],
                    [qkv[rows_of(s, c), 512:768] for s, c in blocks],
                    [p_refs[s][_block_rows(c), g0 + 1280:g0 + 1536] for s, c in blocks],
                    [p_refs[s][_block_rows(c), g0 + 1536:g0 + 1792] for s, c in blocks], m, ROWS)
    gla = [_gla_prep(p_refs[s][_block_rows(c), P_GLA:P_GLA + 896], m, 1) for s, c in blocks]
    sg = [st_gla[s] for s in range(ns)]
    sd = [st_gdn[s] for s in range(ns)]
    og, od = {}, {}
    for c in range(nc):
        o, sg = _gla_scan(gla[c * ns:(c + 1) * ns], sg)
        og.update({(s, c): o[s] for s in range(ns)})
        o, sd = _gdn_scan(gdn[c * ns:(c + 1) * ns], sd, m, 1)
        od.update({(s, c): o[s] for s in range(ns)})
    by_rows = lambda d: jnp.concatenate([d[(s, c)] for s in range(ns) for c in range(nc)], axis=0)
    rs = jnp.concatenate([p_refs[s][:, 512:768] for s in range(ns)], axis=0)
    zs = jnp.concatenate([p_refs[s][:, g0 + 1024:g0 + 1280] for s in range(ns)], axis=0)
    o_gla = _gla_out(by_rows(og), gn_gla_ref[...], rs)
    o_gdn = _gdn_out(by_rows(od), gn_gdn_ref[...], zs)
    for s in range(ns):
        st_gla[s] = sg[s]
        st_gdn[s] = sd[s]
        o_ref[s, :, 0:256] = o_gla[s * sr:(s + 1) * sr]
        o_ref[s, :, 256:512] = o_gdn[s * sr:(s + 1) * sr]
        for c in range(sr // CM_CHUNK):
            rows = slice(c * CM_CHUNK, (c + 1) * CM_CHUNK)
            o_ref[s, rows, 512:768] = _cm_block(p_refs[s][rows, P_CM:P_CM + 512], cmw_ref[...], cmb_ref[...], CM_CHUNK)

    @pl.when(pl.program_id(1) == pl.num_programs(1) - 1)
    def _():
        for s in range(ns):
            sgla_ref[s] = _compact_state(sg[s], GLA_DV, GLA_HEADS)
            sgdn_ref[s] = _compact_state(sd[s], GDN_DV, GDN_HEADS)


def _mixer_prompt(p, gn_gla, gn_gdn, cmw, cmb, *, layer, nseqs, seq_len):
    ns, sr = PROMPT_SEQS, PROMPT_SEQ_ROWS
    nsteps = seq_len // sr
    seq_spec = lambda s: pl.BlockSpec((sr, P_COLS), lambda o, c: ((o * ns + s) * nsteps + c, 0))
    per_group = lambda *shape: pl.BlockSpec((ns,) + shape, lambda o, c: (o,) + (0,) * len(shape))
    per_layer = lambda *shape: _layer_spec(shape, layer)
    return pl.pallas_call(
        _mixer_prompt_kernel,
        grid=(nseqs // ns, nsteps),
        in_specs=[seq_spec(s) for s in range(ns)] + [
            per_layer(1, 256), per_layer(1, 256),
            per_layer(CM_CHUNK, CM_GROUPS * CM_CHUNK), per_layer(CM_CHUNK, 256)],
        out_specs=[pl.BlockSpec((ns, sr, 4 * BRANCH_W), lambda o, c: (o, c, 0)),
                   per_group(GLA_QK, GLA_DV), per_group(GDN_QK, GDN_DV)],
        out_shape=[jax.ShapeDtypeStruct((nseqs, seq_len, 4 * BRANCH_W), F32),
                   jax.ShapeDtypeStruct((nseqs, GLA_QK, GLA_DV), F32),
                   jax.ShapeDtypeStruct((nseqs, GDN_QK, GDN_DV), F32)],
        scratch_shapes=[pltpu.VMEM((ns, GLA_QK, GLA_HEADS * GLA_DV), F32), pltpu.VMEM((ns, GDN_QK, GDN_QK), F32)],
        compiler_params=pltpu.CompilerParams(dimension_semantics=("arbitrary", "arbitrary"),
                                             vmem_limit_bytes=VMEM_LIMIT),
        name="mixer_prompt",
    )(*([p] * ns), gn_gla, gn_gdn, cmw, cmb)


def _mixer_sample_kernel(p_ref, s0gla_ref, s0gdn_ref, gn_gla_ref, gn_gdn_ref, cmw_ref, cmb_ref,
                         o_ref, sgla_ref, sgdn_ref, *, seg):
    nseq = ROWS // seg
    nb = range(SAMPLE_BLOCKS_PER_STEP)
    m = _Masks(seg)
    g0 = P_GDN
    o_ref[:, 768:1024] = p_ref[:, g0 + 768:g0 + 1024]
    qkv = p_ref[:, g0:g0 + 768]
    qn, kn = _gdn_qk_norm(qkv)
    gdn = _gdn_prep([qn[_block_rows(c)] for c in nb], [kn[_block_rows(c)] for c in nb],
                    [qkv[_block_rows(c), 512:768] for c in nb],
                    [p_ref[_block_rows(c), g0 + 1280:g0 + 1536] for c in nb],
                    [p_ref[_block_rows(c), g0 + 1536:g0 + 1792] for c in nb], m, seg)
    gla = [_gla_prep(p_ref[_block_rows(c), P_GLA:P_GLA + 896], m, nseq) for c in nb]
    seqs = lambda c: slice(c * nseq, (c + 1) * nseq)
    sg = [_expand_state(s0gla_ref[seqs(c)].reshape(nseq * GLA_QK, GLA_DV), GLA_DV, GLA_HEADS) * m.bd_gla for c in nb]
    sd = [_expand_state(s0gdn_ref[seqs(c)].reshape(nseq * GDN_QK, GDN_DV), GDN_DV, GDN_HEADS) * m.bd_gdn for c in nb]
    og, sg = _gla_scan(gla, sg)
    od, sd = _gdn_scan(gdn, sd, m, nseq)
    for c in nb:
        rows = _block_rows(c)
        sgla_ref[seqs(c)] = _compact_state(sg[c], GLA_DV, GLA_HEADS).reshape(nseq, GLA_QK, GLA_DV)
        sgdn_ref[seqs(c)] = _compact_state(sd[c], GDN_DV, GDN_HEADS).reshape(nseq, GDN_QK, GDN_DV)
        o_ref[rows, 512:768] = _cm_block(p_ref[rows, P_CM:P_CM + 512], cmw_ref[...], cmb_ref[...], seg)
    o_ref[:, 0:256] = _gla_out(jnp.concatenate(og, axis=0), gn_gla_ref[...], p_ref[:, 512:768])
    o_ref[:, 256:512] = _gdn_out(jnp.concatenate(od, axis=0), gn_gdn_ref[...], p_ref[:, g0 + 1024:g0 + 1280])


def _mixer_sample(p, s0gla, s0gdn, gn_gla, gn_gdn, cmw, cmb, *, layer, base_step, nsteps, seg):
    rows = SAMPLE_STEP_ROWS
    nseq_step = rows // seg
    per_layer = lambda *shape: _layer_spec(shape, layer)
    return pl.pallas_call(
        functools.partial(_mixer_sample_kernel, seg=seg),
        grid=(nsteps,),
        in_specs=[pl.BlockSpec((rows, P_COLS), lambda i: (base_step + i, 0)),
                  pl.BlockSpec((None, nseq_step, GLA_QK, GLA_DV), lambda i: (layer, i, 0, 0)),
                  pl.BlockSpec((None, nseq_step, GDN_QK, GDN_DV), lambda i: (layer, i, 0, 0)),
                  per_layer(1, 256), per_layer(1, 256),
                  per_layer(ROWS, CM_GROUPS * ROWS), per_layer(ROWS, 256)],
        out_specs=[pl.BlockSpec((rows, 4 * BRANCH_W), lambda i: (i, 0)),
                   pl.BlockSpec((nseq_step, GLA_QK, GLA_DV), lambda i: (i, 0, 0)),
                   pl.BlockSpec((nseq_step, GDN_QK, GDN_DV), lambda i: (i, 0, 0))],
        out_shape=[jax.ShapeDtypeStruct((nsteps * rows, 4 * BRANCH_W), F32),
                   jax.ShapeDtypeStruct((nsteps * nseq_step, GLA_QK, GLA_DV), F32),
                   jax.ShapeDtypeStruct((nsteps * nseq_step, GDN_QK, GDN_DV), F32)],
        compiler_params=pltpu.CompilerParams(dimension_semantics=("parallel",), vmem_limit_bytes=VMEM_LIMIT),
        name="mixer_sample",
    )(p, s0gla, s0gdn, gn_gla, gn_gdn, cmw, cmb)


def _merge_kernel(*refs, prompt_tiles, pair):
    nh = 2 if pair else 1
    bp_ref, bs_ref, nw_ref, wg_ref, wb_ref, wo_ref, o_ref = refs[nh:]
    h = _residual_rows(refs[:nh], prompt_tiles)
    xn = _rms(h, nw_ref[...]).astype(BF16)
    br = _stream_rows(bp_ref, bs_ref, prompt_tiles).astype(BF16)
    merged = None
    for gi in range(N_BRANCH):
        gate = _sigmoid(_dot(xn, wg_ref[:, gi * D_MODEL:(gi + 1) * D_MODEL]))
        term = _dot(br[:, gi * BRANCH_W:(gi + 1) * BRANCH_W], wb_ref[gi]) * gate
        merged = term if merged is None else merged + term
    o_ref[...] = h + _dot(merged.astype(BF16), wo_ref[...])


def _stream_specs(tm, width, prompt_tiles, lead=()):
    nlead = (None,) * len(lead)
    return (pl.BlockSpec(nlead + (tm, width), lambda i: lead + (jnp.minimum(i, prompt_tiles - 1), 0)),
            pl.BlockSpec(nlead + (tm, width), lambda i: lead + (jnp.maximum(i - prompt_tiles, 0), 0)))


def _merge(h, br_p, br_s, nw, wg, wb, wo, *, layer):
    tm = TOKEN_TILE
    h_arrays, h_specs, _, ntok = _residual_specs(h, tm)
    prompt_tiles = br_p.shape[0] // tm
    row = lambda n: pl.BlockSpec((tm, n), lambda i: (i, 0))
    return pl.pallas_call(
        functools.partial(_merge_kernel, prompt_tiles=prompt_tiles, pair=isinstance(h, tuple)),
        grid=(ntok // tm,),
        in_specs=[*h_specs, *_stream_specs(tm, 4 * BRANCH_W, prompt_tiles), _layer_spec((1, D_MODEL), layer),
                  _layer_spec((D_MODEL, N_BRANCH * D_MODEL), layer),
                  _layer_spec((N_BRANCH, BRANCH_W, D_MODEL), layer), _layer_spec((D_MODEL, D_MODEL), layer)],
        out_specs=row(D_MODEL),
        out_shape=jax.ShapeDtypeStruct((ntok, D_MODEL), F32),
        compiler_params=pltpu.CompilerParams(dimension_semantics=("parallel",), vmem_limit_bytes=VMEM_LIMIT),
        name="merge",
    )(*h_arrays, br_p, br_s, nw, wg, wb, wo)


def _ffn_kernel(h_ref, pp_ref, ps_ref, nf_ref, wfg_ref, wfu_ref, wfd_ref, np_ref, wpg_ref, wp_ref, nfin_ref,
                *o_refs, final, prompt_tiles):
    h = h_ref[...]
    xf = _rms(h, nf_ref[...]).astype(BF16)
    act = _silu(_dot(xf, wfg_ref[...])) * _dot(xf, wfu_ref[...])
    h = h + _dot(act.astype(BF16), wfd_ref[...])
    pg = _sigmoid(_dot(_rms(h, np_ref[...]).astype(BF16), wpg_ref[...]))
    pe = _stream_rows(pp_ref, ps_ref, prompt_tiles).astype(BF16)
    h = h + pg * _dot(pe, wp_ref[...])
    if not final:
        o_refs[0][...] = h
        return
    out = _rms(h, nfin_ref[...])
    op_ref, os_ref = o_refs

    @pl.when(pl.program_id(0) < prompt_tiles)
    def _():
        op_ref[...] = out

    @pl.when(pl.program_id(0) >= prompt_tiles)
    def _():
        os_ref[...] = out


def _ffn(h, pe_p, pe_s, nf, wfg, wfu, wfd, npl, wpg, wp, nfin, *, layer, final):
    ntok = h.shape[0]
    tm = TOKEN_TILE
    npt = pe_p.shape[1]
    prompt_tiles = npt // tm
    row = lambda n: pl.BlockSpec((tm, n), lambda i: (i, 0))
    once = lambda shape: _layer_spec(shape, layer, buffers=1)
    if final:
        out_specs = list(_stream_specs(tm, D_MODEL, prompt_tiles))
        out_shape = [jax.ShapeDtypeStruct((npt, D_MODEL), F32), jax.ShapeDtypeStruct((ntok - npt, D_MODEL), F32)]
    else:
        out_specs, out_shape = row(D_MODEL), jax.ShapeDtypeStruct((ntok, D_MODEL), F32)
    return pl.pallas_call(
        functools.partial(_ffn_kernel, final=final, prompt_tiles=prompt_tiles),
        grid=(ntok // tm,),
        in_specs=[row(D_MODEL), *_stream_specs(tm, PLE_DIM, prompt_tiles, lead=(layer,)),
                  _layer_spec((1, D_MODEL), layer),
                  once((D_MODEL, D_FF)), once((D_MODEL, D_FF)), once((D_FF, D_MODEL)),
                  _layer_spec((1, D_MODEL), layer), once((D_MODEL, D_MODEL)), once((PLE_DIM, D_MODEL)),
                  _const_spec((1, D_MODEL))],
        out_specs=out_specs,
        out_shape=out_shape,
        compiler_params=pltpu.CompilerParams(dimension_semantics=("arbitrary",), vmem_limit_bytes=VMEM_LIMIT),
        name="ffn_final" if final else "ffn",
    )(h, pe_p, pe_s, nf, wfg, wfu, wfd, npl, wpg, wp, nfin)


def _split_w_in(w_in):
    span = lambda s: w_in[..., s[0]:s[1]].astype(BF16)
    rep = lambda o: jnp.repeat(w_in[..., o:o + GDN_HEADS], GDN_DK, axis=-1)
    small = jnp.concatenate([w_in[..., W_IN_GA:W_IN_GA + GLA_RANK],
                             jnp.zeros(w_in.shape[:-1] + (LANE - GLA_RANK,), w_in.dtype),
                             rep(W_IN_DA), rep(W_IN_DB)], axis=-1).astype(BF16)
    return span(W_IN_GLA), span(W_IN_GDN), span(W_IN_REST), small


def kernel(x_prompt, x_sample, state_gla, state_gdn, state_gdn_conv, state_sconv, p_prompt, p_sample, norm_mix, w_in, gla_wa2, gla_ba, gla_norm, gdn_conv_w, gdn_a_log, gdn_dt_bias, gdn_norm, cm_ln_g, cm_ln_b, cm_ws, cm_bs, sc_conv_w, w_gate, w_branch, w_o, norm_ffn, w_ffn_gate, w_ffn_up, w_ffn_down, norm_ple, w_ple_gate, w_ple, norm_final):
    depth = w_in.shape[0]
    bp, tp, _ = x_prompt.shape
    bs, ts, _ = x_sample.shape
    npt, nst = bp * tp, bs * ts
    sseq = ROWS // ts
    assert ts == HIST and nst % SAMPLE_STEP_ROWS == 0
    assert bp % PROMPT_SEQS == 0 and tp % PROMPT_SEQ_ROWS == 0 and PROMPT_SEQ_ROWS % CM_CHUNK == 0
    assert tp % TOKEN_TILE == 0 and nst % TOKEN_TILE == 0 and TOKEN_TILE % ts == 0

    h = (x_prompt.reshape(npt, D_MODEL), x_sample.reshape(nst, D_MODEL))
    pe_p = p_prompt.reshape(depth, npt, PLE_DIM)
    pe_s = p_sample.reshape(depth, nst, PLE_DIM)
    s0_gla = state_gla.reshape(depth, bs, GLA_QK, GLA_DV)
    s0_gdn = state_gdn.reshape(depth, bs, GDN_QK, GDN_DV)
    rows = lambda a: a.reshape(depth, 1, -1)
    w_in_groups = _split_w_in(w_in)
    wg, wb, wo = w_gate.astype(BF16), w_branch.astype(BF16), w_o.astype(BF16)
    wfg, wfu, wfd = w_ffn_gate.astype(BF16), w_ffn_up.astype(BF16), w_ffn_down.astype(BF16)
    wpg, wp = w_ple_gate.astype(BF16), w_ple.astype(BF16)
    wa2 = jnp.pad(gla_wa2, ((0, 0), (0, LANE - GLA_RANK), (0, 0))).astype(BF16)
    nmix, nffn, nple = rows(norm_mix), rows(norm_ffn), rows(norm_ple)
    inproj_vecs = (rows(gla_ba), rows(jnp.repeat(gdn_a_log, GDN_DK, axis=1)),
                   rows(jnp.repeat(gdn_dt_bias, GDN_DK, axis=1)), rows(cm_ln_g), rows(cm_ln_b))
    cw = jnp.concatenate([gdn_conv_w, jnp.pad(sc_conv_w, ((0, 0), (GDN_CONV - SC_WIDTH, 0), (0, 0)))], axis=2)
    hist_s = jnp.concatenate([
        jnp.pad(state_gdn_conv, ((0, 0), (0, 0), (HIST - (GDN_CONV - 1), 0), (0, 0))),
        jnp.pad(state_sconv, ((0, 0), (0, 0), (HIST - (SC_WIDTH - 1), 0), (0, 0)))], axis=3).reshape(depth, nst, CONV_W)
    gn_gla = rows(jnp.tile(gla_norm, (1, GLA_HEADS)))
    gn_gdn = rows(jnp.tile(gdn_norm, (1, GDN_HEADS)))
    cmw_p = jnp.transpose(cm_ws, (0, 2, 1, 3)).reshape(depth, CM_CHUNK, CM_GROUPS * CM_CHUNK)
    cmb_p = jnp.repeat(jnp.swapaxes(cm_bs, 1, 2), BRANCH_W // CM_GROUPS, axis=2)
    cmw_s = jnp.transpose(jnp.tile(cm_ws[:, :, :ts, :ts], (1, 1, sseq, sseq)),
                          (0, 2, 1, 3)).reshape(depth, ROWS, CM_GROUPS * ROWS)
    cmb_s = jnp.tile(cmb_p[:, :ts], (1, sseq, 1))

    outs = {k: [] for k in ("gla_p", "gla_s", "gdn_p", "gdn_s", "gc_p", "gc_s", "sc_p", "sc_s", "cv_s")}
    for i in range(depth):
        h_p, h_s, first_s = (h[0], h[1], 0) if isinstance(h, tuple) else (h, h, npt // TOKEN_TILE)
        inproj_args = (hist_s, nmix, *w_in_groups, wa2, *inproj_vecs, cw)
        p_p, tails = _inproj(h_p, 0, npt, *inproj_args, layer=i, sample=False, seq_len=tp, seg=ts)
        p_s, xs_raw = _inproj(h_s, first_s, nst, *inproj_args, layer=i, sample=True, seq_len=tp, seg=ts)
        br_p, gla_p, gdn_p = _mixer_prompt(p_p, gn_gla, gn_gdn, cmw_p, cmb_p, layer=i, nseqs=bp, seq_len=tp)
        br_p = br_p.reshape(npt, N_BRANCH * BRANCH_W)
        br_s, gla_s, gdn_s = _mixer_sample(p_s, s0_gla, s0_gdn, gn_gla, gn_gdn, cmw_s, cmb_s, layer=i,
                                           base_step=0, nsteps=nst // SAMPLE_STEP_ROWS, seg=ts)
        outs["gla_p"].append(gla_p.reshape(bp, GLA_HEADS, GLA_DK, GLA_DV))
        outs["gla_s"].append(gla_s.reshape(bs, GLA_HEADS, GLA_DK, GLA_DV))
        outs["gdn_p"].append(gdn_p.reshape(bp, GDN_HEADS, GDN_DK, GDN_DV))
        outs["gdn_s"].append(gdn_s.reshape(bs, GDN_HEADS, GDN_DK, GDN_DV))
        tiles_per_seq = tp // TOKEN_TILE
        tail_p = tails[tiles_per_seq - 1:bp * tiles_per_seq:tiles_per_seq]
        xs3 = xs_raw.reshape(bs, ts, CONV_W)
        outs["gc_p"].append(tail_p[:, HIST - (GDN_CONV - 1):, 0:768])
        outs["gc_s"].append(xs3[:, ts - (GDN_CONV - 1):, 0:768])
        outs["sc_p"].append(tail_p[:, HIST - (SC_WIDTH - 1):, 768:])
        outs["sc_s"].append(xs3[:, ts - (SC_WIDTH - 1):, 768:])
        outs["cv_s"].append(p_s[:, P_CM + 256:P_CM + 512].reshape(bs, ts, BRANCH_W))

        h1 = _merge(h, br_p, br_s, nmix, wg, wb, wo, layer=i)
        h = _ffn(h1, pe_p, pe_s, nffn, wfg, wfu, wfd, nple, wpg, wp, norm_final.reshape(1, D_MODEL),
                 layer=i, final=(i == depth - 1))

    y_prompt = h[0].reshape(bp, tp, D_MODEL)
    y_sample = h[1].reshape(bs, ts, D_MODEL)
    st = lambda k: jnp.stack(outs[k])
    return (y_prompt, y_sample, st("gla_p"), st("gla_s"), st("gdn_p"), st("gdn_s"),
            st("gc_p"), st("gc_s"), st("sc_p"), st("sc_s"), st("cv_s"))
```

```python
import functools

import jax
import jax.numpy as jnp
from jax import lax
from jax.experimental import pallas as pl
from jax.experimental.pallas import tpu as pltpu

F32 = jnp.float32
BF16 = jnp.bfloat16

D_MODEL = 1024
PLE_DIM = 256
BRANCH_W = 256
N_BRANCH = 4
GLA_HEADS = 4
GLA_DK = 32
GLA_DV = 64
GLA_RANK = 16
GLA_TAU = 16.0
GDN_HEADS = 4
GDN_DK = 64
GDN_DV = 64
GDN_CONV = 4
CM_GROUPS = 4
CM_CHUNK = 128
SC_WIDTH = 3
D_FF = 2816
EPS = 1e-6

ROWS = 64
PROMPT_SEQS = 8
PROMPT_CHUNKS = 2
PROMPT_SEQ_ROWS = ROWS * PROMPT_CHUNKS
SAMPLE_BLOCKS_PER_STEP = 2
SAMPLE_STEP_ROWS = ROWS * SAMPLE_BLOCKS_PER_STEP
GLA_QK = GLA_HEADS * GLA_DK
GDN_QK = GDN_HEADS * GDN_DK
LANE = 128
CONV_W = 3 * BRANCH_W + BRANCH_W
HIST = 8
SUM_PIECES = 2

W_IN_GLA, W_IN_GDN, W_IN_REST = (0, 768), (784, 1808), (1816, 3096)
W_IN_GA, W_IN_DA, W_IN_DB = 768, 1808, 1812
SMALL_COLS = LANE + 2 * 256
P_GLA = 0
P_GDN = 896
P_CM = 2688
P_COLS = 3200

V7X_VMEM_BYTES = 64 * 1024 * 1024
VMEM_LIMIT = V7X_VMEM_BYTES * 7 // 8
TOKEN_TILE = 512


def _dot(a, b):
    return jnp.dot(a, b, preferred_element_type=F32)


def _dot_nt(a, b):
    return lax.dot_general(a, b, (((1,), (1,)), ((), ())), preferred_element_type=F32)


def _dot_tn(a, b):
    return lax.dot_general(a, b, (((0,), (0,)), ((), ())), preferred_element_type=F32)


def _split(x, n):
    parts, r = [], x
    for i in range(n):
        p = r.astype(BF16)
        parts.append(p)
        if i + 1 < n:
            r = r - p.astype(F32)
    return parts


def _dot1(a, b, dot=_dot):
    return dot(a.astype(BF16), b.astype(BF16))


def _mask_dot(mask, x, n):
    return _dot(jnp.concatenate([mask.astype(BF16)] * n, axis=1), jnp.concatenate(_split(x, n), axis=0))


def _sigmoid(x):
    return 1.0 / (1.0 + jnp.exp(-x))


def _silu(x):
    return x * _sigmoid(x)


def _softplus(x):
    return jnp.maximum(x, 0.0) + jnp.log(1.0 + jnp.exp(-jnp.abs(x)))


def _gelu_tanh(x):
    return 0.5 * x * (1.0 + jnp.tanh(0.7978845608028654 * (x + 0.044715 * (x * x * x))))


def _rms(x, w):
    return x * lax.rsqrt(jnp.mean(x * x, axis=-1, keepdims=True) + EPS) * w


def _idiv(x, n):
    assert n & (n - 1) == 0
    return lax.shift_right_logical(x, n.bit_length() - 1)


def _imod(x, n):
    assert n & (n - 1) == 0
    return lax.bitwise_and(x, n - 1)


def _const_spec(shape):
    return pl.BlockSpec(shape, lambda *_: (0,) * len(shape))


def _layer_spec(shape, layer, buffers=None):
    mode = {} if buffers is None else {"pipeline_mode": pl.Buffered(buffers)}
    return pl.BlockSpec((None,) + shape, lambda *_: (layer,) + (0,) * len(shape), **mode)


def _iota2(shape):
    return lax.broadcasted_iota(jnp.int32, shape, 0), lax.broadcasted_iota(jnp.int32, shape, 1)


def _stream_rows(hp_ref, hs_ref, prompt_tiles):
    return jnp.where(pl.program_id(0) < prompt_tiles, hp_ref[...], hs_ref[...])


def _residual_rows(h_refs, prompt_tiles):
    return h_refs[0][...] if len(h_refs) == 1 else _stream_rows(*h_refs, prompt_tiles)


def _residual_specs(h, tm):
    if isinstance(h, tuple):
        prompt_tiles = h[0].shape[0] // tm
        return list(h), list(_stream_specs(tm, D_MODEL, prompt_tiles)), prompt_tiles, h[0].shape[0] + h[1].shape[0]
    return [h], [pl.BlockSpec((tm, D_MODEL), lambda i: (i, 0))], None, h.shape[0]


def _inproj_kernel(*refs, sample, tiles_per_seq, seg):
    if sample:
        (h_ref, hist_ref, nw_ref, wgla_ref, wgdn_ref, wrest_ref, wsmall_ref, wa2_ref, ba_ref, alog_ref,
         dtb_ref, lng_ref, lnb_ref, cw_ref, p_ref, xraw_ref) = refs
    else:
        (h_ref, nw_ref, wgla_ref, wgdn_ref, wrest_ref, wsmall_ref, wa2_ref, ba_ref, alog_ref,
         dtb_ref, lng_ref, lnb_ref, cw_ref, p_ref, xraw_ref, carry) = refs
    tm = p_ref.shape[0]
    xn = _rms(h_ref[...], nw_ref[...]).astype(BF16)
    pd = _dot(xn, wgdn_ref[...])
    pr = _dot(xn, wrest_ref[...])
    pg = _dot(xn, wgla_ref[...])
    ps = _dot(xn, wsmall_ref[...])
    g0 = P_GDN
    x = jnp.concatenate([pd[:, 0:768], pr[:, 1024:1280] * pr[:, 512:768]], axis=1)
    taps = _conv_taps(cw_ref)
    if sample:
        acc = _conv_sample(x, hist_ref[...], taps, seg)
        xraw_ref[...] = x
    else:
        hist = jnp.where(lax.rem(pl.program_id(0), tiles_per_seq) == 0, 0.0, carry[...])
        acc = _conv_prompt(x, hist, taps)
        carry[...] = x[tm - HIST:tm]
        xraw_ref[...] = x[tm - HIST:tm]
    p_ref[:, g0:g0 + 768] = _silu(acc[:, 0:768])
    p_ref[:, g0 + 768:g0 + 1024] = pr[:, 768:1024] * acc[:, 768:1024]
    p_ref[:, g0 + 1024:g0 + 1280] = _silu(pd[:, 768:1024])
    p_ref[:, 0:128] = pg[:, 0:128] * (GLA_DK ** -0.5)
    p_ref[:, 128:512] = pg[:, 128:512]
    p_ref[:, 512:768] = _silu(pg[:, 512:768])
    za = _dot(ps[:, 0:LANE].astype(BF16), wa2_ref[...]) + ba_ref[...]
    p_ref[:, 768:896] = -_softplus(-za) * (1.0 / GLA_TAU)
    p_ref[:, g0 + 1280:g0 + 1536] = -jnp.exp(alog_ref[...]) * _softplus(ps[:, LANE:LANE + 256] + dtb_ref[...])
    p_ref[:, g0 + 1536:g0 + 1792] = _sigmoid(ps[:, LANE + 256:LANE + 512])
    p_ref[:, P_CM:P_CM + 256] = _gelu_tanh(pr[:, 0:256])
    gv = _gelu_tanh(pr[:, 256:512])
    mu = jnp.mean(gv, axis=-1, keepdims=True)
    d = gv - mu
    var = jnp.mean(d * d, axis=-1, keepdims=True)
    p_ref[:, P_CM + 256:P_CM + 512] = d * lax.rsqrt(var + EPS) * lng_ref[...] + lnb_ref[...]


def _inproj(h, first_tile, rows, hist_s, nw, wgla, wgdn, wrest, wsmall, wa2, ba, alog, dtb, lng, lnb, cw, *,
            layer, sample, seq_len, seg):
    tm = TOKEN_TILE
    ntiles = rows // tm
    cols = lambda span: span[1] - span[0]
    per_layer = lambda *shape: _layer_spec(shape, layer)
    tile = lambda n: pl.BlockSpec((tm, n), lambda i: (i, 0))
    if sample:
        hist = [hist_s]
        hist_spec = [pl.BlockSpec((None, tm, CONV_W), lambda i: (layer, i, 0))]
        raw_spec, raw_shape, scratch = tile(CONV_W), (rows, CONV_W), []
    else:
        hist, hist_spec = [], []
        raw_spec, raw_shape = pl.BlockSpec((None, HIST, CONV_W), lambda i: (i, 0, 0)), (ntiles, HIST, CONV_W)
        scratch = [pltpu.VMEM((HIST, CONV_W), F32)]
    return pl.pallas_call(
        functools.partial(_inproj_kernel, sample=sample, tiles_per_seq=seq_len // tm, seg=seg),
        grid=(ntiles,),
        in_specs=[pl.BlockSpec((tm, D_MODEL), lambda i: (first_tile + i, 0)), *hist_spec, per_layer(1, D_MODEL),
                  per_layer(D_MODEL, cols(W_IN_GLA)), per_layer(D_MODEL, cols(W_IN_GDN)),
                  per_layer(D_MODEL, cols(W_IN_REST)), per_layer(D_MODEL, SMALL_COLS),
                  per_layer(LANE, GLA_QK), per_layer(1, GLA_QK), per_layer(1, 256),
                  per_layer(1, 256), per_layer(1, 256), per_layer(1, 256), per_layer(GDN_CONV, CONV_W)],
        out_specs=[tile(P_COLS), raw_spec],
        out_shape=[jax.ShapeDtypeStruct((rows, P_COLS), F32), jax.ShapeDtypeStruct(raw_shape, F32)],
        scratch_shapes=scratch,
        compiler_params=pltpu.CompilerParams(dimension_semantics=("arbitrary",), vmem_limit_bytes=VMEM_LIMIT),
        name="inproj_sample" if sample else "inproj_prompt",
    )(h, *hist, nw, wgla, wgdn, wrest, wsmall, wa2, ba, alog, dtb, lng, lnb, cw)


def _stack_heads(x, group, nheads, period=None):
    w = x.shape[1]
    li = lax.broadcasted_iota(jnp.int32, (1, w), 1)
    if period is not None:
        li = _imod(li, period)
    hid = _idiv(li, group)
    zero = jnp.zeros_like(x)
    return jnp.concatenate([jnp.where(hid == h, x, zero) for h in range(nheads)], axis=0)


def _block_diag(x, nblocks):
    r = x.shape[0]
    ri, ci = _iota2((nblocks * r, nblocks * r))
    return jnp.where(_idiv(ri, r) == _idiv(ci, r), jnp.concatenate([x] * nblocks, axis=0), jnp.zeros((), x.dtype))


def _widen(x, nseq):
    if nseq == 1:
        return x
    seg = ROWS // nseq
    sid = _idiv(lax.broadcasted_iota(jnp.int32, (ROWS, 1), 0), seg)
    zero = jnp.zeros_like(x)
    return jnp.concatenate([jnp.where(sid == j, x, zero) for j in range(nseq)], axis=1)


def _group_mean(x, group):
    w = x.shape[1]
    ri, ci = _iota2((w, w))
    avg = jnp.where(_idiv(ri, group) == _idiv(ci, group), 1.0 / group, 0.0).astype(BF16)
    return _dot(x.astype(BF16), avg)


def _head_diag_mask(rows, cols, rhead, chead, rper):
    ri, ci = _iota2((rows, cols))
    return (_idiv(_imod(ri, rper), rhead) == _idiv(ci, chead)).astype(F32)


def _expand_state(s, width, reps):
    return jnp.concatenate([s] * reps, axis=1)


def _compact_state(st, width, reps):
    out = st[:, 0:width]
    for h in range(1, reps):
        out = out + st[:, h * width:(h + 1) * width]
    return out


class _Masks:
    def __init__(self, seg):
        ri, ci = _iota2((ROWS, ROWS))
        self.same = _idiv(ri, seg) == _idiv(ci, seg)
        self.tri = self.same & (ci <= ri)
        rl, cl = _iota2((ROWS, GDN_HEADS * ROWS))
        cl = _imod(cl, ROWS)
        same_l = _idiv(rl, seg) == _idiv(cl, seg)
        self.tri_l = same_l & (cl <= rl)
        self.strict_l = same_l & (cl < rl)
        self.eye_l = (rl == cl).astype(F32)
        self.levels = [(_idiv(rl, 2 * s) == _idiv(cl, 2 * s)) & (_idiv(rl, s) != _idiv(cl, s))
                       for s in (1 << k for k in range(seg.bit_length() - 1))]
        self.ones = jnp.ones((ROWS, ROWS), BF16)
        self.seg = seg
        nseq = ROWS // seg
        self.bd_gla = _head_diag_mask(nseq * GLA_QK, GLA_HEADS * GLA_DV, GLA_DK, GLA_DV, GLA_QK)
        self.bd_gdn = _head_diag_mask(nseq * GDN_QK, GDN_QK, GDN_DK, GDN_DV, GDN_QK)


def _gla_prep(p, m, nseq):
    q, k, v, la = p[:, 0:128], p[:, 128:256], p[:, 256:512], p[:, 768:896]
    b = _mask_dot(m.tri, la, SUM_PIECES)
    if m.seg == ROWS:
        btot = jnp.broadcast_to(b[ROWS - 1:ROWS], b.shape)
    else:
        btot = _mask_dot(m.same, la, SUM_PIECES)
    qd = q * jnp.exp(b)
    kd = k * jnp.exp(-b)
    ke = k * jnp.exp(btot - b)
    a = jnp.where(m.tri_l, _dot1(qd, _stack_heads(kd, GLA_DK, GLA_HEADS), _dot_nt), 0.0)
    o_intra = _dot1(a, _stack_heads(v, GLA_DV, GLA_HEADS))
    dec = jnp.exp(_dot_tn(jnp.concatenate(_split(_widen(la, nseq), SUM_PIECES), axis=0),
                          jnp.ones((SUM_PIECES * ROWS, GLA_HEADS * GLA_DV), BF16)))
    ds = _dot1(_widen(ke, nseq), v, _dot_tn) * m.bd_gla
    return o_intra, _widen(qd, nseq), dec, ds


def _gla_scan(preps, sts):
    outs = [prep[0] + _dot1(prep[1], st) for prep, st in zip(preps, sts)]
    return outs, [st * prep[2] + prep[3] for prep, st in zip(preps, sts)]


def _gla_out(o, gn, rs):
    return o * lax.rsqrt(_group_mean(o * o, GLA_DV) + EPS) * gn * rs


def _gdn_qk_norm(qkv):
    hd = GDN_DK
    r = qkv.shape[0]
    cqk = jnp.concatenate([qkv[:, 0:256], qkv[:, 256:512]], axis=0)
    nrm = lax.rsqrt(_group_mean(cqk * cqk, hd) * hd + EPS)
    return qkv[:, 0:256] * nrm[0:r] * (hd ** -0.5), qkv[:, 256:512] * nrm[r:2 * r]


def _gdn_prep(q, k, cv, gs, betas, m, seg):
    nh, hd = GDN_HEADS, GDN_DK
    n = range(len(q))
    gc = [_mask_dot(m.tri, gs[i], SUM_PIECES) for i in n]
    if seg == ROWS:
        gtot = [jnp.broadcast_to(gc[i][ROWS - 1:ROWS], gc[i].shape) for i in n]
    else:
        gtot = [_mask_dot(m.same, gs[i], SUM_PIECES) for i in n]
    qkk = [_dot1(jnp.concatenate([q[i], k[i]], axis=0), _stack_heads(k[i], hd, nh), _dot_nt) for i in n]
    grow = [_mask_dot(m.ones, gc[i] * m.eye_l, SUM_PIECES) for i in n]
    decay = [jnp.where(m.tri_l, jnp.exp(jnp.where(m.tri_l, gc[i] - grow[i], 0.0)), 0.0) for i in n]
    amat = [jnp.where(m.strict_l, betas[i] * decay[i] * qkk[i][ROWS:2 * ROWS], 0.0) for i in n]
    inv = [m.eye_l - jnp.where(m.levels[0], amat[i], 0.0) for i in n]
    for lvl in m.levels[1:]:
        low = [_block_diag(jnp.where(lvl, amat[i], 0.0).astype(BF16), nh) for i in n]
        prod = [_dot(inv[i].astype(BF16), low[i]) for i in n]
        inv = [inv[i] - _dot(prod[i].astype(BF16), _block_diag(inv[i].astype(BF16), nh)) for i in n]
    eg = [jnp.exp(gc[i]) for i in n]
    rhs = [jnp.concatenate([betas[i] * cv[i], betas[i] * eg[i] * k[i]], axis=1) for i in n]
    uw = [_dot(inv[i].astype(BF16), _stack_heads(rhs[i].astype(BF16), hd, nh, period=GDN_QK)) for i in n]
    return [(uw[i][:, 0:256], uw[i][:, 256:512], q[i] * eg[i], qkk[i][0:ROWS] * decay[i],
             k[i] * jnp.exp(gtot[i] - gc[i]), jnp.exp(gtot[i])) for i in n]


def _gdn_scan(preps, sts, m, nseq):
    seg = ROWS // nseq
    n = range(len(preps))
    ws = [_dot1(jnp.concatenate([_widen(preps[i][1], nseq), _widen(preps[i][2], nseq)], axis=0), sts[i]) for i in n]
    u = [preps[i][0] - ws[i][0:ROWS] for i in n]
    outs = [ws[i][ROWS:2 * ROWS] + _dot1(preps[i][3], _stack_heads(u[i], GDN_DV, GDN_HEADS)) for i in n]
    new = []
    for i in n:
        dn = preps[i][5]
        dn_tall = jnp.concatenate(
            [jnp.broadcast_to(dn[j * seg:j * seg + 1], (GDN_QK, GDN_QK)) for j in range(nseq)], axis=0)
        new.append(sts[i] * dn_tall + _dot1(_widen(preps[i][4], nseq), u[i], _dot_tn) * m.bd_gdn)
    return outs, new


def _gdn_out(o, gn, zs):
    return o * lax.rsqrt(_group_mean(o * o, GDN_DV) + EPS) * gn * zs


def _cm_block(p, ws, bias, seg):
    r = p.shape[0]
    gu, vn = p[:, 0:256], p[:, 256:512]
    ri, ci = _iota2((r, CM_GROUPS * r))
    ci = _imod(ci, r)
    wm = jnp.where((_idiv(ri, seg) == _idiv(ci, seg)) & (ci <= ri), ws, 0.0)
    return gu * (_dot1(wm, _stack_heads(vn, BRANCH_W // CM_GROUPS, CM_GROUPS)) + bias)


def _conv_taps(cw_ref):
    return [cw_ref[GDN_CONV - 1 - d:GDN_CONV - d, :] for d in range(GDN_CONV)]


def _conv_prompt(x, hist, taps):
    n = x.shape[0]
    t8 = lax.broadcasted_iota(jnp.int32, (HIST, 1), 0)
    acc = taps[0] * x
    for d in range(1, GDN_CONV):
        xr = pltpu.roll(x, d, 0)
        head = jnp.where(t8 < d, pltpu.roll(hist, d, 0), xr[0:HIST])
        acc = acc + taps[d] * jnp.concatenate([head, xr[HIST:n]], axis=0)
    return acc


def _conv_sample(x, hist, taps, seg):
    n = x.shape[0]
    tloc = _imod(lax.broadcasted_iota(jnp.int32, (n, 1), 0), seg)
    acc = taps[0] * x
    for d in range(1, GDN_CONV):
        prev = jnp.where(tloc < d, pltpu.roll(hist, (d - seg) % n, 0), pltpu.roll(x, d, 0))
        acc = acc + taps[d] * prev
    return acc


def _block_rows(c):
    return slice(c * ROWS, (c + 1) * ROWS)


def _mixer_prompt_kernel(*refs):
    ns, nc, sr = PROMPT_SEQS, PROMPT_CHUNKS, PROMPT_SEQ_ROWS
    p_refs = refs[:ns]
    gn_gla_ref, gn_gdn_ref, cmw_ref, cmb_ref, o_ref, sgla_ref, sgdn_ref, st_gla, st_gdn = refs[ns:]

    @pl.when(pl.program_id(1) == 0)
    def _():
        st_gla[...] = jnp.zeros_like(st_gla)
        st_gdn[...] = jnp.zeros_like(st_gdn)

    m = _Masks(ROWS)
    g0 = P_GDN
    for s in range(ns):
        o_ref[s, :, 768:1024] = p_refs[s][:, g0 + 768:g0 + 1024]
    qkv = jnp.concatenate([p_refs[s][:, g0:g0 + 768] for s in range(ns)], axis=0)
    qn, kn = _gdn_qk_norm(qkv)
    blocks = [(s, c) for c in range(nc) for s in range(ns)]
    rows_of = lambda s, c: slice(s * sr + c * ROWS, s * sr + (c + 1) * ROWS)
    gdn = _gdn_prep([qn[rows_of(s, c)] for s, c in blocks], [kn[rows_of(s, c)] for s, c in blocks],
                    [qkv[rows_of(s, c), 512:768] for s, c in blocks],
                    [p_refs[s][_block_rows(c), g0 + 1280:g0 + 1536] for s, c in blocks],
                    [p_refs[s][_block_rows(c), g0 + 1536:g0 + 1792] for s, c in blocks], m, ROWS)
    gla = [_gla_prep(p_refs[s][_block_rows(c), P_GLA:P_GLA + 896], m, 1) for s, c in blocks]
    sg = [st_gla[s] for s in range(ns)]
    sd = [st_gdn[s] for s in range(ns)]
    og, od = {}, {}
    for c in range(nc):
        o, sg = _gla_scan(gla[c * ns:(c + 1) * ns], sg)
        og.update({(s, c): o[s] for s in range(ns)})
        o, sd = _gdn_scan(gdn[c * ns:(c + 1) * ns], sd, m, 1)
        od.update({(s, c): o[s] for s in range(ns)})
    by_rows = lambda d: jnp.concatenate([d[(s, c)] for s in range(ns) for c in range(nc)], axis=0)
    rs = jnp.concatenate([p_refs[s][:, 512:768] for s in range(ns)], axis=0)
    zs = jnp.concatenate([p_refs[s][:, g0 + 1024:g0 + 1280] for s in range(ns)], axis=0)
    o_gla = _gla_out(by_rows(og), gn_gla_ref[...], rs)
    o_gdn = _gdn_out(by_rows(od), gn_gdn_ref[...], zs)
    for s in range(ns):
        st_gla[s] = sg[s]
        st_gdn[s] = sd[s]
        o_ref[s, :, 0:256] = o_gla[s * sr:(s + 1) * sr]
        o_ref[s, :, 256:512] = o_gdn[s * sr:(s + 1) * sr]
        for c in range(sr // CM_CHUNK):
            rows = slice(c * CM_CHUNK, (c + 1) * CM_CHUNK)
            o_ref[s, rows, 512:768] = _cm_block(p_refs[s][rows, P_CM:P_CM + 512], cmw_ref[...], cmb_ref[...], CM_CHUNK)

    @pl.when(pl.program_id(1) == pl.num_programs(1) - 1)
    def _():
        for s in range(ns):
            sgla_ref[s] = _compact_state(sg[s], GLA_DV, GLA_HEADS)
            sgdn_ref[s] = _compact_state(sd[s], GDN_DV, GDN_HEADS)


def _mixer_prompt(p, gn_gla, gn_gdn, cmw, cmb, *, layer, nseqs, seq_len):
    ns, sr = PROMPT_SEQS, PROMPT_SEQ_ROWS
    nsteps = seq_len // sr
    seq_spec = lambda s: pl.BlockSpec((sr, P_COLS), lambda o, c: ((o * ns + s) * nsteps + c, 0))
    per_group = lambda *shape: pl.BlockSpec((ns,) + shape, lambda o, c: (o,) + (0,) * len(shape))
    per_layer = lambda *shape: _layer_spec(shape, layer)
    return pl.pallas_call(
        _mixer_prompt_kernel,
        grid=(nseqs // ns, nsteps),
        in_specs=[seq_spec(s) for s in range(ns)] + [
            per_layer(1, 256), per_layer(1, 256),
            per_layer(CM_CHUNK, CM_GROUPS * CM_CHUNK), per_layer(CM_CHUNK, 256)],
        out_specs=[pl.BlockSpec((ns, sr, 4 * BRANCH_W), lambda o, c: (o, c, 0)),
                   per_group(GLA_QK, GLA_DV), per_group(GDN_QK, GDN_DV)],
        out_shape=[jax.ShapeDtypeStruct((nseqs, seq_len, 4 * BRANCH_W), F32),
                   jax.ShapeDtypeStruct((nseqs, GLA_QK, GLA_DV), F32),
                   jax.ShapeDtypeStruct((nseqs, GDN_QK, GDN_DV), F32)],
        scratch_shapes=[pltpu.VMEM((ns, GLA_QK, GLA_HEADS * GLA_DV), F32), pltpu.VMEM((ns, GDN_QK, GDN_QK), F32)],
        compiler_params=pltpu.CompilerParams(dimension_semantics=("arbitrary", "arbitrary"),
                                             vmem_limit_bytes=VMEM_LIMIT),
        name="mixer_prompt",
    )(*([p] * ns), gn_gla, gn_gdn, cmw, cmb)


def _mixer_sample_kernel(p_ref, s0gla_ref, s0gdn_ref, gn_gla_ref, gn_gdn_ref, cmw_ref, cmb_ref,
                         o_ref, sgla_ref, sgdn_ref, *, seg):
    nseq = ROWS // seg
    nb = range(SAMPLE_BLOCKS_PER_STEP)
    m = _Masks(seg)
    g0 = P_GDN
    o_ref[:, 768:1024] = p_ref[:, g0 + 768:g0 + 1024]
    qkv = p_ref[:, g0:g0 + 768]
    qn, kn = _gdn_qk_norm(qkv)
    gdn = _gdn_prep([qn[_block_rows(c)] for c in nb], [kn[_block_rows(c)] for c in nb],
                    [qkv[_block_rows(c), 512:768] for c in nb],
                    [p_ref[_block_rows(c), g0 + 1280:g0 + 1536] for c in nb],
                    [p_ref[_block_rows(c), g0 + 1536:g0 + 1792] for c in nb], m, seg)
    gla = [_gla_prep(p_ref[_block_rows(c), P_GLA:P_GLA + 896], m, nseq) for c in nb]
    seqs = lambda c: slice(c * nseq, (c + 1) * nseq)
    sg = [_expand_state(s0gla_ref[seqs(c)].reshape(nseq * GLA_QK, GLA_DV), GLA_DV, GLA_HEADS) * m.bd_gla for c in nb]
    sd = [_expand_state(s0gdn_ref[seqs(c)].reshape(nseq * GDN_QK, GDN_DV), GDN_DV, GDN_HEADS) * m.bd_gdn for c in nb]
    og, sg = _gla_scan(gla, sg)
    od, sd = _gdn_scan(gdn, sd, m, nseq)
    for c in nb:
        rows = _block_rows(c)
        sgla_ref[seqs(c)] = _compact_state(sg[c], GLA_DV, GLA_HEADS).reshape(nseq, GLA_QK, GLA_DV)
        sgdn_ref[seqs(c)] = _compact_state(sd[c], GDN_DV, GDN_HEADS).reshape(nseq, GDN_QK, GDN_DV)
        o_ref[rows, 512:768] = _cm_block(p_ref[rows, P_CM:P_CM + 512], cmw_ref[...], cmb_ref[...], seg)
    o_ref[:, 0:256] = _gla_out(jnp.concatenate(og, axis=0), gn_gla_ref[...], p_ref[:, 512:768])
    o_ref[:, 256:512] = _gdn_out(jnp.concatenate(od, axis=0), gn_gdn_ref[...], p_ref[:, g0 + 1024:g0 + 1280])


def _mixer_sample(p, s0gla, s0gdn, gn_gla, gn_gdn, cmw, cmb, *, layer, base_step, nsteps, seg):
    rows = SAMPLE_STEP_ROWS
    nseq_step = rows // seg
    per_layer = lambda *shape: _layer_spec(shape, layer)
    return pl.pallas_call(
        functools.partial(_mixer_sample_kernel, seg=seg),
        grid=(nsteps,),
        in_specs=[pl.BlockSpec((rows, P_COLS), lambda i: (base_step + i, 0)),
                  pl.BlockSpec((None, nseq_step, GLA_QK, GLA_DV), lambda i: (layer, i, 0, 0)),
                  pl.BlockSpec((None, nseq_step, GDN_QK, GDN_DV), lambda i: (layer, i, 0, 0)),
                  per_layer(1, 256), per_layer(1, 256),
                  per_layer(ROWS, CM_GROUPS * ROWS), per_layer(ROWS, 256)],
        out_specs=[pl.BlockSpec((rows, 4 * BRANCH_W), lambda i: (i, 0)),
                   pl.BlockSpec((nseq_step, GLA_QK, GLA_DV), lambda i: (i, 0, 0)),
                   pl.BlockSpec((nseq_step, GDN_QK, GDN_DV), lambda i: (i, 0, 0))],
        out_shape=[jax.ShapeDtypeStruct((nsteps * rows, 4 * BRANCH_W), F32),
                   jax.ShapeDtypeStruct((nsteps * nseq_step, GLA_QK, GLA_DV), F32),
                   jax.ShapeDtypeStruct((nsteps * nseq_step, GDN_QK, GDN_DV), F32)],
        compiler_params=pltpu.CompilerParams(dimension_semantics=("parallel",), vmem_limit_bytes=VMEM_LIMIT),
        name="mixer_sample",
    )(p, s0gla, s0gdn, gn_gla, gn_gdn, cmw, cmb)


def _merge_kernel(*refs, prompt_tiles, pair):
    nh = 2 if pair else 1
    bp_ref, bs_ref, nw_ref, wg_ref, wb_ref, wo_ref, o_ref = refs[nh:]
    h = _residual_rows(refs[:nh], prompt_tiles)
    xn = _rms(h, nw_ref[...]).astype(BF16)
    br = _stream_rows(bp_ref, bs_ref, prompt_tiles).astype(BF16)
    merged = None
    for gi in range(N_BRANCH):
        gate = _sigmoid(_dot(xn, wg_ref[:, gi * D_MODEL:(gi + 1) * D_MODEL]))
        term = _dot(br[:, gi * BRANCH_W:(gi + 1) * BRANCH_W], wb_ref[gi]) * gate
        merged = term if merged is None else merged + term
    o_ref[...] = h + _dot(merged.astype(BF16), wo_ref[...])


def _stream_specs(tm, width, prompt_tiles, lead=()):
    nlead = (None,) * len(lead)
    return (pl.BlockSpec(nlead + (tm, width), lambda i: lead + (jnp.minimum(i, prompt_tiles - 1), 0)),
            pl.BlockSpec(nlead + (tm, width), lambda i: lead + (jnp.maximum(i - prompt_tiles, 0), 0)))


def _merge(h, br_p, br_s, nw, wg, wb, wo, *, layer):
    tm = TOKEN_TILE
    h_arrays, h_specs, _, ntok = _residual_specs(h, tm)
    prompt_tiles = br_p.shape[0] // tm
    row = lambda n: pl.BlockSpec((tm, n), lambda i: (i, 0))
    return pl.pallas_call(
        functools.partial(_merge_kernel, prompt_tiles=prompt_tiles, pair=isinstance(h, tuple)),
        grid=(ntok // tm,),
        in_specs=[*h_specs, *_stream_specs(tm, 4 * BRANCH_W, prompt_tiles), _layer_spec((1, D_MODEL), layer),
                  _layer_spec((D_MODEL, N_BRANCH * D_MODEL), layer),
                  _layer_spec((N_BRANCH, BRANCH_W, D_MODEL), layer), _layer_spec((D_MODEL, D_MODEL), layer)],
        out_specs=row(D_MODEL),
        out_shape=jax.ShapeDtypeStruct((ntok, D_MODEL), F32),
        compiler_params=pltpu.CompilerParams(dimension_semantics=("parallel",), vmem_limit_bytes=VMEM_LIMIT),
        name="merge",
    )(*h_arrays, br_p, br_s, nw, wg, wb, wo)


def _ffn_kernel(h_ref, pp_ref, ps_ref, nf_ref, wfg_ref, wfu_ref, wfd_ref, np_ref, wpg_ref, wp_ref, nfin_ref,
                *o_refs, final, prompt_tiles):
    h = h_ref[...]
    xf = _rms(h, nf_ref[...]).astype(BF16)
    act = _silu(_dot(xf, wfg_ref[...])) * _dot(xf, wfu_ref[...])
    h = h + _dot(act.astype(BF16), wfd_ref[...])
    pg = _sigmoid(_dot(_rms(h, np_ref[...]).astype(BF16), wpg_ref[...]))
    pe = _stream_rows(pp_ref, ps_ref, prompt_tiles).astype(BF16)
    h = h + pg * _dot(pe, wp_ref[...])
    if not final:
        o_refs[0][...] = h
        return
    out = _rms(h, nfin_ref[...])
    op_ref, os_ref = o_refs

    @pl.when(pl.program_id(0) < prompt_tiles)
    def _():
        op_ref[...] = out

    @pl.when(pl.program_id(0) >= prompt_tiles)
    def _():
        os_ref[...] = out


def _ffn(h, pe_p, pe_s, nf, wfg, wfu, wfd, npl, wpg, wp, nfin, *, layer, final):
    ntok = h.shape[0]
    tm = TOKEN_TILE
    npt = pe_p.shape[1]
    prompt_tiles = npt // tm
    row = lambda n: pl.BlockSpec((tm, n), lambda i: (i, 0))
    once = lambda shape: _layer_spec(shape, layer, buffers=1)
    if final:
        out_specs = list(_stream_specs(tm, D_MODEL, prompt_tiles))
        out_shape = [jax.ShapeDtypeStruct((npt, D_MODEL), F32), jax.ShapeDtypeStruct((ntok - npt, D_MODEL), F32)]
    else:
        out_specs, out_shape = row(D_MODEL), jax.ShapeDtypeStruct((ntok, D_MODEL), F32)
    return pl.pallas_call(
        functools.partial(_ffn_kernel, final=final, prompt_tiles=prompt_tiles),
        grid=(ntok // tm,),
        in_specs=[row(D_MODEL), *_stream_specs(tm, PLE_DIM, prompt_tiles, lead=(layer,)),
                  _layer_spec((1, D_MODEL), layer),
                  once((D_MODEL, D_FF)), once((D_MODEL, D_FF)), once((D_FF, D_MODEL)),
                  _layer_spec((1, D_MODEL), layer), once((D_MODEL, D_MODEL)), once((PLE_DIM, D_MODEL)),
                  _const_spec((1, D_MODEL))],
        out_specs=out_specs,
        out_shape=out_shape,
        compiler_params=pltpu.CompilerParams(dimension_semantics=("arbitrary",), vmem_limit_bytes=VMEM_LIMIT),
        name="ffn_final" if final else "ffn",
    )(h, pe_p, pe_s, nf, wfg, wfu, wfd, npl, wpg, wp, nfin)


def _split_w_in(w_in):
    span = lambda s: w_in[..., s[0]:s[1]].astype(BF16)
    rep = lambda o: jnp.repeat(w_in[..., o:o + GDN_HEADS], GDN_DK, axis=-1)
    small = jnp.concatenate([w_in[..., W_IN_GA:W_IN_GA + GLA_RANK],
                             jnp.zeros(w_in.shape[:-1] + (LANE - GLA_RANK,), w_in.dtype),
                             rep(W_IN_DA), rep(W_IN_DB)], axis=-1).astype(BF16)
    return span(W_IN_GLA), span(W_IN_GDN), span(W_IN_REST), small


def kernel(x_prompt, x_sample, state_gla, state_gdn, state_gdn_conv, state_sconv, p_prompt, p_sample, norm_mix, w_in, gla_wa2, gla_ba, gla_norm, gdn_conv_w, gdn_a_log, gdn_dt_bias, gdn_norm, cm_ln_g, cm_ln_b, cm_ws, cm_bs, sc_conv_w, w_gate, w_branch, w_o, norm_ffn, w_ffn_gate, w_ffn_up, w_ffn_down, norm_ple, w_ple_gate, w_ple, norm_final):
    depth = w_in.shape[0]
    bp, tp, _ = x_prompt.shape
    bs, ts, _ = x_sample.shape
    npt, nst = bp * tp, bs * ts
    sseq = ROWS // ts
    assert ts == HIST and nst % SAMPLE_STEP_ROWS == 0
    assert bp % PROMPT_SEQS == 0 and tp % PROMPT_SEQ_ROWS == 0 and PROMPT_SEQ_ROWS % CM_CHUNK == 0
    assert tp % TOKEN_TILE == 0 and nst % TOKEN_TILE == 0 and TOKEN_TILE % ts == 0

    h = (x_prompt.reshape(npt, D_MODEL), x_sample.reshape(nst, D_MODEL))
    pe_p = p_prompt.reshape(depth, npt, PLE_DIM)
    pe_s = p_sample.reshape(depth, nst, PLE_DIM)
    s0_gla = state_gla.reshape(depth, bs, GLA_QK, GLA_DV)
    s0_gdn = state_gdn.reshape(depth, bs, GDN_QK, GDN_DV)
    rows = lambda a: a.reshape(depth, 1, -1)
    w_in_groups = _split_w_in(w_in)
    wg, wb, wo = w_gate.astype(BF16), w_branch.astype(BF16), w_o.astype(BF16)
    wfg, wfu, wfd = w_ffn_gate.astype(BF16), w_ffn_up.astype(BF16), w_ffn_down.astype(BF16)
    wpg, wp = w_ple_gate.astype(BF16), w_ple.astype(BF16)
    wa2 = jnp.pad(gla_wa2, ((0, 0), (0, LANE - GLA_RANK), (0, 0))).astype(BF16)
    nmix, nffn, nple = rows(norm_mix), rows(norm_ffn), rows(norm_ple)
    inproj_vecs = (rows(gla_ba), rows(jnp.repeat(gdn_a_log, GDN_DK, axis=1)),
                   rows(jnp.repeat(gdn_dt_bias, GDN_DK, axis=1)), rows(cm_ln_g), rows(cm_ln_b))
    cw = jnp.concatenate([gdn_conv_w, jnp.pad(sc_conv_w, ((0, 0), (GDN_CONV - SC_WIDTH, 0), (0, 0)))], axis=2)
    hist_s = jnp.concatenate([
        jnp.pad(state_gdn_conv, ((0, 0), (0, 0), (HIST - (GDN_CONV - 1), 0), (0, 0))),
        jnp.pad(state_sconv, ((0, 0), (0, 0), (HIST - (SC_WIDTH - 1), 0), (0, 0)))], axis=3).reshape(depth, nst, CONV_W)
    gn_gla = rows(jnp.tile(gla_norm, (1, GLA_HEADS)))
    gn_gdn = rows(jnp.tile(gdn_norm, (1, GDN_HEADS)))
    cmw_p = jnp.transpose(cm_ws, (0, 2, 1, 3)).reshape(depth, CM_CHUNK, CM_GROUPS * CM_CHUNK)
    cmb_p = jnp.repeat(jnp.swapaxes(cm_bs, 1, 2), BRANCH_W // CM_GROUPS, axis=2)
    cmw_s = jnp.transpose(jnp.tile(cm_ws[:, :, :ts, :ts], (1, 1, sseq, sseq)),
                          (0, 2, 1, 3)).reshape(depth, ROWS, CM_GROUPS * ROWS)
    cmb_s = jnp.tile(cmb_p[:, :ts], (1, sseq, 1))

    outs = {k: [] for k in ("gla_p", "gla_s", "gdn_p", "gdn_s", "gc_p", "gc_s", "sc_p", "sc_s", "cv_s")}
    for i in range(depth):
        h_p, h_s, first_s = (h[0], h[1], 0) if isinstance(h, tuple) else (h, h, npt // TOKEN_TILE)
        inproj_args = (hist_s, nmix, *w_in_groups, wa2, *inproj_vecs, cw)
        p_p, tails = _inproj(h_p, 0, npt, *inproj_args, layer=i, sample=False, seq_len=tp, seg=ts)
        p_s, xs_raw = _inproj(h_s, first_s, nst, *inproj_args, layer=i, sample=True, seq_len=tp, seg=ts)
        br_p, gla_p, gdn_p = _mixer_prompt(p_p, gn_gla, gn_gdn, cmw_p, cmb_p, layer=i, nseqs=bp, seq_len=tp)
        br_p = br_p.reshape(npt, N_BRANCH * BRANCH_W)
        br_s, gla_s, gdn_s = _mixer_sample(p_s, s0_gla, s0_gdn, gn_gla, gn_gdn, cmw_s, cmb_s, layer=i,
                                           base_step=0, nsteps=nst // SAMPLE_STEP_ROWS, seg=ts)
        outs["gla_p"].append(gla_p.reshape(bp, GLA_HEADS, GLA_DK, GLA_DV))
        outs["gla_s"].append(gla_s.reshape(bs, GLA_HEADS, GLA_DK, GLA_DV))
        outs["gdn_p"].append(gdn_p.reshape(bp, GDN_HEADS, GDN_DK, GDN_DV))
        outs["gdn_s"].append(gdn_s.reshape(bs, GDN_HEADS, GDN_DK, GDN_DV))
        tiles_per_seq = tp // TOKEN_TILE
        tail_p = tails[tiles_per_seq - 1:bp * tiles_per_seq:tiles_per_seq]
        xs3 = xs_raw.reshape(bs, ts, CONV_W)
        outs["gc_p"].append(tail_p[:, HIST - (GDN_CONV - 1):, 0:768])
        outs["gc_s"].append(xs3[:, ts - (GDN_CONV - 1):, 0:768])
        outs["sc_p"].append(tail_p[:, HIST - (SC_WIDTH - 1):, 768:])
        outs["sc_s"].append(xs3[:, ts - (SC_WIDTH - 1):, 768:])
        outs["cv_s"].append(p_s[:, P_CM + 256:P_CM + 512].reshape(bs, ts, BRANCH_W))

        h1 = _merge(h, br_p, br_s, nmix, wg, wb, wo, layer=i)
        h = _ffn(h1, pe_p, pe_s, nffn, wfg, wfu, wfd, nple, wpg, wp, norm_final.reshape(1, D_MODEL),
                 layer=i, final=(i == depth - 1))

    y_prompt = h[0].reshape(bp, tp, D_MODEL)
    y_sample = h[1].reshape(bs, ts, D_MODEL)
    st = lambda k: jnp.stack(outs[k])
    return (y_prompt, y_sample, st("gla_p"), st("gla_s"), st("gdn_p"), st("gdn_s"),
            st("gc_p"), st("gc_s"), st("sc_p"), st("sc_s"), st("cv_s"))
```

```python
import functools

import jax
import jax.numpy as jnp
from jax import lax
from jax.experimental import pallas as pl
from jax.experimental.pallas import tpu as pltpu

F32 = jnp.float32
BF16 = jnp.bfloat16

D_MODEL = 1024
PLE_DIM = 256
BRANCH_W = 256
N_BRANCH = 4
GLA_HEADS = 4
GLA_DK = 32
GLA_DV = 64
GLA_RANK = 16
GLA_TAU = 16.0
GDN_HEADS = 4
GDN_DK = 64
GDN_DV = 64
GDN_CONV = 4
CM_GROUPS = 4
CM_CHUNK = 128
SC_WIDTH = 3
D_FF = 2816
EPS = 1e-6

ROWS = 64
PROMPT_SEQS = 8
PROMPT_CHUNKS = 2
PROMPT_SEQ_ROWS = ROWS * PROMPT_CHUNKS
SAMPLE_BLOCKS_PER_STEP = 2
SAMPLE_STEP_ROWS = ROWS * SAMPLE_BLOCKS_PER_STEP
GLA_QK = GLA_HEADS * GLA_DK
GDN_QK = GDN_HEADS * GDN_DK
LANE = 128
CONV_W = 3 * BRANCH_W + BRANCH_W
HIST = 8
SUM_PIECES = 2

W_IN_GLA, W_IN_GDN, W_IN_REST = (0, 768), (784, 1808), (1816, 3096)
W_IN_GA, W_IN_DA, W_IN_DB = 768, 1808, 1812
SMALL_COLS = LANE + 2 * 256
P_GLA = 0
P_GDN = 896
P_CM = 2688
P_COLS = 3200

V7X_VMEM_BYTES = 64 * 1024 * 1024
VMEM_LIMIT = V7X_VMEM_BYTES * 7 // 8
TOKEN_TILE = 512


def _dot(a, b):
    return jnp.dot(a, b, preferred_element_type=F32)


def _dot_nt(a, b):
    return lax.dot_general(a, b, (((1,), (1,)), ((), ())), preferred_element_type=F32)


def _dot_tn(a, b):
    return lax.dot_general(a, b, (((0,), (0,)), ((), ())), preferred_element_type=F32)


def _split(x, n):
    parts, r = [], x
    for i in range(n):
        p = r.astype(BF16)
        parts.append(p)
        if i + 1 < n:
            r = r - p.astype(F32)
    return parts


def _dot1(a, b, dot=_dot):
    return dot(a.astype(BF16), b.astype(BF16))


def _mask_dot(mask, x, n):
    return _dot(jnp.concatenate([mask.astype(BF16)] * n, axis=1), jnp.concatenate(_split(x, n), axis=0))


def _sigmoid(x):
    return 1.0 / (1.0 + jnp.exp(-x))


def _silu(x):
    return x * _sigmoid(x)


def _softplus(x):
    return jnp.maximum(x, 0.0) + jnp.log(1.0 + jnp.exp(-jnp.abs(x)))


def _gelu_tanh(x):
    return 0.5 * x * (1.0 + jnp.tanh(0.7978845608028654 * (x + 0.044715 * (x * x * x))))


def _rms(x, w):
    return x * lax.rsqrt(jnp.mean(x * x, axis=-1, keepdims=True) + EPS) * w


def _idiv(x, n):
    assert n & (n - 1) == 0
    return lax.shift_right_logical(x, n.bit_length() - 1)


def _imod(x, n):
    assert n & (n - 1) == 0
    return lax.bitwise_and(x, n - 1)


def _const_spec(shape):
    return pl.BlockSpec(shape, lambda *_: (0,) * len(shape))


def _layer_spec(shape, layer, buffers=None):
    mode = {} if buffers is None else {"pipeline_mode": pl.Buffered(buffers)}
    return pl.BlockSpec((None,) + shape, lambda *_: (layer,) + (0,) * len(shape), **mode)


def _iota2(shape):
    return lax.broadcasted_iota(jnp.int32, shape, 0), lax.broadcasted_iota(jnp.int32, shape, 1)


def _stream_rows(hp_ref, hs_ref, prompt_tiles):
    return jnp.where(pl.program_id(0) < prompt_tiles, hp_ref[...], hs_ref[...])


def _residual_rows(h_refs, prompt_tiles):
    return h_refs[0][...] if len(h_refs) == 1 else _stream_rows(*h_refs, prompt_tiles)


def _residual_specs(h, tm):
    if isinstance(h, tuple):
        prompt_tiles = h[0].shape[0] // tm
        return list(h), list(_stream_specs(tm, D_MODEL, prompt_tiles)), prompt_tiles, h[0].shape[0] + h[1].shape[0]
    return [h], [pl.BlockSpec((tm, D_MODEL), lambda i: (i, 0))], None, h.shape[0]


def _inproj_kernel(*refs, sample, tiles_per_seq, seg):
    if sample:
        (h_ref, hist_ref, nw_ref, wgla_ref, wgdn_ref, wrest_ref, wsmall_ref, wa2_ref, ba_ref, alog_ref,
         dtb_ref, lng_ref, lnb_ref, cw_ref, p_ref, xraw_ref) = refs
    else:
        (h_ref, nw_ref, wgla_ref, wgdn_ref, wrest_ref, wsmall_ref, wa2_ref, ba_ref, alog_ref,
         dtb_ref, lng_ref, lnb_ref, cw_ref, p_ref, xraw_ref, carry) = refs
    tm = p_ref.shape[0]
    xn = _rms(h_ref[...], nw_ref[...]).astype(BF16)
    pd = _dot(xn, wgdn_ref[...])
    pr = _dot(xn, wrest_ref[...])
    pg = _dot(xn, wgla_ref[...])
    ps = _dot(xn, wsmall_ref[...])
    g0 = P_GDN
    x = jnp.concatenate([pd[:, 0:768], pr[:, 1024:1280] * pr[:, 512:768]], axis=1)
    taps = _conv_taps(cw_ref)
    if sample:
        acc = _conv_sample(x, hist_ref[...], taps, seg)
        xraw_ref[...] = x
    else:
        hist = jnp.where(lax.rem(pl.program_id(0), tiles_per_seq) == 0, 0.0, carry[...])
        acc = _conv_prompt(x, hist, taps)
        carry[...] = x[tm - HIST:tm]
        xraw_ref[...] = x[tm - HIST:tm]
    p_ref[:, g0:g0 + 768] = _silu(acc[:, 0:768])
    p_ref[:, g0 + 768:g0 + 1024] = pr[:, 768:1024] * acc[:, 768:1024]
    p_ref[:, g0 + 1024:g0 + 1280] = _silu(pd[:, 768:1024])
    p_ref[:, 0:128] = pg[:, 0:128] * (GLA_DK ** -0.5)
    p_ref[:, 128:512] = pg[:, 128:512]
    p_ref[:, 512:768] = _silu(pg[:, 512:768])
    za = _dot(ps[:, 0:LANE].astype(BF16), wa2_ref[...]) + ba_ref[...]
    p_ref[:, 768:896] = -_softplus(-za) * (1.0 / GLA_TAU)
    p_ref[:, g0 + 1280:g0 + 1536] = -jnp.exp(alog_ref[...]) * _softplus(ps[:, LANE:LANE + 256] + dtb_ref[...])
    p_ref[:, g0 + 1536:g0 + 1792] = _sigmoid(ps[:, LANE + 256:LANE + 512])
    p_ref[:, P_CM:P_CM + 256] = _gelu_tanh(pr[:, 0:256])
    gv = _gelu_tanh(pr[:, 256:512])
    mu = jnp.mean(gv, axis=-1, keepdims=True)
    d = gv - mu
    var = jnp.mean(d * d, axis=-1, keepdims=True)
    p_ref[:, P_CM + 256:P_CM + 512] = d * lax.rsqrt(var + EPS) * lng_ref[...] + lnb_ref[...]


def _inproj(h, first_tile, rows, hist_s, nw, wgla, wgdn, wrest, wsmall, wa2, ba, alog, dtb, lng, lnb, cw, *,
            layer, sample, seq_len, seg):
    tm = TOKEN_TILE
    ntiles = rows // tm
    cols = lambda span: span[1] - span[0]
    per_layer = lambda *shape: _layer_spec(shape, layer)
    tile = lambda n: pl.BlockSpec((tm, n), lambda i: (i, 0))
    if sample:
        hist = [hist_s]
        hist_spec = [pl.BlockSpec((None, tm, CONV_W), lambda i: (layer, i, 0))]
        raw_spec, raw_shape, scratch = tile(CONV_W), (rows, CONV_W), []
    else:
        hist, hist_spec = [], []
        raw_spec, raw_shape = pl.BlockSpec((None, HIST, CONV_W), lambda i: (i, 0, 0)), (ntiles, HIST, CONV_W)
        scratch = [pltpu.VMEM((HIST, CONV_W), F32)]
    return pl.pallas_call(
        functools.partial(_inproj_kernel, sample=sample, tiles_per_seq=seq_len // tm, seg=seg),
        grid=(ntiles,),
        in_specs=[pl.BlockSpec((tm, D_MODEL), lambda i: (first_tile + i, 0)), *hist_spec, per_layer(1, D_MODEL),
                  per_layer(D_MODEL, cols(W_IN_GLA)), per_layer(D_MODEL, cols(W_IN_GDN)),
                  per_layer(D_MODEL, cols(W_IN_REST)), per_layer(D_MODEL, SMALL_COLS),
                  per_layer(LANE, GLA_QK), per_layer(1, GLA_QK), per_layer(1, 256),
                  per_layer(1, 256), per_layer(1, 256), per_layer(1, 256), per_layer(GDN_CONV, CONV_W)],
        out_specs=[tile(P_COLS), raw_spec],
        out_shape=[jax.ShapeDtypeStruct((rows, P_COLS), F32), jax.ShapeDtypeStruct(raw_shape, F32)],
        scratch_shapes=scratch,
        compiler_params=pltpu.CompilerParams(dimension_semantics=("arbitrary",), vmem_limit_bytes=VMEM_LIMIT),
        name="inproj_sample" if sample else "inproj_prompt",
    )(h, *hist, nw, wgla, wgdn, wrest, wsmall, wa2, ba, alog, dtb, lng, lnb, cw)


def _stack_heads(x, group, nheads, period=None):
    w = x.shape[1]
    li = lax.broadcasted_iota(jnp.int32, (1, w), 1)
    if period is not None:
        li = _imod(li, period)
    hid = _idiv(li, group)
    zero = jnp.zeros_like(x)
    return jnp.concatenate([jnp.where(hid == h, x, zero) for h in range(nheads)], axis=0)


def _block_diag(x, nblocks):
    r = x.shape[0]
    ri, ci = _iota2((nblocks * r, nblocks * r))
    return jnp.where(_idiv(ri, r) == _idiv(ci, r), jnp.concatenate([x] * nblocks, axis=0), jnp.zeros((), x.dtype))


def _widen(x, nseq):
    if nseq == 1:
        return x
    seg = ROWS // nseq
    sid = _idiv(lax.broadcasted_iota(jnp.int32, (ROWS, 1), 0), seg)
    zero = jnp.zeros_like(x)
    return jnp.concatenate([jnp.where(sid == j, x, zero) for j in range(nseq)], axis=1)


def _group_mean(x, group):
    w = x.shape[1]
    ri, ci = _iota2((w, w))
    avg = jnp.where(_idiv(ri, group) == _idiv(ci, group), 1.0 / group, 0.0).astype(BF16)
    return _dot(x.astype(BF16), avg)


def _head_diag_mask(rows, cols, rhead, chead, rper):
    ri, ci = _iota2((rows, cols))
    return (_idiv(_imod(ri, rper), rhead) == _idiv(ci, chead)).astype(F32)


def _expand_state(s, width, reps):
    return jnp.concatenate([s] * reps, axis=1)


def _compact_state(st, width, reps):
    out = st[:, 0:width]
    for h in range(1, reps):
        out = out + st[:, h * width:(h + 1) * width]
    return out


class _Masks:
    def __init__(self, seg):
        ri, ci = _iota2((ROWS, ROWS))
        self.same = _idiv(ri, seg) == _idiv(ci, seg)
        self.tri = self.same & (ci <= ri)
        rl, cl = _iota2((ROWS, GDN_HEADS * ROWS))
        cl = _imod(cl, ROWS)
        same_l = _idiv(rl, seg) == _idiv(cl, seg)
        self.tri_l = same_l & (cl <= rl)
        self.strict_l = same_l & (cl < rl)
        self.eye_l = (rl == cl).astype(F32)
        self.levels = [(_idiv(rl, 2 * s) == _idiv(cl, 2 * s)) & (_idiv(rl, s) != _idiv(cl, s))
                       for s in (1 << k for k in range(seg.bit_length() - 1))]
        self.seg = seg
        nseq = ROWS // seg
        self.bd_gla = _head_diag_mask(nseq * GLA_QK, GLA_HEADS * GLA_DV, GLA_DK, GLA_DV, GLA_QK)
        self.bd_gdn = _head_diag_mask(nseq * GDN_QK, GDN_QK, GDN_DK, GDN_DV, GDN_QK)


def _gla_prep(p, m, nseq):
    q, k, v, la = p[:, 0:128], p[:, 128:256], p[:, 256:512], p[:, 768:896]
    b = _mask_dot(m.tri, la, SUM_PIECES)
    if m.seg == ROWS:
        btot = jnp.broadcast_to(b[ROWS - 1:ROWS], b.shape)
    else:
        btot = _mask_dot(m.same, la, SUM_PIECES)
    qd = q * jnp.exp(b)
    kd = k * jnp.exp(-b)
    ke = k * jnp.exp(btot - b)
    a = jnp.where(m.tri_l, _dot1(qd, _stack_heads(kd, GLA_DK, GLA_HEADS), _dot_nt), 0.0)
    o_intra = _dot1(a, _stack_heads(v, GLA_DV, GLA_HEADS))
    dec = jnp.exp(_dot_tn(jnp.concatenate(_split(_widen(la, nseq), SUM_PIECES), axis=0),
                          jnp.ones((SUM_PIECES * ROWS, GLA_HEADS * GLA_DV), BF16)))
    ds = _dot1(_widen(ke, nseq), v, _dot_tn) * m.bd_gla
    return o_intra, _widen(qd, nseq), dec, ds


def _gla_scan(preps, sts):
    outs = [prep[0] + _dot1(prep[1], st) for prep, st in zip(preps, sts)]
    return outs, [st * prep[2] + prep[3] for prep, st in zip(preps, sts)]


def _gla_out(o, gn, rs):
    return o * lax.rsqrt(_group_mean(o * o, GLA_DV) + EPS) * gn * rs


def _gdn_qk_norm(qkv):
    hd = GDN_DK
    r = qkv.shape[0]
    cqk = jnp.concatenate([qkv[:, 0:256], qkv[:, 256:512]], axis=0)
    nrm = lax.rsqrt(_group_mean(cqk * cqk, hd) * hd + EPS)
    return qkv[:, 0:256] * nrm[0:r] * (hd ** -0.5), qkv[:, 256:512] * nrm[r:2 * r]


def _gdn_prep(q, k, cv, gs, betas, m, seg):
    nh, hd = GDN_HEADS, GDN_DK
    n = range(len(q))
    gc = [_mask_dot(m.tri, gs[i], SUM_PIECES) for i in n]
    if seg == ROWS:
        gtot = [jnp.broadcast_to(gc[i][ROWS - 1:ROWS], gc[i].shape) for i in n]
    else:
        gtot = [_mask_dot(m.same, gs[i], SUM_PIECES) for i in n]
    qkk = [_dot1(jnp.concatenate([q[i], k[i]], axis=0), _stack_heads(k[i], hd, nh), _dot_nt) for i in n]
    gct = [gc[i].T for i in n]
    grow = [jnp.concatenate([jnp.broadcast_to(gct[i][h * hd:h * hd + 1, :], (ROWS, ROWS)) for h in range(nh)], axis=1)
            for i in n]
    decay = [jnp.where(m.tri_l, jnp.exp(jnp.where(m.tri_l, gc[i] - grow[i], 0.0)), 0.0) for i in n]
    amat = [jnp.where(m.strict_l, betas[i] * decay[i] * qkk[i][ROWS:2 * ROWS], 0.0) for i in n]
    inv = [m.eye_l - jnp.where(m.levels[0], amat[i], 0.0) for i in n]
    for lvl in m.levels[1:]:
        low = [_block_diag(jnp.where(lvl, amat[i], 0.0).astype(BF16), nh) for i in n]
        prod = [_dot(inv[i].astype(BF16), low[i]) for i in n]
        inv = [inv[i] - _dot(prod[i].astype(BF16), _block_diag(inv[i].astype(BF16), nh)) for i in n]
    eg = [jnp.exp(gc[i]) for i in n]
    rhs = [jnp.concatenate([betas[i] * cv[i], betas[i] * eg[i] * k[i]], axis=1) for i in n]
    uw = [_dot(inv[i].astype(BF16), _stack_heads(rhs[i].astype(BF16), hd, nh, period=GDN_QK)) for i in n]
    return [(uw[i][:, 0:256], uw[i][:, 256:512], q[i] * eg[i], qkk[i][0:ROWS] * decay[i],
             k[i] * jnp.exp(gtot[i] - gc[i]), jnp.exp(gtot[i])) for i in n]


def _gdn_scan(preps, sts, m, nseq):
    seg = ROWS // nseq
    n = range(len(preps))
    ws = [_dot1(jnp.concatenate([_widen(preps[i][1], nseq), _widen(preps[i][2], nseq)], axis=0), sts[i]) for i in n]
    u = [preps[i][0] - ws[i][0:ROWS] for i in n]
    outs = [ws[i][ROWS:2 * ROWS] + _dot1(preps[i][3], _stack_heads(u[i], GDN_DV, GDN_HEADS)) for i in n]
    new = []
    for i in n:
        dn = preps[i][5]
        dn_tall = jnp.concatenate(
            [jnp.broadcast_to(dn[j * seg:j * seg + 1], (GDN_QK, GDN_QK)) for j in range(nseq)], axis=0)
        new.append(sts[i] * dn_tall + _dot1(_widen(preps[i][4], nseq), u[i], _dot_tn) * m.bd_gdn)
    return outs, new


def _gdn_out(o, gn, zs):
    return o * lax.rsqrt(_group_mean(o * o, GDN_DV) + EPS) * gn * zs


def _cm_block(p, ws, bias, seg):
    r = p.shape[0]
    gu, vn = p[:, 0:256], p[:, 256:512]
    ri, ci = _iota2((r, CM_GROUPS * r))
    ci = _imod(ci, r)
    wm = jnp.where((_idiv(ri, seg) == _idiv(ci, seg)) & (ci <= ri), ws, 0.0)
    return gu * (_dot1(wm, _stack_heads(vn, BRANCH_W // CM_GROUPS, CM_GROUPS)) + bias)


def _conv_taps(cw_ref):
    return [cw_ref[GDN_CONV - 1 - d:GDN_CONV - d, :] for d in range(GDN_CONV)]


def _conv_prompt(x, hist, taps):
    n = x.shape[0]
    t8 = lax.broadcasted_iota(jnp.int32, (HIST, 1), 0)
    acc = taps[0] * x
    for d in range(1, GDN_CONV):
        xr = pltpu.roll(x, d, 0)
        head = jnp.where(t8 < d, pltpu.roll(hist, d, 0), xr[0:HIST])
        acc = acc + taps[d] * jnp.concatenate([head, xr[HIST:n]], axis=0)
    return acc


def _conv_sample(x, hist, taps, seg):
    n = x.shape[0]
    tloc = _imod(lax.broadcasted_iota(jnp.int32, (n, 1), 0), seg)
    acc = taps[0] * x
    for d in range(1, GDN_CONV):
        prev = jnp.where(tloc < d, pltpu.roll(hist, (d - seg) % n, 0), pltpu.roll(x, d, 0))
        acc = acc + taps[d] * prev
    return acc


def _block_rows(c):
    return slice(c * ROWS, (c + 1) * ROWS)


def _mixer_prompt_kernel(*refs):
    ns, nc, sr = PROMPT_SEQS, PROMPT_CHUNKS, PROMPT_SEQ_ROWS
    p_refs = refs[:ns]
    gn_gla_ref, gn_gdn_ref, cmw_ref, cmb_ref, o_ref, sgla_ref, sgdn_ref, st_gla, st_gdn = refs[ns:]

    @pl.when(pl.program_id(1) == 0)
    def _():
        st_gla[...] = jnp.zeros_like(st_gla)
        st_gdn[...] = jnp.zeros_like(st_gdn)

    m = _Masks(ROWS)
    g0 = P_GDN
    for s in range(ns):
        o_ref[s, :, 768:1024] = p_refs[s][:, g0 + 768:g0 + 1024]
    qkv = jnp.concatenate([p_refs[s][:, g0:g0 + 768] for s in range(ns)], axis=0)
    qn, kn = _gdn_qk_norm(qkv)
    blocks = [(s, c) for c in range(nc) for s in range(ns)]
    rows_of = lambda s, c: slice(s * sr + c * ROWS, s * sr + (c + 1) * ROWS)
    gdn = _gdn_prep([qn[rows_of(s, c)] for s, c in blocks], [kn[rows_of(s, c)] for s, c in blocks],
                    [qkv[rows_of(s, c), 512:768] for s, c in blocks],
                    [p_refs[s][_block_rows(c), g0 + 1280:g0 + 1536] for s, c in blocks],
                    [p_refs[s][_block_rows(c), g0 + 1536:g0 + 1792] for s, c in blocks], m, ROWS)
    gla = [_gla_prep(p_refs[s][_block_rows(c), P_GLA:P_GLA + 896], m, 1) for s, c in blocks]
    sg = [st_gla[s] for s in range(ns)]
    sd = [st_gdn[s] for s in range(ns)]
    og, od = {}, {}
    for c in range(nc):
        o, sg = _gla_scan(gla[c * ns:(c + 1) * ns], sg)
        og.update({(s, c): o[s] for s in range(ns)})
        o, sd = _gdn_scan(gdn[c * ns:(c + 1) * ns], sd, m, 1)
        od.update({(s, c): o[s] for s in range(ns)})
    by_rows = lambda d: jnp.concatenate([d[(s, c)] for s in range(ns) for c in range(nc)], axis=0)
    rs = jnp.concatenate([p_refs[s][:, 512:768] for s in range(ns)], axis=0)
    zs = jnp.concatenate([p_refs[s][:, g0 + 1024:g0 + 1280] for s in range(ns)], axis=0)
    o_gla = _gla_out(by_rows(og), gn_gla_ref[...], rs)
    o_gdn = _gdn_out(by_rows(od), gn_gdn_ref[...], zs)
    for s in range(ns):
        st_gla[s] = sg[s]
        st_gdn[s] = sd[s]
        o_ref[s, :, 0:256] = o_gla[s * sr:(s + 1) * sr]
        o_ref[s, :, 256:512] = o_gdn[s * sr:(s + 1) * sr]
        for c in range(sr // CM_CHUNK):
            rows = slice(c * CM_CHUNK, (c + 1) * CM_CHUNK)
            o_ref[s, rows, 512:768] = _cm_block(p_refs[s][rows, P_CM:P_CM + 512], cmw_ref[...], cmb_ref[...], CM_CHUNK)

    @pl.when(pl.program_id(1) == pl.num_programs(1) - 1)
    def _():
        for s in range(ns):
            sgla_ref[s] = _compact_state(sg[s], GLA_DV, GLA_HEADS)
            sgdn_ref[s] = _compact_state(sd[s], GDN_DV, GDN_HEADS)


def _mixer_prompt(p, gn_gla, gn_gdn, cmw, cmb, *, layer, nseqs, seq_len):
    ns, sr = PROMPT_SEQS, PROMPT_SEQ_ROWS
    nsteps = seq_len // sr
    seq_spec = lambda s: pl.BlockSpec((sr, P_COLS), lambda o, c: ((o * ns + s) * nsteps + c, 0))
    per_group = lambda *shape: pl.BlockSpec((ns,) + shape, lambda o, c: (o,) + (0,) * len(shape))
    per_layer = lambda *shape: _layer_spec(shape, layer)
    return pl.pallas_call(
        _mixer_prompt_kernel,
        grid=(nseqs // ns, nsteps),
        in_specs=[seq_spec(s) for s in range(ns)] + [
            per_layer(1, 256), per_layer(1, 256),
            per_layer(CM_CHUNK, CM_GROUPS * CM_CHUNK), per_layer(CM_CHUNK, 256)],
        out_specs=[pl.BlockSpec((ns, sr, 4 * BRANCH_W), lambda o, c: (o, c, 0)),
                   per_group(GLA_QK, GLA_DV), per_group(GDN_QK, GDN_DV)],
        out_shape=[jax.ShapeDtypeStruct((nseqs, seq_len, 4 * BRANCH_W), F32),
                   jax.ShapeDtypeStruct((nseqs, GLA_QK, GLA_DV), F32),
                   jax.ShapeDtypeStruct((nseqs, GDN_QK, GDN_DV), F32)],
        scratch_shapes=[pltpu.VMEM((ns, GLA_QK, GLA_HEADS * GLA_DV), F32), pltpu.VMEM((ns, GDN_QK, GDN_QK), F32)],
        compiler_params=pltpu.CompilerParams(dimension_semantics=("arbitrary", "arbitrary"),
                                             vmem_limit_bytes=VMEM_LIMIT),
        name="mixer_prompt",
    )(*([p] * ns), gn_gla, gn_gdn, cmw, cmb)


def _mixer_sample_kernel(p_ref, s0gla_ref, s0gdn_ref, gn_gla_ref, gn_gdn_ref, cmw_ref, cmb_ref,
                         o_ref, sgla_ref, sgdn_ref, *, seg):
    nseq = ROWS // seg
    nb = range(SAMPLE_BLOCKS_PER_STEP)
    m = _Masks(seg)
    g0 = P_GDN
    o_ref[:, 768:1024] = p_ref[:, g0 + 768:g0 + 1024]
    qkv = p_ref[:, g0:g0 + 768]
    qn, kn = _gdn_qk_norm(qkv)
    gdn = _gdn_prep([qn[_block_rows(c)] for c in nb], [kn[_block_rows(c)] for c in nb],
                    [qkv[_block_rows(c), 512:768] for c in nb],
                    [p_ref[_block_rows(c), g0 + 1280:g0 + 1536] for c in nb],
                    [p_ref[_block_rows(c), g0 + 1536:g0 + 1792] for c in nb], m, seg)
    gla = [_gla_prep(p_ref[_block_rows(c), P_GLA:P_GLA + 896], m, nseq) for c in nb]
    seqs = lambda c: slice(c * nseq, (c + 1) * nseq)
    sg = [_expand_state(s0gla_ref[seqs(c)].reshape(nseq * GLA_QK, GLA_DV), GLA_DV, GLA_HEADS) * m.bd_gla for c in nb]
    sd = [_expand_state(s0gdn_ref[seqs(c)].reshape(nseq * GDN_QK, GDN_DV), GDN_DV, GDN_HEADS) * m.bd_gdn for c in nb]
    og, sg = _gla_scan(gla, sg)
    od, sd = _gdn_scan(gdn, sd, m, nseq)
    for c in nb:
        rows = _block_rows(c)
        sgla_ref[seqs(c)] = _compact_state(sg[c], GLA_DV, GLA_HEADS).reshape(nseq, GLA_QK, GLA_DV)
        sgdn_ref[seqs(c)] = _compact_state(sd[c], GDN_DV, GDN_HEADS).reshape(nseq, GDN_QK, GDN_DV)
        o_ref[rows, 512:768] = _cm_block(p_ref[rows, P_CM:P_CM + 512], cmw_ref[...], cmb_ref[...], seg)
    o_ref[:, 0:256] = _gla_out(jnp.concatenate(og, axis=0), gn_gla_ref[...], p_ref[:, 512:768])
    o_ref[:, 256:512] = _gdn_out(jnp.concatenate(od, axis=0), gn_gdn_ref[...], p_ref[:, g0 + 1024:g0 + 1280])


def _mixer_sample(p, s0gla, s0gdn, gn_gla, gn_gdn, cmw, cmb, *, layer, base_step, nsteps, seg):
    rows = SAMPLE_STEP_ROWS
    nseq_step = rows // seg
    per_layer = lambda *shape: _layer_spec(shape, layer)
    return pl.pallas_call(
        functools.partial(_mixer_sample_kernel, seg=seg),
        grid=(nsteps,),
        in_specs=[pl.BlockSpec((rows, P_COLS), lambda i: (base_step + i, 0)),
                  pl.BlockSpec((None, nseq_step, GLA_QK, GLA_DV), lambda i: (layer, i, 0, 0)),
                  pl.BlockSpec((None, nseq_step, GDN_QK, GDN_DV), lambda i: (layer, i, 0, 0)),
                  per_layer(1, 256), per_layer(1, 256),
                  per_layer(ROWS, CM_GROUPS * ROWS), per_layer(ROWS, 256)],
        out_specs=[pl.BlockSpec((rows, 4 * BRANCH_W), lambda i: (i, 0)),
                   pl.BlockSpec((nseq_step, GLA_QK, GLA_DV), lambda i: (i, 0, 0)),
                   pl.BlockSpec((nseq_step, GDN_QK, GDN_DV), lambda i: (i, 0, 0))],
        out_shape=[jax.ShapeDtypeStruct((nsteps * rows, 4 * BRANCH_W), F32),
                   jax.ShapeDtypeStruct((nsteps * nseq_step, GLA_QK, GLA_DV), F32),
                   jax.ShapeDtypeStruct((nsteps * nseq_step, GDN_QK, GDN_DV), F32)],
        compiler_params=pltpu.CompilerParams(dimension_semantics=("parallel",), vmem_limit_bytes=VMEM_LIMIT),
        name="mixer_sample",
    )(p, s0gla, s0gdn, gn_gla, gn_gdn, cmw, cmb)


def _merge_kernel(*refs, prompt_tiles, pair):
    nh = 2 if pair else 1
    bp_ref, bs_ref, nw_ref, wg_ref, wb_ref, wo_ref, o_ref = refs[nh:]
    h = _residual_rows(refs[:nh], prompt_tiles)
    xn = _rms(h, nw_ref[...]).astype(BF16)
    br = _stream_rows(bp_ref, bs_ref, prompt_tiles).astype(BF16)
    merged = None
    for gi in range(N_BRANCH):
        gate = _sigmoid(_dot(xn, wg_ref[:, gi * D_MODEL:(gi + 1) * D_MODEL]))
        term = _dot(br[:, gi * BRANCH_W:(gi + 1) * BRANCH_W], wb_ref[gi]) * gate
        merged = term if merged is None else merged + term
    o_ref[...] = h + _dot(merged.astype(BF16), wo_ref[...])


def _stream_specs(tm, width, prompt_tiles, lead=()):
    nlead = (None,) * len(lead)
    return (pl.BlockSpec(nlead + (tm, width), lambda i: lead + (jnp.minimum(i, prompt_tiles - 1), 0)),
            pl.BlockSpec(nlead + (tm, width), lambda i: lead + (jnp.maximum(i - prompt_tiles, 0), 0)))


def _merge(h, br_p, br_s, nw, wg, wb, wo, *, layer):
    tm = TOKEN_TILE
    h_arrays, h_specs, _, ntok = _residual_specs(h, tm)
    prompt_tiles = br_p.shape[0] // tm
    row = lambda n: pl.BlockSpec((tm, n), lambda i: (i, 0))
    return pl.pallas_call(
        functools.partial(_merge_kernel, prompt_tiles=prompt_tiles, pair=isinstance(h, tuple)),
        grid=(ntok // tm,),
        in_specs=[*h_specs, *_stream_specs(tm, 4 * BRANCH_W, prompt_tiles), _layer_spec((1, D_MODEL), layer),
                  _layer_spec((D_MODEL, N_BRANCH * D_MODEL), layer),
                  _layer_spec((N_BRANCH, BRANCH_W, D_MODEL), layer), _layer_spec((D_MODEL, D_MODEL), layer)],
        out_specs=row(D_MODEL),
        out_shape=jax.ShapeDtypeStruct((ntok, D_MODEL), F32),
        compiler_params=pltpu.CompilerParams(dimension_semantics=("parallel",), vmem_limit_bytes=VMEM_LIMIT),
        name="merge",
    )(*h_arrays, br_p, br_s, nw, wg, wb, wo)


def _ffn_kernel(h_ref, pp_ref, ps_ref, nf_ref, wfg_ref, wfu_ref, wfd_ref, np_ref, wpg_ref, wp_ref, nfin_ref,
                *o_refs, final, prompt_tiles):
    h = h_ref[...]
    xf = _rms(h, nf_ref[...]).astype(BF16)
    act = _silu(_dot(xf, wfg_ref[...])) * _dot(xf, wfu_ref[...])
    h = h + _dot(act.astype(BF16), wfd_ref[...])
    pg = _sigmoid(_dot(_rms(h, np_ref[...]).astype(BF16), wpg_ref[...]))
    pe = _stream_rows(pp_ref, ps_ref, prompt_tiles).astype(BF16)
    h = h + pg * _dot(pe, wp_ref[...])
    if not final:
        o_refs[0][...] = h
        return
    out = _rms(h, nfin_ref[...])
    op_ref, os_ref = o_refs

    @pl.when(pl.program_id(0) < prompt_tiles)
    def _():
        op_ref[...] = out

    @pl.when(pl.program_id(0) >= prompt_tiles)
    def _():
        os_ref[...] = out


def _ffn(h, pe_p, pe_s, nf, wfg, wfu, wfd, npl, wpg, wp, nfin, *, layer, final):
    ntok = h.shape[0]
    tm = TOKEN_TILE
    npt = pe_p.shape[1]
    prompt_tiles = npt // tm
    row = lambda n: pl.BlockSpec((tm, n), lambda i: (i, 0))
    once = lambda shape: _layer_spec(shape, layer, buffers=1)
    if final:
        out_specs = list(_stream_specs(tm, D_MODEL, prompt_tiles))
        out_shape = [jax.ShapeDtypeStruct((npt, D_MODEL), F32), jax.ShapeDtypeStruct((ntok - npt, D_MODEL), F32)]
    else:
        out_specs, out_shape = row(D_MODEL), jax.ShapeDtypeStruct((ntok, D_MODEL), F32)
    return pl.pallas_call(
        functools.partial(_ffn_kernel, final=final, prompt_tiles=prompt_tiles),
        grid=(ntok // tm,),
        in_specs=[row(D_MODEL), *_stream_specs(tm, PLE_DIM, prompt_tiles, lead=(layer,)),
                  _layer_spec((1, D_MODEL), layer),
                  once((D_MODEL, D_FF)), once((D_MODEL, D_FF)), once((D_FF, D_MODEL)),
                  _layer_spec((1, D_MODEL), layer), once((D_MODEL, D_MODEL)), once((PLE_DIM, D_MODEL)),
                  _const_spec((1, D_MODEL))],
        out_specs=out_specs,
        out_shape=out_shape,
        compiler_params=pltpu.CompilerParams(dimension_semantics=("arbitrary",), vmem_limit_bytes=VMEM_LIMIT),
        name="ffn_final" if final else "ffn",
    )(h, pe_p, pe_s, nf, wfg, wfu, wfd, npl, wpg, wp, nfin)


def _split_w_in(w_in):
    span = lambda s: w_in[..., s[0]:s[1]].astype(BF16)
    rep = lambda o: jnp.repeat(w_in[..., o:o + GDN_HEADS], GDN_DK, axis=-1)
    small = jnp.concatenate([w_in[..., W_IN_GA:W_IN_GA + GLA_RANK],
                             jnp.zeros(w_in.shape[:-1] + (LANE - GLA_RANK,), w_in.dtype),
                             rep(W_IN_DA), rep(W_IN_DB)], axis=-1).astype(BF16)
    return span(W_IN_GLA), span(W_IN_GDN), span(W_IN_REST), small


def kernel(x_prompt, x_sample, state_gla, state_gdn, state_gdn_conv, state_sconv, p_prompt, p_sample, norm_mix, w_in, gla_wa2, gla_ba, gla_norm, gdn_conv_w, gdn_a_log, gdn_dt_bias, gdn_norm, cm_ln_g, cm_ln_b, cm_ws, cm_bs, sc_conv_w, w_gate, w_branch, w_o, norm_ffn, w_ffn_gate, w_ffn_up, w_ffn_down, norm_ple, w_ple_gate, w_ple, norm_final):
    depth = w_in.shape[0]
    bp, tp, _ = x_prompt.shape
    bs, ts, _ = x_sample.shape
    npt, nst = bp * tp, bs * ts
    sseq = ROWS // ts
    assert ts == HIST and nst % SAMPLE_STEP_ROWS == 0
    assert bp % PROMPT_SEQS == 0 and tp % PROMPT_SEQ_ROWS == 0 and PROMPT_SEQ_ROWS % CM_CHUNK == 0
    assert tp % TOKEN_TILE == 0 and nst % TOKEN_TILE == 0 and TOKEN_TILE % ts == 0

    h = (x_prompt.reshape(npt, D_MODEL), x_sample.reshape(nst, D_MODEL))
    pe_p = p_prompt.reshape(depth, npt, PLE_DIM)
    pe_s = p_sample.reshape(depth, nst, PLE_DIM)
    s0_gla = state_gla.reshape(depth, bs, GLA_QK, GLA_DV)
    s0_gdn = state_gdn.reshape(depth, bs, GDN_QK, GDN_DV)
    rows = lambda a: a.reshape(depth, 1, -1)
    w_in_groups = _split_w_in(w_in)
    wg, wb, wo = w_gate.astype(BF16), w_branch.astype(BF16), w_o.astype(BF16)
    wfg, wfu, wfd = w_ffn_gate.astype(BF16), w_ffn_up.astype(BF16), w_ffn_down.astype(BF16)
    wpg, wp = w_ple_gate.astype(BF16), w_ple.astype(BF16)
    wa2 = jnp.pad(gla_wa2, ((0, 0), (0, LANE - GLA_RANK), (0, 0))).astype(BF16)
    nmix, nffn, nple = rows(norm_mix), rows(norm_ffn), rows(norm_ple)
    inproj_vecs = (rows(gla_ba), rows(jnp.repeat(gdn_a_log, GDN_DK, axis=1)),
                   rows(jnp.repeat(gdn_dt_bias, GDN_DK, axis=1)), rows(cm_ln_g), rows(cm_ln_b))
    cw = jnp.concatenate([gdn_conv_w, jnp.pad(sc_conv_w, ((0, 0), (GDN_CONV - SC_WIDTH, 0), (0, 0)))], axis=2)
    hist_s = jnp.concatenate([
        jnp.pad(state_gdn_conv, ((0, 0), (0, 0), (HIST - (GDN_CONV - 1), 0), (0, 0))),
        jnp.pad(state_sconv, ((0, 0), (0, 0), (HIST - (SC_WIDTH - 1), 0), (0, 0)))], axis=3).reshape(depth, nst, CONV_W)
    gn_gla = rows(jnp.tile(gla_norm, (1, GLA_HEADS)))
    gn_gdn = rows(jnp.tile(gdn_norm, (1, GDN_HEADS)))
    cmw_p = jnp.transpose(cm_ws, (0, 2, 1, 3)).reshape(depth, CM_CHUNK, CM_GROUPS * CM_CHUNK)
    cmb_p = jnp.repeat(jnp.swapaxes(cm_bs, 1, 2), BRANCH_W // CM_GROUPS, axis=2)
    cmw_s = jnp.transpose(jnp.tile(cm_ws[:, :, :ts, :ts], (1, 1, sseq, sseq)),
                          (0, 2, 1, 3)).reshape(depth, ROWS, CM_GROUPS * ROWS)
    cmb_s = jnp.tile(cmb_p[:, :ts], (1, sseq, 1))

    outs = {k: [] for k in ("gla_p", "gla_s", "gdn_p", "gdn_s", "gc_p", "gc_s", "sc_p", "sc_s", "cv_s")}
    for i in range(depth):
        h_p, h_s, first_s = (h[0], h[1], 0) if isinstance(h, tuple) else (h, h, npt // TOKEN_TILE)
        inproj_args = (hist_s, nmix, *w_in_groups, wa2, *inproj_vecs, cw)
        p_p, tails = _inproj(h_p, 0, npt, *inproj_args, layer=i, sample=False, seq_len=tp, seg=ts)
        p_s, xs_raw = _inproj(h_s, first_s, nst, *inproj_args, layer=i, sample=True, seq_len=tp, seg=ts)
        br_p, gla_p, gdn_p = _mixer_prompt(p_p, gn_gla, gn_gdn, cmw_p, cmb_p, layer=i, nseqs=bp, seq_len=tp)
        br_p = br_p.reshape(npt, N_BRANCH * BRANCH_W)
        br_s, gla_s, gdn_s = _mixer_sample(p_s, s0_gla, s0_gdn, gn_gla, gn_gdn, cmw_s, cmb_s, layer=i,
                                           base_step=0, nsteps=nst // SAMPLE_STEP_ROWS, seg=ts)
        outs["gla_p"].append(gla_p.reshape(bp, GLA_HEADS, GLA_DK, GLA_DV))
        outs["gla_s"].append(gla_s.reshape(bs, GLA_HEADS, GLA_DK, GLA_DV))
        outs["gdn_p"].append(gdn_p.reshape(bp, GDN_HEADS, GDN_DK, GDN_DV))
        outs["gdn_s"].append(gdn_s.reshape(bs, GDN_HEADS, GDN_DK, GDN_DV))
        tiles_per_seq = tp // TOKEN_TILE
        tail_p = tails[tiles_per_seq - 1:bp * tiles_per_seq:tiles_per_seq]
        xs3 = xs_raw.reshape(bs, ts, CONV_W)
        outs["gc_p"].append(tail_p[:, HIST - (GDN_CONV - 1):, 0:768])
        outs["gc_s"].append(xs3[:, ts - (GDN_CONV - 1):, 0:768])
        outs["sc_p"].append(tail_p[:, HIST - (SC_WIDTH - 1):, 768:])
        outs["sc_s"].append(xs3[:, ts - (SC_WIDTH - 1):, 768:])
        outs["cv_s"].append(p_s[:, P_CM + 256:P_CM + 512].reshape(bs, ts, BRANCH_W))

        h1 = _merge(h, br_p, br_s, nmix, wg, wb, wo, layer=i)
        h = _ffn(h1, pe_p, pe_s, nffn, wfg, wfu, wfd, nple, wpg, wp, norm_final.reshape(1, D_MODEL),
                 layer=i, final=(i == depth - 1))

    y_prompt = h[0].reshape(bp, tp, D_MODEL)
    y_sample = h[1].reshape(bs, ts, D_MODEL)
    st = lambda k: jnp.stack(outs[k])
    return (y_prompt, y_sample, st("gla_p"), st("gla_s"), st("gdn_p"), st("gdn_s"),
            st("gc_p"), st("gc_s"), st("sc_p"), st("sc_s"), st("cv_s"))
```

```python
import functools

import jax
import jax.numpy as jnp
from jax import lax
from jax.experimental import pallas as pl
from jax.experimental.pallas import tpu as pltpu

F32 = jnp.float32
BF16 = jnp.bfloat16

D_MODEL = 1024
PLE_DIM = 256
BRANCH_W = 256
N_BRANCH = 4
GLA_HEADS = 4
GLA_DK = 32
GLA_DV = 64
GLA_RANK = 16
GLA_TAU = 16.0
GDN_HEADS = 4
GDN_DK = 64
GDN_DV = 64
GDN_CONV = 4
CM_GROUPS = 4
CM_CHUNK = 128
SC_WIDTH = 3
D_FF = 2816
EPS = 1e-6

ROWS = 64
PROMPT_SEQS = 4
PROMPT_CHUNKS = 4
PROMPT_SEQ_ROWS = ROWS * PROMPT_CHUNKS
SAMPLE_BLOCKS_PER_STEP = 2
SAMPLE_STEP_ROWS = ROWS * SAMPLE_BLOCKS_PER_STEP
GLA_QK = GLA_HEADS * GLA_DK
GDN_QK = GDN_HEADS * GDN_DK
LANE = 128
CONV_W = 3 * BRANCH_W + BRANCH_W
HIST = 8
SUM_PIECES = 2

W_IN_GLA, W_IN_GDN, W_IN_REST = (0, 768), (784, 1808), (1816, 3096)
W_IN_GA, W_IN_DA, W_IN_DB = 768, 1808, 1812
SMALL_COLS = LANE + 2 * 256
P_GLA = 0
P_GDN = 896
P_CM = 2688
P_COLS = 3200

V7X_VMEM_BYTES = 64 * 1024 * 1024
VMEM_LIMIT = V7X_VMEM_BYTES * 7 // 8
TOKEN_TILE = 512


def _dot(a, b):
    return jnp.dot(a, b, preferred_element_type=F32)


def _dot_nt(a, b):
    return lax.dot_general(a, b, (((1,), (1,)), ((), ())), preferred_element_type=F32)


def _dot_tn(a, b):
    return lax.dot_general(a, b, (((0,), (0,)), ((), ())), preferred_element_type=F32)


def _split(x, n):
    parts, r = [], x
    for i in range(n):
        p = r.astype(BF16)
        parts.append(p)
        if i + 1 < n:
            r = r - p.astype(F32)
    return parts


def _dot1(a, b, dot=_dot):
    return dot(a.astype(BF16), b.astype(BF16))


def _mask_dot(mask, x, n):
    return _dot(jnp.concatenate([mask.astype(BF16)] * n, axis=1), jnp.concatenate(_split(x, n), axis=0))


def _sigmoid(x):
    return 1.0 / (1.0 + jnp.exp(-x))


def _silu(x):
    return x * _sigmoid(x)


def _softplus(x):
    return jnp.maximum(x, 0.0) + jnp.log(1.0 + jnp.exp(-jnp.abs(x)))


def _gelu_tanh(x):
    return 0.5 * x * (1.0 + jnp.tanh(0.7978845608028654 * (x + 0.044715 * (x * x * x))))


def _rms(x, w):
    return x * lax.rsqrt(jnp.mean(x * x, axis=-1, keepdims=True) + EPS) * w


def _idiv(x, n):
    assert n & (n - 1) == 0
    return lax.shift_right_logical(x, n.bit_length() - 1)


def _imod(x, n):
    assert n & (n - 1) == 0
    return lax.bitwise_and(x, n - 1)


def _const_spec(shape):
    return pl.BlockSpec(shape, lambda *_: (0,) * len(shape))


def _layer_spec(shape, layer, buffers=None):
    mode = {} if buffers is None else {"pipeline_mode": pl.Buffered(buffers)}
    return pl.BlockSpec((None,) + shape, lambda *_: (layer,) + (0,) * len(shape), **mode)


def _iota2(shape):
    return lax.broadcasted_iota(jnp.int32, shape, 0), lax.broadcasted_iota(jnp.int32, shape, 1)


def _stream_rows(hp_ref, hs_ref, prompt_tiles):
    return jnp.where(pl.program_id(0) < prompt_tiles, hp_ref[...], hs_ref[...])


def _residual_rows(h_refs, prompt_tiles):
    return h_refs[0][...] if len(h_refs) == 1 else _stream_rows(*h_refs, prompt_tiles)


def _residual_specs(h, tm):
    if isinstance(h, tuple):
        prompt_tiles = h[0].shape[0] // tm
        return list(h), list(_stream_specs(tm, D_MODEL, prompt_tiles)), prompt_tiles, h[0].shape[0] + h[1].shape[0]
    return [h], [pl.BlockSpec((tm, D_MODEL), lambda i: (i, 0))], None, h.shape[0]


def _inproj_kernel(*refs, sample, tiles_per_seq, seg):
    if sample:
        (h_ref, hist_ref, nw_ref, wgla_ref, wgdn_ref, wrest_ref, wsmall_ref, wa2_ref, ba_ref, alog_ref,
         dtb_ref, lng_ref, lnb_ref, cw_ref, p_ref, xraw_ref) = refs
    else:
        (h_ref, nw_ref, wgla_ref, wgdn_ref, wrest_ref, wsmall_ref, wa2_ref, ba_ref, alog_ref,
         dtb_ref, lng_ref, lnb_ref, cw_ref, p_ref, xraw_ref, carry) = refs
    tm = p_ref.shape[0]
    xn = _rms(h_ref[...], nw_ref[...]).astype(BF16)
    pd = _dot(xn, wgdn_ref[...])
    pr = _dot(xn, wrest_ref[...])
    pg = _dot(xn, wgla_ref[...])
    ps = _dot(xn, wsmall_ref[...])
    g0 = P_GDN
    x = jnp.concatenate([pd[:, 0:768], pr[:, 1024:1280] * pr[:, 512:768]], axis=1)
    taps = _conv_taps(cw_ref)
    if sample:
        acc = _conv_sample(x, hist_ref[...], taps, seg)
        xraw_ref[...] = x
    else:
        hist = jnp.where(lax.rem(pl.program_id(0), tiles_per_seq) == 0, 0.0, carry[...])
        acc = _conv_prompt(x, hist, taps)
        carry[...] = x[tm - HIST:tm]
        xraw_ref[...] = x[tm - HIST:tm]
    p_ref[:, g0:g0 + 768] = _silu(acc[:, 0:768])
    p_ref[:, g0 + 768:g0 + 1024] = pr[:, 768:1024] * acc[:, 768:1024]
    p_ref[:, g0 + 1024:g0 + 1280] = _silu(pd[:, 768:1024])
    p_ref[:, 0:128] = pg[:, 0:128] * (GLA_DK ** -0.5)
    p_ref[:, 128:512] = pg[:, 128:512]
    p_ref[:, 512:768] = _silu(pg[:, 512:768])
    za = _dot(ps[:, 0:LANE].astype(BF16), wa2_ref[...]) + ba_ref[...]
    p_ref[:, 768:896] = -_softplus(-za) * (1.0 / GLA_TAU)
    p_ref[:, g0 + 1280:g0 + 1536] = -jnp.exp(alog_ref[...]) * _softplus(ps[:, LANE:LANE + 256] + dtb_ref[...])
    p_ref[:, g0 + 1536:g0 + 1792] = _sigmoid(ps[:, LANE + 256:LANE + 512])
    p_ref[:, P_CM:P_CM + 256] = _gelu_tanh(pr[:, 0:256])
    gv = _gelu_tanh(pr[:, 256:512])
    mu = jnp.mean(gv, axis=-1, keepdims=True)
    d = gv - mu
    var = jnp.mean(d * d, axis=-1, keepdims=True)
    p_ref[:, P_CM + 256:P_CM + 512] = d * lax.rsqrt(var + EPS) * lng_ref[...] + lnb_ref[...]


def _inproj(h, first_tile, rows, hist_s, nw, wgla, wgdn, wrest, wsmall, wa2, ba, alog, dtb, lng, lnb, cw, *,
            layer, sample, seq_len, seg):
    tm = TOKEN_TILE
    ntiles = rows // tm
    cols = lambda span: span[1] - span[0]
    per_layer = lambda *shape: _layer_spec(shape, layer)
    tile = lambda n: pl.BlockSpec((tm, n), lambda i: (i, 0))
    if sample:
        hist = [hist_s]
        hist_spec = [pl.BlockSpec((None, tm, CONV_W), lambda i: (layer, i, 0))]
        raw_spec, raw_shape, scratch = tile(CONV_W), (rows, CONV_W), []
    else:
        hist, hist_spec = [], []
        raw_spec, raw_shape = pl.BlockSpec((None, HIST, CONV_W), lambda i: (i, 0, 0)), (ntiles, HIST, CONV_W)
        scratch = [pltpu.VMEM((HIST, CONV_W), F32)]
    return pl.pallas_call(
        functools.partial(_inproj_kernel, sample=sample, tiles_per_seq=seq_len // tm, seg=seg),
        grid=(ntiles,),
        in_specs=[pl.BlockSpec((tm, D_MODEL), lambda i: (first_tile + i, 0)), *hist_spec, per_layer(1, D_MODEL),
                  per_layer(D_MODEL, cols(W_IN_GLA)), per_layer(D_MODEL, cols(W_IN_GDN)),
                  per_layer(D_MODEL, cols(W_IN_REST)), per_layer(D_MODEL, SMALL_COLS),
                  per_layer(LANE, GLA_QK), per_layer(1, GLA_QK), per_layer(1, 256),
                  per_layer(1, 256), per_layer(1, 256), per_layer(1, 256), per_layer(GDN_CONV, CONV_W)],
        out_specs=[tile(P_COLS), raw_spec],
        out_shape=[jax.ShapeDtypeStruct((rows, P_COLS), F32), jax.ShapeDtypeStruct(raw_shape, F32)],
        scratch_shapes=scratch,
        compiler_params=pltpu.CompilerParams(dimension_semantics=("arbitrary",), vmem_limit_bytes=VMEM_LIMIT),
        name="inproj_sample" if sample else "inproj_prompt",
    )(h, *hist, nw, wgla, wgdn, wrest, wsmall, wa2, ba, alog, dtb, lng, lnb, cw)


def _stack_heads(x, group, nheads, period=None):
    w = x.shape[1]
    li = lax.broadcasted_iota(jnp.int32, (1, w), 1)
    if period is not None:
        li = _imod(li, period)
    hid = _idiv(li, group)
    zero = jnp.zeros_like(x)
    return jnp.concatenate([jnp.where(hid == h, x, zero) for h in range(nheads)], axis=0)


def _block_diag(x, nblocks):
    r = x.shape[0]
    ri, ci = _iota2((nblocks * r, nblocks * r))
    return jnp.where(_idiv(ri, r) == _idiv(ci, r), jnp.concatenate([x] * nblocks, axis=0), jnp.zeros((), x.dtype))


def _widen(x, nseq):
    if nseq == 1:
        return x
    seg = ROWS // nseq
    sid = _idiv(lax.broadcasted_iota(jnp.int32, (ROWS, 1), 0), seg)
    zero = jnp.zeros_like(x)
    return jnp.concatenate([jnp.where(sid == j, x, zero) for j in range(nseq)], axis=1)


def _group_mean(x, group):
    w = x.shape[1]
    ri, ci = _iota2((w, w))
    avg = jnp.where(_idiv(ri, group) == _idiv(ci, group), 1.0 / group, 0.0).astype(BF16)
    return _dot(x.astype(BF16), avg)


def _head_diag_mask(rows, cols, rhead, chead, rper):
    ri, ci = _iota2((rows, cols))
    return (_idiv(_imod(ri, rper), rhead) == _idiv(ci, chead)).astype(F32)


def _expand_state(s, width, reps):
    return jnp.concatenate([s] * reps, axis=1)


def _compact_state(st, width, reps):
    out = st[:, 0:width]
    for h in range(1, reps):
        out = out + st[:, h * width:(h + 1) * width]
    return out


class _Masks:
    def __init__(self, seg):
        ri, ci = _iota2((ROWS, ROWS))
        self.same = _idiv(ri, seg) == _idiv(ci, seg)
        self.tri = self.same & (ci <= ri)
        rl, cl = _iota2((ROWS, GDN_HEADS * ROWS))
        cl = _imod(cl, ROWS)
        same_l = _idiv(rl, seg) == _idiv(cl, seg)
        self.tri_l = same_l & (cl <= rl)
        self.strict_l = same_l & (cl < rl)
        self.eye_l = (rl == cl).astype(F32)
        self.levels = [(_idiv(rl, 2 * s) == _idiv(cl, 2 * s)) & (_idiv(rl, s) != _idiv(cl, s))
                       for s in (1 << k for k in range(seg.bit_length() - 1))]
        self.seg = seg
        nseq = ROWS // seg
        self.bd_gla = _head_diag_mask(nseq * GLA_QK, GLA_HEADS * GLA_DV, GLA_DK, GLA_DV, GLA_QK)
        self.bd_gdn = _head_diag_mask(nseq * GDN_QK, GDN_QK, GDN_DK, GDN_DV, GDN_QK)


def _gla_prep(p, m, nseq):
    q, k, v, la = p[:, 0:128], p[:, 128:256], p[:, 256:512], p[:, 768:896]
    b = _mask_dot(m.tri, la, SUM_PIECES)
    if m.seg == ROWS:
        btot = jnp.broadcast_to(b[ROWS - 1:ROWS], b.shape)
    else:
        btot = _mask_dot(m.same, la, SUM_PIECES)
    qd = q * jnp.exp(b)
    kd = k * jnp.exp(-b)
    ke = k * jnp.exp(btot - b)
    a = jnp.where(m.tri_l, _dot1(qd, _stack_heads(kd, GLA_DK, GLA_HEADS), _dot_nt), 0.0)
    o_intra = _dot1(a, _stack_heads(v, GLA_DV, GLA_HEADS))
    dec = jnp.exp(_dot_tn(jnp.concatenate(_split(_widen(la, nseq), SUM_PIECES), axis=0),
                          jnp.ones((SUM_PIECES * ROWS, GLA_HEADS * GLA_DV), BF16)))
    ds = _dot1(_widen(ke, nseq), v, _dot_tn) * m.bd_gla
    return o_intra, _widen(qd, nseq), dec, ds


def _gla_scan(preps, sts):
    outs = [prep[0] + _dot1(prep[1], st) for prep, st in zip(preps, sts)]
    return outs, [st * prep[2] + prep[3] for prep, st in zip(preps, sts)]


def _gla_out(o, gn, rs):
    return o * lax.rsqrt(_group_mean(o * o, GLA_DV) + EPS) * gn * rs


def _gdn_qk_norm(qkv):
    hd = GDN_DK
    r = qkv.shape[0]
    cqk = jnp.concatenate([qkv[:, 0:256], qkv[:, 256:512]], axis=0)
    nrm = lax.rsqrt(_group_mean(cqk * cqk, hd) * hd + EPS)
    return qkv[:, 0:256] * nrm[0:r] * (hd ** -0.5), qkv[:, 256:512] * nrm[r:2 * r]


def _gdn_prep(q, k, cv, gs, betas, m, seg):
    nh, hd = GDN_HEADS, GDN_DK
    n = range(len(q))
    gc = [_mask_dot(m.tri, gs[i], SUM_PIECES) for i in n]
    if seg == ROWS:
        gtot = [jnp.broadcast_to(gc[i][ROWS - 1:ROWS], gc[i].shape) for i in n]
    else:
        gtot = [_mask_dot(m.same, gs[i], SUM_PIECES) for i in n]
    qkk = [_dot1(jnp.concatenate([q[i], k[i]], axis=0), _stack_heads(k[i], hd, nh), _dot_nt) for i in n]
    gct = [gc[i].T for i in n]
    grow = [jnp.concatenate([jnp.broadcast_to(gct[i][h * hd:h * hd + 1, :], (ROWS, ROWS)) for h in range(nh)], axis=1)
            for i in n]
    decay = [jnp.where(m.tri_l, jnp.exp(jnp.where(m.tri_l, gc[i] - grow[i], 0.0)), 0.0) for i in n]
    amat = [jnp.where(m.strict_l, betas[i] * decay[i] * qkk[i][ROWS:2 * ROWS], 0.0) for i in n]
    inv = [m.eye_l - jnp.where(m.levels[0], amat[i], 0.0) for i in n]
    for lvl in m.levels[1:]:
        low = [_block_diag(jnp.where(lvl, amat[i], 0.0).astype(BF16), nh) for i in n]
        prod = [_dot(inv[i].astype(BF16), low[i]) for i in n]
        inv = [inv[i] - _dot(prod[i].astype(BF16), _block_diag(inv[i].astype(BF16), nh)) for i in n]
    eg = [jnp.exp(gc[i]) for i in n]
    rhs = [jnp.concatenate([betas[i] * cv[i], betas[i] * eg[i] * k[i]], axis=1) for i in n]
    uw = [_dot(inv[i].astype(BF16), _stack_heads(rhs[i].astype(BF16), hd, nh, period=GDN_QK)) for i in n]
    return [(uw[i][:, 0:256], uw[i][:, 256:512], q[i] * eg[i], qkk[i][0:ROWS] * decay[i],
             k[i] * jnp.exp(gtot[i] - gc[i]), jnp.exp(gtot[i])) for i in n]


def _gdn_scan(preps, sts, m, nseq):
    seg = ROWS // nseq
    n = range(len(preps))
    ws = [_dot1(jnp.concatenate([_widen(preps[i][1], nseq), _widen(preps[i][2], nseq)], axis=0), sts[i]) for i in n]
    u = [preps[i][0] - ws[i][0:ROWS] for i in n]
    outs = [ws[i][ROWS:2 * ROWS] + _dot1(preps[i][3], _stack_heads(u[i], GDN_DV, GDN_HEADS)) for i in n]
    new = []
    for i in n:
        dn = preps[i][5]
        dn_tall = jnp.concatenate(
            [jnp.broadcast_to(dn[j * seg:j * seg + 1], (GDN_QK, GDN_QK)) for j in range(nseq)], axis=0)
        new.append(sts[i] * dn_tall + _dot1(_widen(preps[i][4], nseq), u[i], _dot_tn) * m.bd_gdn)
    return outs, new


def _gdn_out(o, gn, zs):
    return o * lax.rsqrt(_group_mean(o * o, GDN_DV) + EPS) * gn * zs


def _cm_block(p, ws, bias, seg):
    r = p.shape[0]
    gu, vn = p[:, 0:256], p[:, 256:512]
    ri, ci = _iota2((r, CM_GROUPS * r))
    ci = _imod(ci, r)
    wm = jnp.where((_idiv(ri, seg) == _idiv(ci, seg)) & (ci <= ri), ws, 0.0)
    return gu * (_dot1(wm, _stack_heads(vn, BRANCH_W // CM_GROUPS, CM_GROUPS)) + bias)


def _conv_taps(cw_ref):
    return [cw_ref[GDN_CONV - 1 - d:GDN_CONV - d, :] for d in range(GDN_CONV)]


def _conv_prompt(x, hist, taps):
    n = x.shape[0]
    t8 = lax.broadcasted_iota(jnp.int32, (HIST, 1), 0)
    acc = taps[0] * x
    for d in range(1, GDN_CONV):
        xr = pltpu.roll(x, d, 0)
        head = jnp.where(t8 < d, pltpu.roll(hist, d, 0), xr[0:HIST])
        acc = acc + taps[d] * jnp.concatenate([head, xr[HIST:n]], axis=0)
    return acc


def _conv_sample(x, hist, taps, seg):
    n = x.shape[0]
    tloc = _imod(lax.broadcasted_iota(jnp.int32, (n, 1), 0), seg)
    acc = taps[0] * x
    for d in range(1, GDN_CONV):
        prev = jnp.where(tloc < d, pltpu.roll(hist, (d - seg) % n, 0), pltpu.roll(x, d, 0))
        acc = acc + taps[d] * prev
    return acc


def _block_rows(c):
    return slice(c * ROWS, (c + 1) * ROWS)


def _mixer_prompt_kernel(*refs):
    ns, nc, sr = PROMPT_SEQS, PROMPT_CHUNKS, PROMPT_SEQ_ROWS
    p_refs = refs[:ns]
    gn_gla_ref, gn_gdn_ref, cmw_ref, cmb_ref, o_ref, sgla_ref, sgdn_ref, st_gla, st_gdn = refs[ns:]

    @pl.when(pl.program_id(1) == 0)
    def _():
        st_gla[...] = jnp.zeros_like(st_gla)
        st_gdn[...] = jnp.zeros_like(st_gdn)

    m = _Masks(ROWS)
    g0 = P_GDN
    for s in range(ns):
        o_ref[s, :, 768:1024] = p_refs[s][:, g0 + 768:g0 + 1024]
    qkv = jnp.concatenate([p_refs[s][:, g0:g0 + 768] for s in range(ns)], axis=0)
    qn, kn = _gdn_qk_norm(qkv)
    blocks = [(s, c) for c in range(nc) for s in range(ns)]
    rows_of = lambda s, c: slice(s * sr + c * ROWS, s * sr + (c + 1) * ROWS)
    gdn = _gdn_prep([qn[rows_of(s, c)] for s, c in blocks], [kn[rows_of(s, c)] for s, c in blocks],
                    [qkv[rows_of(s, c), 512:768] for s, c in blocks],
                    [p_refs[s][_block_rows(c), g0 + 1280:g0 + 1536] for s, c in blocks],
                    [p_refs[s][_block_rows(c), g0 + 1536:g0 + 1792] for s, c in blocks], m, ROWS)
    gla = [_gla_prep(p_refs[s][_block_rows(c), P_GLA:P_GLA + 896], m, 1) for s, c in blocks]
    sg = [st_gla[s] for s in range(ns)]
    sd = [st_gdn[s] for s in range(ns)]
    og, od = {}, {}
    for c in range(nc):
        o, sg = _gla_scan(gla[c * ns:(c + 1) * ns], sg)
        og.update({(s, c): o[s] for s in range(ns)})
        o, sd = _gdn_scan(gdn[c * ns:(c + 1) * ns], sd, m, 1)
        od.update({(s, c): o[s] for s in range(ns)})
    by_rows = lambda d: jnp.concatenate([d[(s, c)] for s in range(ns) for c in range(nc)], axis=0)
    rs = jnp.concatenate([p_refs[s][:, 512:768] for s in range(ns)], axis=0)
    zs = jnp.concatenate([p_refs[s][:, g0 + 1024:g0 + 1280] for s in range(ns)], axis=0)
    o_gla = _gla_out(by_rows(og), gn_gla_ref[...], rs)
    o_gdn = _gdn_out(by_rows(od), gn_gdn_ref[...], zs)
    for s in range(ns):
        st_gla[s] = sg[s]
        st_gdn[s] = sd[s]
        o_ref[s, :, 0:256] = o_gla[s * sr:(s + 1) * sr]
        o_ref[s, :, 256:512] = o_gdn[s * sr:(s + 1) * sr]
        for c in range(sr // CM_CHUNK):
            rows = slice(c * CM_CHUNK, (c + 1) * CM_CHUNK)
            o_ref[s, rows, 512:768] = _cm_block(p_refs[s][rows, P_CM:P_CM + 512], cmw_ref[...], cmb_ref[...], CM_CHUNK)

    @pl.when(pl.program_id(1) == pl.num_programs(1) - 1)
    def _():
        for s in range(ns):
            sgla_ref[s] = _compact_state(sg[s], GLA_DV, GLA_HEADS)
            sgdn_ref[s] = _compact_state(sd[s], GDN_DV, GDN_HEADS)


def _mixer_prompt(p, gn_gla, gn_gdn, cmw, cmb, *, layer, nseqs, seq_len):
    ns, sr = PROMPT_SEQS, PROMPT_SEQ_ROWS
    nsteps = seq_len // sr
    seq_spec = lambda s: pl.BlockSpec((sr, P_COLS), lambda o, c: ((o * ns + s) * nsteps + c, 0))
    per_group = lambda *shape: pl.BlockSpec((ns,) + shape, lambda o, c: (o,) + (0,) * len(shape))
    per_layer = lambda *shape: _layer_spec(shape, layer)
    return pl.pallas_call(
        _mixer_prompt_kernel,
        grid=(nseqs // ns, nsteps),
        in_specs=[seq_spec(s) for s in range(ns)] + [
            per_layer(1, 256), per_layer(1, 256),
            per_layer(CM_CHUNK, CM_GROUPS * CM_CHUNK), per_layer(CM_CHUNK, 256)],
        out_specs=[pl.BlockSpec((ns, sr, 4 * BRANCH_W), lambda o, c: (o, c, 0)),
                   per_group(GLA_QK, GLA_DV), per_group(GDN_QK, GDN_DV)],
        out_shape=[jax.ShapeDtypeStruct((nseqs, seq_len, 4 * BRANCH_W), F32),
                   jax.ShapeDtypeStruct((nseqs, GLA_QK, GLA_DV), F32),
                   jax.ShapeDtypeStruct((nseqs, GDN_QK, GDN_DV), F32)],
        scratch_shapes=[pltpu.VMEM((ns, GLA_QK, GLA_HEADS * GLA_DV), F32), pltpu.VMEM((ns, GDN_QK, GDN_QK), F32)],
        compiler_params=pltpu.CompilerParams(dimension_semantics=("arbitrary", "arbitrary"),
                                             vmem_limit_bytes=VMEM_LIMIT),
        name="mixer_prompt",
    )(*([p] * ns), gn_gla, gn_gdn, cmw, cmb)


def _mixer_sample_kernel(p_ref, s0gla_ref, s0gdn_ref, gn_gla_ref, gn_gdn_ref, cmw_ref, cmb_ref,
                         o_ref, sgla_ref, sgdn_ref, *, seg):
    nseq = ROWS // seg
    nb = range(SAMPLE_BLOCKS_PER_STEP)
    m = _Masks(seg)
    g0 = P_GDN
    o_ref[:, 768:1024] = p_ref[:, g0 + 768:g0 + 1024]
    qkv = p_ref[:, g0:g0 + 768]
    qn, kn = _gdn_qk_norm(qkv)
    gdn = _gdn_prep([qn[_block_rows(c)] for c in nb], [kn[_block_rows(c)] for c in nb],
                    [qkv[_block_rows(c), 512:768] for c in nb],
                    [p_ref[_block_rows(c), g0 + 1280:g0 + 1536] for c in nb],
                    [p_ref[_block_rows(c), g0 + 1536:g0 + 1792] for c in nb], m, seg)
    gla = [_gla_prep(p_ref[_block_rows(c), P_GLA:P_GLA + 896], m, nseq) for c in nb]
    seqs = lambda c: slice(c * nseq, (c + 1) * nseq)
    sg = [_expand_state(s0gla_ref[seqs(c)].reshape(nseq * GLA_QK, GLA_DV), GLA_DV, GLA_HEADS) * m.bd_gla for c in nb]
    sd = [_expand_state(s0gdn_ref[seqs(c)].reshape(nseq * GDN_QK, GDN_DV), GDN_DV, GDN_HEADS) * m.bd_gdn for c in nb]
    og, sg = _gla_scan(gla, sg)
    od, sd = _gdn_scan(gdn, sd, m, nseq)
    for c in nb:
        rows = _block_rows(c)
        sgla_ref[seqs(c)] = _compact_state(sg[c], GLA_DV, GLA_HEADS).reshape(nseq, GLA_QK, GLA_DV)
        sgdn_ref[seqs(c)] = _compact_state(sd[c], GDN_DV, GDN_HEADS).reshape(nseq, GDN_QK, GDN_DV)
        o_ref[rows, 512:768] = _cm_block(p_ref[rows, P_CM:P_CM + 512], cmw_ref[...], cmb_ref[...], seg)
    o_ref[:, 0:256] = _gla_out(jnp.concatenate(og, axis=0), gn_gla_ref[...], p_ref[:, 512:768])
    o_ref[:, 256:512] = _gdn_out(jnp.concatenate(od, axis=0), gn_gdn_ref[...], p_ref[:, g0 + 1024:g0 + 1280])


def _mixer_sample(p, s0gla, s0gdn, gn_gla, gn_gdn, cmw, cmb, *, layer, base_step, nsteps, seg):
    rows = SAMPLE_STEP_ROWS
    nseq_step = rows // seg
    per_layer = lambda *shape: _layer_spec(shape, layer)
    return pl.pallas_call(
        functools.partial(_mixer_sample_kernel, seg=seg),
        grid=(nsteps,),
        in_specs=[pl.BlockSpec((rows, P_COLS), lambda i: (base_step + i, 0)),
                  pl.BlockSpec((None, nseq_step, GLA_QK, GLA_DV), lambda i: (layer, i, 0, 0)),
                  pl.BlockSpec((None, nseq_step, GDN_QK, GDN_DV), lambda i: (layer, i, 0, 0)),
                  per_layer(1, 256), per_layer(1, 256),
                  per_layer(ROWS, CM_GROUPS * ROWS), per_layer(ROWS, 256)],
        out_specs=[pl.BlockSpec((rows, 4 * BRANCH_W), lambda i: (i, 0)),
                   pl.BlockSpec((nseq_step, GLA_QK, GLA_DV), lambda i: (i, 0, 0)),
                   pl.BlockSpec((nseq_step, GDN_QK, GDN_DV), lambda i: (i, 0, 0))],
        out_shape=[jax.ShapeDtypeStruct((nsteps * rows, 4 * BRANCH_W), F32),
                   jax.ShapeDtypeStruct((nsteps * nseq_step, GLA_QK, GLA_DV), F32),
                   jax.ShapeDtypeStruct((nsteps * nseq_step, GDN_QK, GDN_DV), F32)],
        compiler_params=pltpu.CompilerParams(dimension_semantics=("parallel",), vmem_limit_bytes=VMEM_LIMIT),
        name="mixer_sample",
    )(p, s0gla, s0gdn, gn_gla, gn_gdn, cmw, cmb)


def _merge_kernel(*refs, prompt_tiles, pair):
    nh = 2 if pair else 1
    bp_ref, bs_ref, nw_ref, wg_ref, wb_ref, wo_ref, o_ref = refs[nh:]
    h = _residual_rows(refs[:nh], prompt_tiles)
    xn = _rms(h, nw_ref[...]).astype(BF16)
    br = _stream_rows(bp_ref, bs_ref, prompt_tiles).astype(BF16)
    merged = None
    for gi in range(N_BRANCH):
        gate = _sigmoid(_dot(xn, wg_ref[:, gi * D_MODEL:(gi + 1) * D_MODEL]))
        term = _dot(br[:, gi * BRANCH_W:(gi + 1) * BRANCH_W], wb_ref[gi]) * gate
        merged = term if merged is None else merged + term
    o_ref[...] = h + _dot(merged.astype(BF16), wo_ref[...])


def _stream_specs(tm, width, prompt_tiles, lead=()):
    nlead = (None,) * len(lead)
    return (pl.BlockSpec(nlead + (tm, width), lambda i: lead + (jnp.minimum(i, prompt_tiles - 1), 0)),
            pl.BlockSpec(nlead + (tm, width), lambda i: lead + (jnp.maximum(i - prompt_tiles, 0), 0)))


def _merge(h, br_p, br_s, nw, wg, wb, wo, *, layer):
    tm = TOKEN_TILE
    h_arrays, h_specs, _, ntok = _residual_specs(h, tm)
    prompt_tiles = br_p.shape[0] // tm
    row = lambda n: pl.BlockSpec((tm, n), lambda i: (i, 0))
    return pl.pallas_call(
        functools.partial(_merge_kernel, prompt_tiles=prompt_tiles, pair=isinstance(h, tuple)),
        grid=(ntok // tm,),
        in_specs=[*h_specs, *_stream_specs(tm, 4 * BRANCH_W, prompt_tiles), _layer_spec((1, D_MODEL), layer),
                  _layer_spec((D_MODEL, N_BRANCH * D_MODEL), layer),
                  _layer_spec((N_BRANCH, BRANCH_W, D_MODEL), layer), _layer_spec((D_MODEL, D_MODEL), layer)],
        out_specs=row(D_MODEL),
        out_shape=jax.ShapeDtypeStruct((ntok, D_MODEL), F32),
        compiler_params=pltpu.CompilerParams(dimension_semantics=("parallel",), vmem_limit_bytes=VMEM_LIMIT),
        name="merge",
    )(*h_arrays, br_p, br_s, nw, wg, wb, wo)


def _ffn_kernel(h_ref, pp_ref, ps_ref, nf_ref, wfg_ref, wfu_ref, wfd_ref, np_ref, wpg_ref, wp_ref, nfin_ref,
                *o_refs, final, prompt_tiles):
    h = h_ref[...]
    xf = _rms(h, nf_ref[...]).astype(BF16)
    act = _silu(_dot(xf, wfg_ref[...])) * _dot(xf, wfu_ref[...])
    h = h + _dot(act.astype(BF16), wfd_ref[...])
    pg = _sigmoid(_dot(_rms(h, np_ref[...]).astype(BF16), wpg_ref[...]))
    pe = _stream_rows(pp_ref, ps_ref, prompt_tiles).astype(BF16)
    h = h + pg * _dot(pe, wp_ref[...])
    if not final:
        o_refs[0][...] = h
        return
    out = _rms(h, nfin_ref[...])
    op_ref, os_ref = o_refs

    @pl.when(pl.program_id(0) < prompt_tiles)
    def _():
        op_ref[...] = out

    @pl.when(pl.program_id(0) >= prompt_tiles)
    def _():
        os_ref[...] = out


def _ffn(h, pe_p, pe_s, nf, wfg, wfu, wfd, npl, wpg, wp, nfin, *, layer, final):
    ntok = h.shape[0]
    tm = TOKEN_TILE
    npt = pe_p.shape[1]
    prompt_tiles = npt // tm
    row = lambda n: pl.BlockSpec((tm, n), lambda i: (i, 0))
    once = lambda shape: _layer_spec(shape, layer, buffers=1)
    if final:
        out_specs = list(_stream_specs(tm, D_MODEL, prompt_tiles))
        out_shape = [jax.ShapeDtypeStruct((npt, D_MODEL), F32), jax.ShapeDtypeStruct((ntok - npt, D_MODEL), F32)]
    else:
        out_specs, out_shape = row(D_MODEL), jax.ShapeDtypeStruct((ntok, D_MODEL), F32)
    return pl.pallas_call(
        functools.partial(_ffn_kernel, final=final, prompt_tiles=prompt_tiles),
        grid=(ntok // tm,),
        in_specs=[row(D_MODEL), *_stream_specs(tm, PLE_DIM, prompt_tiles, lead=(layer,)),
                  _layer_spec((1, D_MODEL), layer),
                  once((D_MODEL, D_FF)), once((D_MODEL, D_FF)), once((D_FF, D_MODEL)),
                  _layer_spec((1, D_MODEL), layer), once((D_MODEL, D_MODEL)), once((PLE_DIM, D_MODEL)),
                  _const_spec((1, D_MODEL))],
        out_specs=out_specs,
        out_shape=out_shape,
        compiler_params=pltpu.CompilerParams(dimension_semantics=("arbitrary",), vmem_limit_bytes=VMEM_LIMIT),
        name="ffn_final" if final else "ffn",
    )(h, pe_p, pe_s, nf, wfg, wfu, wfd, npl, wpg, wp, nfin)


def _split_w_in(w_in):
    span = lambda s: w_in[..., s[0]:s[1]].astype(BF16)
    rep = lambda o: jnp.repeat(w_in[..., o:o + GDN_HEADS], GDN_DK, axis=-1)
    small = jnp.concatenate([w_in[..., W_IN_GA:W_IN_GA + GLA_RANK],
                             jnp.zeros(w_in.shape[:-1] + (LANE - GLA_RANK,), w_in.dtype),
                             rep(W_IN_DA), rep(W_IN_DB)], axis=-1).astype(BF16)
    return span(W_IN_GLA), span(W_IN_GDN), span(W_IN_REST), small


def kernel(x_prompt, x_sample, state_gla, state_gdn, state_gdn_conv, state_sconv, p_prompt, p_sample, norm_mix, w_in, gla_wa2, gla_ba, gla_norm, gdn_conv_w, gdn_a_log, gdn_dt_bias, gdn_norm, cm_ln_g, cm_ln_b, cm_ws, cm_bs, sc_conv_w, w_gate, w_branch, w_o, norm_ffn, w_ffn_gate, w_ffn_up, w_ffn_down, norm_ple, w_ple_gate, w_ple, norm_final):
    depth = w_in.shape[0]
    bp, tp, _ = x_prompt.shape
    bs, ts, _ = x_sample.shape
    npt, nst = bp * tp, bs * ts
    sseq = ROWS // ts
    assert ts == HIST and nst % SAMPLE_STEP_ROWS == 0
    assert bp % PROMPT_SEQS == 0 and tp % PROMPT_SEQ_ROWS == 0 and PROMPT_SEQ_ROWS % CM_CHUNK == 0
    assert tp % TOKEN_TILE == 0 and nst % TOKEN_TILE == 0 and TOKEN_TILE % ts == 0

    h = (x_prompt.reshape(npt, D_MODEL), x_sample.reshape(nst, D_MODEL))
    pe_p = p_prompt.reshape(depth, npt, PLE_DIM)
    pe_s = p_sample.reshape(depth, nst, PLE_DIM)
    s0_gla = state_gla.reshape(depth, bs, GLA_QK, GLA_DV)
    s0_gdn = state_gdn.reshape(depth, bs, GDN_QK, GDN_DV)
    rows = lambda a: a.reshape(depth, 1, -1)
    w_in_groups = _split_w_in(w_in)
    wg, wb, wo = w_gate.astype(BF16), w_branch.astype(BF16), w_o.astype(BF16)
    wfg, wfu, wfd = w_ffn_gate.astype(BF16), w_ffn_up.astype(BF16), w_ffn_down.astype(BF16)
    wpg, wp = w_ple_gate.astype(BF16), w_ple.astype(BF16)
    wa2 = jnp.pad(gla_wa2, ((0, 0), (0, LANE - GLA_RANK), (0, 0))).astype(BF16)
    nmix, nffn, nple = rows(norm_mix), rows(norm_ffn), rows(norm_ple)
    inproj_vecs = (rows(gla_ba), rows(jnp.repeat(gdn_a_log, GDN_DK, axis=1)),
                   rows(jnp.repeat(gdn_dt_bias, GDN_DK, axis=1)), rows(cm_ln_g), rows(cm_ln_b))
    cw = jnp.concatenate([gdn_conv_w, jnp.pad(sc_conv_w, ((0, 0), (GDN_CONV - SC_WIDTH, 0), (0, 0)))], axis=2)
    hist_s = jnp.concatenate([
        jnp.pad(state_gdn_conv, ((0, 0), (0, 0), (HIST - (GDN_CONV - 1), 0), (0, 0))),
        jnp.pad(state_sconv, ((0, 0), (0, 0), (HIST - (SC_WIDTH - 1), 0), (0, 0)))], axis=3).reshape(depth, nst, CONV_W)
    gn_gla = rows(jnp.tile(gla_norm, (1, GLA_HEADS)))
    gn_gdn = rows(jnp.tile(gdn_norm, (1, GDN_HEADS)))
    cmw_p = jnp.transpose(cm_ws, (0, 2, 1, 3)).reshape(depth, CM_CHUNK, CM_GROUPS * CM_CHUNK)
    cmb_p = jnp.repeat(jnp.swapaxes(cm_bs, 1, 2), BRANCH_W // CM_GROUPS, axis=2)
    cmw_s = jnp.transpose(jnp.tile(cm_ws[:, :, :ts, :ts], (1, 1, sseq, sseq)),
                          (0, 2, 1, 3)).reshape(depth, ROWS, CM_GROUPS * ROWS)
    cmb_s = jnp.tile(cmb_p[:, :ts], (1, sseq, 1))

    outs = {k: [] for k in ("gla_p", "gla_s", "gdn_p", "gdn_s", "gc_p", "gc_s", "sc_p", "sc_s", "cv_s")}
    for i in range(depth):
        h_p, h_s, first_s = (h[0], h[1], 0) if isinstance(h, tuple) else (h, h, npt // TOKEN_TILE)
        inproj_args = (hist_s, nmix, *w_in_groups, wa2, *inproj_vecs, cw)
        p_p, tails = _inproj(h_p, 0, npt, *inproj_args, layer=i, sample=False, seq_len=tp, seg=ts)
        p_s, xs_raw = _inproj(h_s, first_s, nst, *inproj_args, layer=i, sample=True, seq_len=tp, seg=ts)
        br_p, gla_p, gdn_p = _mixer_prompt(p_p, gn_gla, gn_gdn, cmw_p, cmb_p, layer=i, nseqs=bp, seq_len=tp)
        br_p = br_p.reshape(npt, N_BRANCH * BRANCH_W)
        br_s, gla_s, gdn_s = _mixer_sample(p_s, s0_gla, s0_gdn, gn_gla, gn_gdn, cmw_s, cmb_s, layer=i,
                                           base_step=0, nsteps=nst // SAMPLE_STEP_ROWS, seg=ts)
        outs["gla_p"].append(gla_p.reshape(bp, GLA_HEADS, GLA_DK, GLA_DV))
        outs["gla_s"].append(gla_s.reshape(bs, GLA_HEADS, GLA_DK, GLA_DV))
        outs["gdn_p"].append(gdn_p.reshape(bp, GDN_HEADS, GDN_DK, GDN_DV))
        outs["gdn_s"].append(gdn_s.reshape(bs, GDN_HEADS, GDN_DK, GDN_DV))
        tiles_per_seq = tp // TOKEN_TILE
        tail_p = tails[tiles_per_seq - 1:bp * tiles_per_seq:tiles_per_seq]
        xs3 = xs_raw.reshape(bs, ts, CONV_W)
        outs["gc_p"].append(tail_p[:, HIST - (GDN_CONV - 1):, 0:768])
        outs["gc_s"].append(xs3[:, ts - (GDN_CONV - 1):, 0:768])
        outs["sc_p"].append(tail_p[:, HIST - (SC_WIDTH - 1):, 768:])
        outs["sc_s"].append(xs3[:, ts - (SC_WIDTH - 1):, 768:])
        outs["cv_s"].append(p_s[:, P_CM + 256:P_CM + 512].reshape(bs, ts, BRANCH_W))

        h1 = _merge(h, br_p, br_s, nmix, wg, wb, wo, layer=i)
        h = _ffn(h1, pe_p, pe_s, nffn, wfg, wfu, wfd, nple, wpg, wp, norm_final.reshape(1, D_MODEL),
                 layer=i, final=(i == depth - 1))

    y_prompt = h[0].reshape(bp, tp, D_MODEL)
    y_sample = h[1].reshape(bs, ts, D_MODEL)
    st = lambda k: jnp.stack(outs[k])
    return (y_prompt, y_sample, st("gla_p"), st("gla_s"), st("gdn_p"), st("gdn_s"),
            st("gc_p"), st("gc_s"), st("sc_p"), st("sc_s"), st("cv_s"))
```
